```python
import math
import jax, jax.numpy as jnp
from jax import lax
import numpy as np

D_MODEL = 1024
BATCH = 8
SEQ = 16384
DEPTH = 2

N_Q = 8
N_KV = 2
GROUP = N_Q // N_KV
HEAD_DIM = 64
ATTN_W = N_Q * HEAD_DIM
KV_W = N_KV * HEAD_DIM
WINDOW = 128
BLOCK = 128
CONV_C = D_MODEL // 2
CONV_K = 31
D_FF = 4 * D_MODEL
IN_W = ATTN_W + 2 * KV_W + 2 * CONV_C + 2 * D_MODEL
EPS = 1e-6
NEG = -1e30

kernel_name = "hybrid_swa_sink_alibi_conformer_conv_gated_block"


def rms_norm(x, g):
    xf = x.astype(jnp.float32)
    y = xf * lax.rsqrt(jnp.mean(xf * xf, axis=-1, keepdims=True) + EPS)
    return (y * g.astype(jnp.float32)).astype(x.dtype)


def layer_norm(x, g, b):
    xf = x.astype(jnp.float32)
    mu = jnp.mean(xf, axis=-1, keepdims=True)
    xc = xf - mu
    var = jnp.mean(xc * xc, axis=-1, keepdims=True)
    y = xc * lax.rsqrt(var + EPS) * g.astype(jnp.float32) + b.astype(jnp.float32)
    return y.astype(x.dtype)


def alibi_slopes():
    return jnp.asarray(2.0 ** (-8.0 * np.arange(1, N_Q + 1) / N_Q), dtype=jnp.float32)


def sliding_window_attention(q, k, v, sinks):
    B, S = q.shape[0], q.shape[1]
    nb = S // BLOCK
    qb = q.reshape(B, nb, BLOCK, N_KV, GROUP, HEAD_DIM)

    def band(t):
        t = t.reshape(B, S, N_KV, HEAD_DIM)
        tp = jnp.pad(t, ((0, 0), (BLOCK, 0), (0, 0), (0, 0))).reshape(B, nb + 1, BLOCK, N_KV, HEAD_DIM)
        return jnp.concatenate([tp[:, :-1], tp[:, 1:]], axis=2)

    kb, vb = band(k), band(v)
    scale = 1.0 / math.sqrt(HEAD_DIM)
    s = jnp.einsum('bnqkgd,bnskd->bkgnqs', qb, kb).astype(jnp.float32) * scale

    qi = jnp.arange(BLOCK)[:, None] + BLOCK
    kj = jnp.arange(2 * BLOCK)[None, :]
    dist = qi - kj
    key_pos = jnp.arange(nb)[:, None, None] * BLOCK + kj[None] - BLOCK
    valid = (dist >= 0)[None] & (dist < WINDOW)[None] & (key_pos >= 0)

    slopes = alibi_slopes().reshape(N_KV, GROUP)
    s = s - slopes[None, :, :, None, None, None] * dist.astype(jnp.float32)
    s = jnp.where(valid, s, NEG)

    sink = sinks.astype(jnp.float32).reshape(N_KV, GROUP)[None, :, :, None, None]
    m = jnp.maximum(jnp.max(s, axis=-1), sink)
    p = jnp.exp(s - m[..., None])
    denom = jnp.sum(p, axis=-1) + jnp.exp(sink - m)
    p = p / denom[..., None]
    o = jnp.einsum('bkgnqs,bnskd->bnqkgd', p.astype(v.dtype), vb)
    return o.reshape(B, S, ATTN_W)


def causal_depthwise_conv(u, w, b):
    y = lax.conv_general_dilated(
        u, w[:, None, :].astype(u.dtype), window_strides=(1,), padding=[(CONV_K - 1, 0)],
        dimension_numbers=('NWC', 'WIO', 'NWC'), feature_group_count=CONV_C)
    return y + b


def _fwd_setup_inputs(seed: int = 0) -> dict:
    key = jax.random.key(seed)
    ks = jax.random.split(key, 20)
    f = jnp.float32
    nrm = lambda k, shp, s: jax.random.normal(k, shp, f) * s
    return {
        "x": nrm(ks[0], (BATCH, SEQ, D_MODEL), 1.0),
        "mix_norm_g": 1.0 + nrm(ks[1], (DEPTH, D_MODEL), 0.01),
        "w_in": nrm(ks[2], (DEPTH, D_MODEL, IN_W), D_MODEL ** -0.5),
        "b_in": nrm(ks[3], (DEPTH, IN_W), 0.02),
        "sinks": nrm(ks[4], (DEPTH, N_Q), 0.5),
        "conv_w": nrm(ks[5], (DEPTH, CONV_K, CONV_C), CONV_K ** -0.5),
        "conv_b": nrm(ks[6], (DEPTH, CONV_C), 0.02),
        "conv_ln_g": 1.0 + nrm(ks[7], (DEPTH, CONV_C), 0.01),
        "conv_ln_b": nrm(ks[8], (DEPTH, CONV_C), 0.01),
        "w_attn_proj": nrm(ks[9], (DEPTH, ATTN_W, D_MODEL), ATTN_W ** -0.5),
        "w_conv_proj": nrm(ks[10], (DEPTH, CONV_C, D_MODEL), CONV_C ** -0.5),
        "b_conv_proj": nrm(ks[11], (DEPTH, D_MODEL), 0.02),
        "w_out": nrm(ks[12], (DEPTH, D_MODEL, D_MODEL), D_MODEL ** -0.5),
        "mlp_norm_g": 1.0 + nrm(ks[13], (DEPTH, D_MODEL), 0.01),
        "w_mlp1": nrm(ks[14], (DEPTH, D_MODEL, D_FF), D_MODEL ** -0.5),
        "w_mlp2": nrm(ks[15], (DEPTH, D_FF, D_MODEL), D_FF ** -0.5),
        "final_norm_g": 1.0 + nrm(ks[16], (D_MODEL,), 0.01),
    }


def _fwd_reference(x, mix_norm_g, w_in, b_in, sinks, conv_w, conv_b, conv_ln_g, conv_ln_b,
              w_attn_proj, w_conv_proj, b_conv_proj, w_out, mlp_norm_g, w_mlp1, w_mlp2,
              final_norm_g):
    splits = np.cumsum([ATTN_W, KV_W, KV_W, CONV_C, CONV_C, D_MODEL]).tolist()
    for l in range(DEPTH):
        h = rms_norm(x, mix_norm_g[l])
        proj = jnp.einsum('bsd,de->bse', h, w_in[l]) + b_in[l]
        q, k, v, glu_a, glu_b, gate_a, gate_c = jnp.split(proj, splits, axis=-1)

        attn = sliding_window_attention(q, k, v, sinks[l])
        br_a = jnp.einsum('bse,ed->bsd', attn, w_attn_proj[l])

        u = glu_a * jax.nn.sigmoid(glu_b)
        u = causal_depthwise_conv(u, conv_w[l], conv_b[l])
        u = jax.nn.silu(layer_norm(u, conv_ln_g[l], conv_ln_b[l]))
        br_c = jnp.einsum('bsc,cd->bsd', u, w_conv_proj[l]) + b_conv_proj[l]

        merged = jax.nn.sigmoid(gate_a) * br_a + jax.nn.sigmoid(gate_c) * br_c
        x = x + jnp.einsum('bsd,de->bse', merged, w_out[l])

        h2 = rms_norm(x, mlp_norm_g[l])
        a = jnp.square(jax.nn.relu(jnp.einsum('bsd,df->bsf', h2, w_mlp1[l])))
        x = x + jnp.einsum('bsf,fd->bsd', a, w_mlp2[l])
    return rms_norm(x, final_norm_g)


import jax as _jax
import jax.numpy as _jnp

TWIN_FORMAT = 'train_step'
FWD_PARAMS = ['x', 'mix_norm_g', 'w_in', 'b_in', 'sinks', 'conv_w', 'conv_b', 'conv_ln_g', 'conv_ln_b', 'w_attn_proj', 'w_conv_proj', 'b_conv_proj', 'w_out', 'mlp_norm_g', 'w_mlp1', 'w_mlp2', 'final_norm_g']
TWIN_WEIGHTS = ['mix_norm_g', 'w_in', 'b_in', 'sinks', 'conv_w', 'conv_b', 'conv_ln_g', 'conv_ln_b', 'w_attn_proj', 'w_conv_proj', 'b_conv_proj', 'w_out', 'mlp_norm_g', 'w_mlp1', 'w_mlp2', 'final_norm_g']
TWIN_DIFF_INPUT = 'x'
TWIN_INPUTS = ['x', 'mix_norm_g', 'w_in', 'b_in', 'sinks', 'conv_w', 'conv_b', 'conv_ln_g', 'conv_ln_b', 'w_attn_proj', 'w_conv_proj', 'b_conv_proj', 'w_out', 'mlp_norm_g', 'w_mlp1', 'w_mlp2', 'final_norm_g', 'loss_target', 'm_mix_norm_g', 'm_w_in', 'm_b_in', 'm_sinks', 'm_conv_w', 'm_conv_b', 'm_conv_ln_g', 'm_conv_ln_b', 'm_w_attn_proj', 'm_w_conv_proj', 'm_b_conv_proj', 'm_w_out', 'm_mlp_norm_g', 'm_w_mlp1', 'm_w_mlp2', 'm_final_norm_g', 'v_mix_norm_g', 'v_w_in', 'v_b_in', 'v_sinks', 'v_conv_w', 'v_conv_b', 'v_conv_ln_g', 'v_conv_ln_b', 'v_w_attn_proj', 'v_w_conv_proj', 'v_b_conv_proj', 'v_w_out', 'v_mlp_norm_g', 'v_w_mlp1', 'v_w_mlp2', 'v_final_norm_g']
TWIN_OUTPUTS = ['loss', 'grad_x', 'grad_mix_norm_g', 'grad_w_in', 'grad_b_in', 'grad_sinks', 'grad_conv_w', 'grad_conv_b', 'grad_conv_ln_g', 'grad_conv_ln_b', 'grad_w_attn_proj', 'grad_w_conv_proj', 'grad_b_conv_proj', 'grad_w_out', 'grad_mlp_norm_g', 'grad_w_mlp1', 'grad_w_mlp2', 'grad_final_norm_g', 'delta_mix_norm_g', 'delta_w_in', 'delta_b_in', 'delta_sinks', 'delta_conv_w', 'delta_conv_b', 'delta_conv_ln_g', 'delta_conv_ln_b', 'delta_w_attn_proj', 'delta_w_conv_proj', 'delta_b_conv_proj', 'delta_w_out', 'delta_mlp_norm_g', 'delta_w_mlp1', 'delta_w_mlp2', 'delta_final_norm_g', 'new_m_mix_norm_g', 'new_m_w_in', 'new_m_b_in', 'new_m_sinks', 'new_m_conv_w', 'new_m_conv_b', 'new_m_conv_ln_g', 'new_m_conv_ln_b', 'new_m_w_attn_proj', 'new_m_w_conv_proj', 'new_m_b_conv_proj', 'new_m_w_out', 'new_m_mlp_norm_g', 'new_m_w_mlp1', 'new_m_w_mlp2', 'new_m_final_norm_g', 'new_v_mix_norm_g', 'new_v_w_in', 'new_v_b_in', 'new_v_sinks', 'new_v_conv_w', 'new_v_conv_b', 'new_v_conv_ln_g', 'new_v_conv_ln_b', 'new_v_w_attn_proj', 'new_v_w_conv_proj', 'new_v_b_conv_proj', 'new_v_w_out', 'new_v_mlp_norm_g', 'new_v_w_mlp1', 'new_v_w_mlp2', 'new_v_final_norm_g']
TWIN_LEAF_KINDS = {'loss': 'loss', 'grad_x': 'grad_x', 'grad_mix_norm_g': 'grad_w', 'grad_w_in': 'grad_w', 'grad_b_in': 'grad_w', 'grad_sinks': 'grad_w', 'grad_conv_w': 'grad_w', 'grad_conv_b': 'grad_w', 'grad_conv_ln_g': 'grad_w', 'grad_conv_ln_b': 'grad_w', 'grad_w_attn_proj': 'grad_w', 'grad_w_conv_proj': 'grad_w', 'grad_b_conv_proj': 'grad_w', 'grad_w_out': 'grad_w', 'grad_mlp_norm_g': 'grad_w', 'grad_w_mlp1': 'grad_w', 'grad_w_mlp2': 'grad_w', 'grad_final_norm_g': 'grad_w', 'delta_mix_norm_g': 'delta_w', 'delta_w_in': 'delta_w', 'delta_b_in': 'delta_w', 'delta_sinks': 'delta_w', 'delta_conv_w': 'delta_w', 'delta_conv_b': 'delta_w', 'delta_conv_ln_g': 'delta_w', 'delta_conv_ln_b': 'delta_w', 'delta_w_attn_proj': 'delta_w', 'delta_w_conv_proj': 'delta_w', 'delta_b_conv_proj': 'delta_w', 'delta_w_out': 'delta_w', 'delta_mlp_norm_g': 'delta_w', 'delta_w_mlp1': 'delta_w', 'delta_w_mlp2': 'delta_w', 'delta_final_norm_g': 'delta_w', 'new_m_mix_norm_g': 'new_m', 'new_m_w_in': 'new_m', 'new_m_b_in': 'new_m', 'new_m_sinks': 'new_m', 'new_m_conv_w': 'new_m', 'new_m_conv_b': 'new_m', 'new_m_conv_ln_g': 'new_m', 'new_m_conv_ln_b': 'new_m', 'new_m_w_attn_proj': 'new_m', 'new_m_w_conv_proj': 'new_m', 'new_m_b_conv_proj': 'new_m', 'new_m_w_out': 'new_m', 'new_m_mlp_norm_g': 'new_m', 'new_m_w_mlp1': 'new_m', 'new_m_w_mlp2': 'new_m', 'new_m_final_norm_g': 'new_m', 'new_v_mix_norm_g': 'new_v', 'new_v_w_in': 'new_v', 'new_v_b_in': 'new_v', 'new_v_sinks': 'new_v', 'new_v_conv_w': 'new_v', 'new_v_conv_b': 'new_v', 'new_v_conv_ln_g': 'new_v', 'new_v_conv_ln_b': 'new_v', 'new_v_w_attn_proj': 'new_v', 'new_v_w_conv_proj': 'new_v', 'new_v_b_conv_proj': 'new_v', 'new_v_w_out': 'new_v', 'new_v_mlp_norm_g': 'new_v', 'new_v_w_mlp1': 'new_v', 'new_v_w_mlp2': 'new_v', 'new_v_final_norm_g': 'new_v'}


def _forward(args):
    return _fwd_reference(*[args[k] for k in FWD_PARAMS])


def _output_shape():
    def fwd():
        inp = _fwd_setup_inputs(0)
        return _fwd_reference(*[inp[k] for k in FWD_PARAMS])
    out = _jax.eval_shape(fwd)
    return out.shape, out.dtype

N_MICROBATCH = 1
ADAM_LR = 0.001
ADAM_B1 = 0.9
ADAM_B2 = 0.999
ADAM_EPS = 1e-08
ADAM_WD = 0.01
ADAM_STEP = 10
PER_EXAMPLE_BATCH_AXIS = {'x': 0, 'loss_target': 0}
SHARED_INPUTS = []
_WEIGHT_DTYPES = {'mix_norm_g': _jnp.float32, 'w_in': _jnp.float32, 'b_in': _jnp.float32, 'sinks': _jnp.float32, 'conv_w': _jnp.float32, 'conv_b': _jnp.float32, 'conv_ln_g': _jnp.float32, 'conv_ln_b': _jnp.float32, 'w_attn_proj': _jnp.float32, 'w_conv_proj': _jnp.float32, 'b_conv_proj': _jnp.float32, 'w_out': _jnp.float32, 'mlp_norm_g': _jnp.float32, 'w_mlp1': _jnp.float32, 'w_mlp2': _jnp.float32, 'final_norm_g': _jnp.float32}
MOMENT_SCALE = {'mix_norm_g': 1.518794e-01, 'w_in': 7.622861e-02, 'b_in': 1.468251e-01, 'sinks': 8.123164e-02, 'conv_w': 1.515888e-01, 'conv_b': 3.307332e-01, 'conv_ln_g': 1.824792e-01, 'conv_ln_b': 1.717914e-01, 'w_attn_proj': 5.805001e-02, 'w_conv_proj': 1.079951e-01, 'b_conv_proj': 2.327085e-01, 'w_out': 1.222370e-01, 'mlp_norm_g': 2.885128e-01, 'w_mlp1': 1.460712e-01, 'w_mlp2': 2.820686e-01, 'final_norm_g': 1.305220e+02}


def _to_microbatches(a, axis):
    t = _jnp.moveaxis(a, axis, 0)
    t = t.reshape((N_MICROBATCH, t.shape[0] // N_MICROBATCH) + t.shape[1:])
    return _jnp.moveaxis(t, 1, axis + 1)


def setup_inputs(seed: int = 0) -> dict:
    inp = _fwd_setup_inputs(seed)
    key = _jax.random.fold_in(_jax.random.key(seed), 7919)
    shape, _ = _output_shape()
    out = dict(inp)
    out["loss_target"] = _jax.random.normal(_jax.random.fold_in(key, 0), shape, _jnp.float32)
    for i, name in enumerate(TWIN_WEIGHTS):
        w = inp[name].astype(_jnp.float32)
        if MOMENT_SCALE is None:
            s = _jnp.sqrt(_jnp.mean(_jnp.square(w)) + 1e-30)
        else:
            s = MOMENT_SCALE[name]
        km, kv = _jax.random.split(_jax.random.fold_in(key, i + 1))
        out[name] = w
        out["m_" + name] = s * _jax.random.normal(km, w.shape, _jnp.float32)
        out["v_" + name] = (s * s) * _jax.random.uniform(kv, w.shape, _jnp.float32, 0.5, 1.5)
    if N_MICROBATCH > 1:
        for name, axis in PER_EXAMPLE_BATCH_AXIS.items():
            out[name] = _to_microbatches(out[name], axis)
    return {'x': out['x'], 'mix_norm_g': out['mix_norm_g'], 'w_in': out['w_in'], 'b_in': out['b_in'], 'sinks': out['sinks'], 'conv_w': out['conv_w'], 'conv_b': out['conv_b'], 'conv_ln_g': out['conv_ln_g'], 'conv_ln_b': out['conv_ln_b'], 'w_attn_proj': out['w_attn_proj'], 'w_conv_proj': out['w_conv_proj'], 'b_conv_proj': out['b_conv_proj'], 'w_out': out['w_out'], 'mlp_norm_g': out['mlp_norm_g'], 'w_mlp1': out['w_mlp1'], 'w_mlp2': out['w_mlp2'], 'final_norm_g': out['final_norm_g'], 'loss_target': out['loss_target'], 'm_mix_norm_g': out['m_mix_norm_g'], 'm_w_in': out['m_w_in'], 'm_b_in': out['m_b_in'], 'm_sinks': out['m_sinks'], 'm_conv_w': out['m_conv_w'], 'm_conv_b': out['m_conv_b'], 'm_conv_ln_g': out['m_conv_ln_g'], 'm_conv_ln_b': out['m_conv_ln_b'], 'm_w_attn_proj': out['m_w_attn_proj'], 'm_w_conv_proj': out['m_w_conv_proj'], 'm_b_conv_proj': out['m_b_conv_proj'], 'm_w_out': out['m_w_out'], 'm_mlp_norm_g': out['m_mlp_norm_g'], 'm_w_mlp1': out['m_w_mlp1'], 'm_w_mlp2': out['m_w_mlp2'], 'm_final_norm_g': out['m_final_norm_g'], 'v_mix_norm_g': out['v_mix_norm_g'], 'v_w_in': out['v_w_in'], 'v_b_in': out['v_b_in'], 'v_sinks': out['v_sinks'], 'v_conv_w': out['v_conv_w'], 'v_conv_b': out['v_conv_b'], 'v_conv_ln_g': out['v_conv_ln_g'], 'v_conv_ln_b': out['v_conv_ln_b'], 'v_w_attn_proj': out['v_w_attn_proj'], 'v_w_conv_proj': out['v_w_conv_proj'], 'v_b_conv_proj': out['v_b_conv_proj'], 'v_w_out': out['v_w_out'], 'v_mlp_norm_g': out['v_mlp_norm_g'], 'v_w_mlp1': out['v_w_mlp1'], 'v_w_mlp2': out['v_w_mlp2'], 'v_final_norm_g': out['v_final_norm_g']}


def _loss(weights, diff, rest, loss_target):
    with _jax.named_scope("forward"):
        args = {**rest, TWIN_DIFF_INPUT: diff, **{k: w.astype(_WEIGHT_DTYPES[k]) for k, w in weights.items()}}
        y = _forward(args)
    with _jax.named_scope("loss_head"):
        err = _jnp.square(y.astype(_jnp.float32) - loss_target)
        return 0.5 * _jnp.sum(_jnp.mean(err, axis=-1)) if err.ndim else 0.5 * err


def _adamw(w, g, m, v):
    m = ADAM_B1 * m + (1.0 - ADAM_B1) * g
    v = ADAM_B2 * v + (1.0 - ADAM_B2) * _jnp.square(g)
    m_hat = m / (1.0 - ADAM_B1 ** ADAM_STEP)
    v_hat = v / (1.0 - ADAM_B2 ** ADAM_STEP)
    delta = -ADAM_LR * (m_hat / (_jnp.sqrt(v_hat) + ADAM_EPS) + ADAM_WD * w)
    return delta, m, v


def reference(x, mix_norm_g, w_in, b_in, sinks, conv_w, conv_b, conv_ln_g, conv_ln_b, w_attn_proj, w_conv_proj, b_conv_proj, w_out, mlp_norm_g, w_mlp1, w_mlp2, final_norm_g, loss_target, m_mix_norm_g, m_w_in, m_b_in, m_sinks, m_conv_w, m_conv_b, m_conv_ln_g, m_conv_ln_b, m_w_attn_proj, m_w_conv_proj, m_b_conv_proj, m_w_out, m_mlp_norm_g, m_w_mlp1, m_w_mlp2, m_final_norm_g, v_mix_norm_g, v_w_in, v_b_in, v_sinks, v_conv_w, v_conv_b, v_conv_ln_g, v_conv_ln_b, v_w_attn_proj, v_w_conv_proj, v_b_conv_proj, v_w_out, v_mlp_norm_g, v_w_mlp1, v_w_mlp2, v_final_norm_g):
    given = dict(x=x, mix_norm_g=mix_norm_g, w_in=w_in, b_in=b_in, sinks=sinks, conv_w=conv_w, conv_b=conv_b, conv_ln_g=conv_ln_g, conv_ln_b=conv_ln_b, w_attn_proj=w_attn_proj, w_conv_proj=w_conv_proj, b_conv_proj=b_conv_proj, w_out=w_out, mlp_norm_g=mlp_norm_g, w_mlp1=w_mlp1, w_mlp2=w_mlp2, final_norm_g=final_norm_g, loss_target=loss_target, m_mix_norm_g=m_mix_norm_g, m_w_in=m_w_in, m_b_in=m_b_in, m_sinks=m_sinks, m_conv_w=m_conv_w, m_conv_b=m_conv_b, m_conv_ln_g=m_conv_ln_g, m_conv_ln_b=m_conv_ln_b, m_w_attn_proj=m_w_attn_proj, m_w_conv_proj=m_w_conv_proj, m_b_conv_proj=m_b_conv_proj, m_w_out=m_w_out, m_mlp_norm_g=m_mlp_norm_g, m_w_mlp1=m_w_mlp1, m_w_mlp2=m_w_mlp2, m_final_norm_g=m_final_norm_g, v_mix_norm_g=v_mix_norm_g, v_w_in=v_w_in, v_b_in=v_b_in, v_sinks=v_sinks, v_conv_w=v_conv_w, v_conv_b=v_conv_b, v_conv_ln_g=v_conv_ln_g, v_conv_ln_b=v_conv_ln_b, v_w_attn_proj=v_w_attn_proj, v_w_conv_proj=v_w_conv_proj, v_b_conv_proj=v_b_conv_proj, v_w_out=v_w_out, v_mlp_norm_g=v_mlp_norm_g, v_w_mlp1=v_w_mlp1, v_w_mlp2=v_w_mlp2, v_final_norm_g=v_final_norm_g)
    weights = {n: given[n] for n in TWIN_WEIGHTS}
    shared = {n: given[n] for n in SHARED_INPUTS}
    per_example = {n: given[n] for n in ['x']}
    grad_fn = _jax.value_and_grad(_loss, argnums=(0, 1))

    def one_microbatch(ex, loss_target):
        ex = dict(ex)
        diff = ex.pop(TWIN_DIFF_INPUT)
        return grad_fn(weights, diff, {**shared, **ex}, loss_target)

    if N_MICROBATCH == 1:
        loss, (grad_w, grad_x) = one_microbatch(per_example, given["loss_target"])
    else:
        def body(carry, xs):
            loss_sum, grad_sum = carry
            l_k, (gw_k, gx_k) = one_microbatch(xs[0], xs[1])
            with _jax.named_scope("update"):
                return (loss_sum + l_k, _jax.tree.map(_jnp.add, grad_sum, gw_k)), gx_k

        init = (_jnp.zeros((), _jnp.float32), _jax.tree.map(_jnp.zeros_like, weights))
        (loss, grad_w), grad_x = _jax.lax.scan(body, init, (per_example, given["loss_target"]))
    with _jax.named_scope("update"):
        delta_w, new_m, new_v = {}, {}, {}
        for n in TWIN_WEIGHTS:
            delta_w[n], new_m[n], new_v[n] = _adamw(weights[n], grad_w[n], given["m_" + n], given["v_" + n])
    return (loss, grad_x, *[grad_w[n] for n in TWIN_WEIGHTS], *[delta_w[n] for n in TWIN_WEIGHTS],
            *[new_m[n] for n in TWIN_WEIGHTS], *[new_v[n] for n in TWIN_WEIGHTS])
```

```python
import functools
import math

import jax
import jax.numpy as jnp
from jax import lax
from jax.experimental import pallas as pl
from jax.experimental.pallas import tpu as pltpu

D_MODEL = 1024
SEQ = 16384
DEPTH = 2
N_Q = 8
N_KV = 2
GROUP = N_Q // N_KV
HEAD_DIM = 64
ATTN_W = N_Q * HEAD_DIM
KV_W = N_KV * HEAD_DIM
BLOCK = 128
CONV_C = D_MODEL // 2
CONV_K = 31
D_FF = 4 * D_MODEL
QKV_W = ATTN_W + 2 * KV_W
IN_W = QKV_W + 2 * CONV_C + 2 * D_MODEL
EPS = 1e-6
NEG = -1e30
N_DEV = 8

ADAM_LR = 0.001
ADAM_B1 = 0.9
ADAM_B2 = 0.999
ADAM_EPS = 1e-08
ADAM_WD = 0.01
ADAM_STEP = 10

F32 = jnp.float32
BF16 = jnp.bfloat16
MESH = pl.DeviceIdType.MESH

SUBLANES = 8
HALO = 32
CONV_ROWS = 64
VMEM_LIMIT = 52 * 1024 * 1024

_NT = (((1,), (1,)), ((), ()))
_TN = (((0,), (0,)), ((), ()))


def _params(*sem):
    return pltpu.CompilerParams(dimension_semantics=sem, vmem_limit_bytes=VMEM_LIMIT)


def _tile(n, pref):
    t = min(n, pref)
    assert n % t == 0, (n, t)
    return t


def _sigmoid(v):
    return 1.0 / (1.0 + jnp.exp(-v))


def _rows8(v):
    r, n = v.shape
    return jnp.sum(v.reshape(r // SUBLANES, SUBLANES, n), axis=0)


def _dot(a, b):
    return jnp.dot(a, b, preferred_element_type=F32)


def _dot_nt(a, b):
    return lax.dot_general(a, b, _NT, preferred_element_type=F32)


def _dot_tn(a, b):
    return lax.dot_general(a, b, _TN, preferred_element_type=F32)


def _rms_bwd(xv, g, dh):
    r = lax.rsqrt(jnp.mean(xv * xv, axis=-1, keepdims=True) + EPS)
    xhat = xv * r
    dxhat = dh * g
    dx = r * (dxhat - xhat * jnp.mean(dxhat * xhat, axis=-1, keepdims=True))
    return dx, dh * xhat


def _row_spec(tm, n, col=0):
    return pl.BlockSpec((tm, n), lambda i: (i, col))


def _full_spec(shape):
    return pl.BlockSpec(shape, lambda *_: (0,) * len(shape))


def _inproj_fwd(x, g, w, b):
    T, D = x.shape
    rest_w = IN_W - QKV_W
    tm = _tile(T, 512)

    def body(x_ref, g_ref, w_ref, b_ref, h_ref, qkv_ref, rest_ref):
        xv = x_ref[...]
        r = lax.rsqrt(jnp.mean(xv * xv, axis=-1, keepdims=True) + EPS)
        h = (xv * r * g_ref[...]).astype(BF16)
        h_ref[...] = h
        qkv_ref[...] = (_dot(h, w_ref[:, :QKV_W]) + b_ref[:, :QKV_W]).astype(BF16)
        rest_ref[...] = _dot(h, w_ref[:, QKV_W:]) + b_ref[:, QKV_W:]

    return pl.pallas_call(
        body, name="inproj_fwd", grid=(T // tm,),
        in_specs=[_row_spec(tm, D), _full_spec((1, D)), _full_spec((D, IN_W)), _full_spec((1, IN_W))],
        out_specs=[_row_spec(tm, D), _row_spec(tm, QKV_W), _row_spec(tm, rest_w)],
        out_shape=[jax.ShapeDtypeStruct((T, D), BF16), jax.ShapeDtypeStruct((T, QKV_W), BF16),
                   jax.ShapeDtypeStruct((T, rest_w), F32)],
        compiler_params=_params("parallel"),
    )(x, g, w, b)


def _attn_masks(first):
    row = lax.broadcasted_iota(jnp.int32, (BLOCK, 2 * BLOCK), 0)
    col = lax.broadcasted_iota(jnp.int32, (BLOCK, 2 * BLOCK), 1)
    dist = row + BLOCK - col
    valid = (dist >= 0) & (dist < BLOCK)
    first_valid = valid & (col >= jnp.where(first, BLOCK, 0))
    return dist.astype(F32), valid, first_valid


def _attn_probs(qh, k, h, sink, distf, valid):
    scale = 1.0 / math.sqrt(HEAD_DIM)
    s = _dot_nt(qh, k) * scale - (2.0 ** (-8.0 * (h + 1) / N_Q)) * distf
    s = jnp.where(valid, s, NEG)
    m = jnp.maximum(jnp.max(s, axis=-1, keepdims=True), sink)
    p = jnp.exp(s - m)
    inv = 1.0 / (jnp.sum(p, axis=-1, keepdims=True) + jnp.exp(sink - m))
    return p * inv, jnp.exp(sink - m) * inv


def _attn_fwd(qkv, sinks):
    T = qkv.shape[0]
    tq = _tile(T, 512)
    nblk = tq // BLOCK

    def body(sink_ref, cur_ref, prev_ref, o_ref, kv_buf):
        i = pl.program_id(0)
        kv_buf[0:BLOCK, :] = prev_ref[:, ATTN_W:]
        kv_buf[BLOCK:, :] = cur_ref[:, ATTN_W:]
        distf, valid, first_valid = _attn_masks(i == 0)
        for j in range(nblk):
            band = kv_buf[j * BLOCK:(j + 2) * BLOCK, :]
            q = cur_ref[j * BLOCK:(j + 1) * BLOCK, :ATTN_W]
            outs = []
            for h in range(N_Q):
                kh = h // GROUP
                k = band[:, kh * HEAD_DIM:(kh + 1) * HEAD_DIM]
                v = band[:, KV_W + kh * HEAD_DIM:KV_W + (kh + 1) * HEAD_DIM]
                p, _ = _attn_probs(q[:, h * HEAD_DIM:(h + 1) * HEAD_DIM], k, h, sink_ref[h], distf,
                                   first_valid if j == 0 else valid)
                outs.append(_dot(p.astype(BF16), v))
            o_ref[j * BLOCK:(j + 1) * BLOCK, :] = jnp.concatenate(outs, axis=1).astype(BF16)

    return pl.pallas_call(
        body, name="attn_fwd", grid=(T // tq,),
        in_specs=[pl.BlockSpec(memory_space=pltpu.SMEM),
                  _row_spec(tq, QKV_W),
                  pl.BlockSpec((BLOCK, QKV_W), lambda i: (jnp.maximum(i * nblk - 1, 0), 0))],
        out_specs=_row_spec(tq, ATTN_W),
        out_shape=jax.ShapeDtypeStruct((T, ATTN_W), BF16),
        scratch_shapes=[pltpu.VMEM((tq + BLOCK, 2 * KV_W), BF16)],
        compiler_params=_params("parallel"),
    )(sinks, qkv, qkv)


def _conv_fwd(rest, cw, cb, lg, lb):
    T = rest.shape[0]
    C = CONV_C
    tm = _tile(T, 256)
    R = _tile(tm, CONV_ROWS)
    per = tm // HALO

    def body(cur_ref, prev_ref, w_ref, cb_ref, g_ref, b_ref, u0_ref, yc_ref, u_ref, ubuf):
        i = pl.program_id(0)
        up = prev_ref[:, :C] * _sigmoid(prev_ref[:, C:])
        ubuf[0:HALO, :] = jnp.where(i > 0, up, 0.0)
        u0 = cur_ref[:, :C] * _sigmoid(cur_ref[:, C:])
        ubuf[HALO:, :] = u0
        u0_ref[...] = u0
        off = HALO - (CONV_K - 1)
        for c in range(tm // R):
            acc = jnp.broadcast_to(cb_ref[...], (R, C))
            for j in range(CONV_K):
                acc = acc + w_ref[j:j + 1, :] * ubuf[c * R + off + j:c * R + off + j + R, :]
            yc_ref[c * R:(c + 1) * R, :] = acc
            xc = acc - jnp.mean(acc, axis=-1, keepdims=True)
            ln = xc * lax.rsqrt(jnp.mean(xc * xc, axis=-1, keepdims=True) + EPS) * g_ref[...] + b_ref[...]
            u_ref[c * R:(c + 1) * R, :] = (ln * _sigmoid(ln)).astype(BF16)

    return pl.pallas_call(
        body, name="conv_fwd", grid=(T // tm,),
        in_specs=[_row_spec(tm, 2 * C),
                  pl.BlockSpec((HALO, 2 * C), lambda i: (jnp.maximum(i * per - 1, 0), 0)),
                  _full_spec((CONV_K, C)), _full_spec((1, C)), _full_spec((1, C)), _full_spec((1, C))],
        out_specs=[_row_spec(tm, C), _row_spec(tm, C), _row_spec(tm, C)],
        out_shape=[jax.ShapeDtypeStruct((T, C), F32), jax.ShapeDtypeStruct((T, C), F32),
                   jax.ShapeDtypeStruct((T, C), BF16)],
        scratch_shapes=[pltpu.VMEM((tm + HALO, C), F32)],
        compiler_params=_params("parallel"),
    )(rest, rest, cw, cb, lg, lb)


def _merge_fwd(attn, u, rest, x, wa, wc, bc, wo):
    T, D = x.shape
    tm = _tile(T, 512)
    gcol = 2 * CONV_C // D

    def body(attn_ref, u_ref, ga_ref, gc_ref, x_ref, wa_ref, wc_ref, bc_ref, wo_ref, m_ref, x1_ref):
        bra = _dot(attn_ref[...], wa_ref[...])
        brc = _dot(u_ref[...], wc_ref[...]) + bc_ref[...]
        mb = (_sigmoid(ga_ref[...]) * bra + _sigmoid(gc_ref[...]) * brc).astype(BF16)
        m_ref[...] = mb
        x1_ref[...] = x_ref[...] + _dot(mb, wo_ref[...])

    return pl.pallas_call(
        body, name="merge_fwd", grid=(T // tm,),
        in_specs=[_row_spec(tm, ATTN_W), _row_spec(tm, CONV_C), _row_spec(tm, D, gcol), _row_spec(tm, D, gcol + 1),
                  _row_spec(tm, D), _full_spec((ATTN_W, D)), _full_spec((CONV_C, D)), _full_spec((1, D)),
                  _full_spec((D, D))],
        out_specs=[_row_spec(tm, D), _row_spec(tm, D)],
        out_shape=[jax.ShapeDtypeStruct((T, D), BF16), jax.ShapeDtypeStruct((T, D), F32)],
        compiler_params=_params("parallel"),
    )(attn, u, rest, rest, x, wa, wc, bc, wo)


def _mlp_fwd(x1, g, w1, w2):
    T, D = x1.shape
    tm = _tile(T, 256)

    def body(x_ref, g_ref, w1_ref, w2_ref, h_ref, z_ref, o_ref):
        xv = x_ref[...]
        r = lax.rsqrt(jnp.mean(xv * xv, axis=-1, keepdims=True) + EPS)
        h = (xv * r * g_ref[...]).astype(BF16)
        h_ref[...] = h
        z = _dot(h, w1_ref[...])
        z_ref[...] = z.astype(BF16)
        a = jnp.square(jnp.maximum(z, 0.0)).astype(BF16)
        o_ref[...] = xv + _dot(a, w2_ref[...])

    return pl.pallas_call(
        body, name="mlp_fwd", grid=(T // tm,),
        in_specs=[_row_spec(tm, D), _full_spec((1, D)), _full_spec((D, D_FF)), _full_spec((D_FF, D))],
        out_specs=[_row_spec(tm, D), _row_spec(tm, D_FF), _row_spec(tm, D)],
        out_shape=[jax.ShapeDtypeStruct((T, D), BF16), jax.ShapeDtypeStruct((T, D_FF), BF16),
                   jax.ShapeDtypeStruct((T, D), F32)],
        compiler_params=_params("parallel"),
    )(x1, g, w1, w2)


def _final_loss(x, g, target):
    T, D = x.shape
    tm = _tile(T, 512)

    def body(x_ref, g_ref, t_ref, l_ref, dx_ref, dg_ref):
        @pl.when(pl.program_id(0) == 0)
        def _():
            l_ref[...] = jnp.zeros_like(l_ref)
            dg_ref[...] = jnp.zeros_like(dg_ref)

        xv = x_ref[...]
        r = lax.rsqrt(jnp.mean(xv * xv, axis=-1, keepdims=True) + EPS)
        e = xv * r * g_ref[...] - t_ref[...]
        l_ref[...] += _rows8(e * e) * (0.5 / D)
        dx, dg = _rms_bwd(xv, g_ref[...], e * (1.0 / D))
        dx_ref[...] = dx
        dg_ref[...] += _rows8(dg)

    return pl.pallas_call(
        body, name="final_loss", grid=(T // tm,),
        in_specs=[_row_spec(tm, D), _full_spec((1, D)), _row_spec(tm, D)],
        out_specs=[_full_spec((SUBLANES, D)), _row_spec(tm, D), _full_spec((SUBLANES, D))],
        out_shape=[jax.ShapeDtypeStruct((SUBLANES, D), F32), jax.ShapeDtypeStruct((T, D), F32),
                   jax.ShapeDtypeStruct((SUBLANES, D), F32)],
        compiler_params=_params("arbitrary"),
    )(x, g, target)


def _mlp_bwd(dx2, x1, z, g, w1, w2):
    T, D = x1.shape
    tm = _tile(T, 256)

    def body(dx2_ref, x_ref, z_ref, g_ref, w1_ref, w2_ref, dx1_ref, dz_ref, dg_ref):
        @pl.when(pl.program_id(0) == 0)
        def _():
            dg_ref[...] = jnp.zeros_like(dg_ref)

        dxo = dx2_ref[...]
        da = _dot_nt(dxo.astype(BF16), w2_ref[...])
        dz = (da * (2.0 * jnp.maximum(z_ref[...].astype(F32), 0.0))).astype(BF16)
        dz_ref[...] = dz
        dh = _dot_nt(dz, w1_ref[...])
        dx, dg = _rms_bwd(x_ref[...], g_ref[...], dh)
        dx1_ref[...] = dxo + dx
        dg_ref[...] += _rows8(dg)

    return pl.pallas_call(
        body, name="mlp_bwd", grid=(T // tm,),
        in_specs=[_row_spec(tm, D), _row_spec(tm, D), _row_spec(tm, D_FF), _full_spec((1, D)),
                  _full_spec((D, D_FF)), _full_spec((D_FF, D))],
        out_specs=[_row_spec(tm, D), _row_spec(tm, D_FF), _full_spec((SUBLANES, D))],
        out_shape=[jax.ShapeDtypeStruct((T, D), F32), jax.ShapeDtypeStruct((T, D_FF), BF16),
                   jax.ShapeDtypeStruct((SUBLANES, D), F32)],
        compiler_params=_params("arbitrary"),
    )(dx2, x1, z, g, w1, w2)


def _tn_matmul(a, b, name, relu_sq=False):
    T, M = a.shape
    N = b.shape[1]
    tk = _tile(T, 1024)
    tm = _tile(M, 1024)
    tn = next((c for c in (1280, 1024, 512, 256) if N % c == 0), N)

    def body(a_ref, b_ref, o_ref):
        @pl.when(pl.program_id(2) == 0)
        def _():
            o_ref[...] = jnp.zeros_like(o_ref)

        av = a_ref[...]
        if relu_sq:
            av = jnp.square(jnp.maximum(av.astype(F32), 0.0))
        o_ref[...] += _dot_tn(av.astype(BF16), b_ref[...].astype(BF16))

    return pl.pallas_call(
        body, name=name, grid=(M // tm, N // tn, T // tk),
        in_specs=[pl.BlockSpec((tk, tm), lambda i, j, k: (k, i)), pl.BlockSpec((tk, tn), lambda i, j, k: (k, j))],
        out_specs=pl.BlockSpec((tm, tn), lambda i, j, k: (i, j)),
        out_shape=jax.ShapeDtypeStruct((M, N), F32),
        compiler_params=_params("parallel", "parallel", "arbitrary"),
    )(a, b)


def _merge_bwd(dx1, attn, u, rest, wa, wc, bc, wo):
    T, D = dx1.shape
    tm = _tile(T, 256)
    gcol = 2 * CONV_C // D

    def body(dx_ref, attn_ref, u_ref, ga_ref, gc_ref, wa_ref, wc_ref, bc_ref, wo_ref,
             dattn_ref, du_ref, dga_ref, dgc_ref, dbra_ref, dbrc_ref, dbc_ref):
        @pl.when(pl.program_id(0) == 0)
        def _():
            dbc_ref[...] = jnp.zeros_like(dbc_ref)

        dm = _dot_nt(dx_ref[...].astype(BF16), wo_ref[...])
        bra = _dot(attn_ref[...], wa_ref[...])
        brc = _dot(u_ref[...], wc_ref[...]) + bc_ref[...]
        sa = _sigmoid(ga_ref[...])
        sc = _sigmoid(gc_ref[...])
        dbra = dm * sa
        dbrc = dm * sc
        dga_ref[...] = (dm * bra * sa * (1.0 - sa)).astype(BF16)
        dgc_ref[...] = (dm * brc * sc * (1.0 - sc)).astype(BF16)
        dbra_b = dbra.astype(BF16)
        dbrc_b = dbrc.astype(BF16)
        dbra_ref[...] = dbra_b
        dbrc_ref[...] = dbrc_b
        dbc_ref[...] += _rows8(dbrc)
        dattn_ref[...] = _dot_nt(dbra_b, wa_ref[...]).astype(BF16)
        du_ref[...] = _dot_nt(dbrc_b, wc_ref[...])

    return pl.pallas_call(
        body, name="merge_bwd", grid=(T // tm,),
        in_specs=[_row_spec(tm, D), _row_spec(tm, ATTN_W), _row_spec(tm, CONV_C), _row_spec(tm, D, gcol),
                  _row_spec(tm, D, gcol + 1), _full_spec((ATTN_W, D)), _full_spec((CONV_C, D)), _full_spec((1, D)),
                  _full_spec((D, D))],
        out_specs=[_row_spec(tm, ATTN_W), _row_spec(tm, CONV_C), _row_spec(tm, D), _row_spec(tm, D),
                   _row_spec(tm, D), _row_spec(tm, D), _full_spec((SUBLANES, D))],
        out_shape=[jax.ShapeDtypeStruct((T, ATTN_W), BF16), jax.ShapeDtypeStruct((T, CONV_C), F32),
                   jax.ShapeDtypeStruct((T, D), BF16), jax.ShapeDtypeStruct((T, D), BF16),
                   jax.ShapeDtypeStruct((T, D), BF16), jax.ShapeDtypeStruct((T, D), BF16),
                   jax.ShapeDtypeStruct((SUBLANES, D), F32)],
        compiler_params=_params("arbitrary"),
    )(dx1, attn, u, rest, rest, wa, wc, bc, wo)


def _conv_bwd_norm(du, yc, lg, lb):
    T, C = yc.shape
    tm = _tile(T, 512)

    def body(du_ref, yc_ref, g_ref, b_ref, dyc_ref, dg_ref, db_ref):
        @pl.when(pl.program_id(0) == 0)
        def _():
            dg_ref[...] = jnp.zeros_like(dg_ref)
            db_ref[...] = jnp.zeros_like(db_ref)

        yv = yc_ref[...]
        xc = yv - jnp.mean(yv, axis=-1, keepdims=True)
        rstd = lax.rsqrt(jnp.mean(xc * xc, axis=-1, keepdims=True) + EPS)
        xn = xc * rstd
        ln = xn * g_ref[...] + b_ref[...]
        sg = _sigmoid(ln)
        dln = du_ref[...] * sg * (1.0 + ln * (1.0 - sg))
        dg_ref[...] += _rows8(dln * xn)
        db_ref[...] += _rows8(dln)
        dxn = dln * g_ref[...]
        dyc_ref[...] = rstd * (dxn - jnp.mean(dxn, axis=-1, keepdims=True)
                               - xn * jnp.mean(dxn * xn, axis=-1, keepdims=True))

    return pl.pallas_call(
        body, name="conv_bwd_norm", grid=(T // tm,),
        in_specs=[_row_spec(tm, C), _row_spec(tm, C), _full_spec((1, C)), _full_spec((1, C))],
        out_specs=[_row_spec(tm, C), _full_spec((SUBLANES, C)), _full_spec((SUBLANES, C))],
        out_shape=[jax.ShapeDtypeStruct((T, C), F32), jax.ShapeDtypeStruct((SUBLANES, C), F32),
                   jax.ShapeDtypeStruct((SUBLANES, C), F32)],
        compiler_params=_params("arbitrary"),
    )(du, yc, lg, lb)


def _conv_bwd_taps(dyc, u0, rest, cw):
    T, C = dyc.shape
    tm = _tile(T, 256)
    R = _tile(tm, CONV_ROWS)
    per = tm // HALO
    nt = T // tm

    def body(dy_ref, dyn_ref, u0_ref, u0p_ref, glu_ref, w_ref, dga_ref, dgb_ref, dw_ref, db_ref, dbuf, ubuf):
        i = pl.program_id(0)

        @pl.when(i == 0)
        def _():
            dw_ref[...] = jnp.zeros_like(dw_ref)
            db_ref[...] = jnp.zeros_like(db_ref)

        dbuf[0:tm, :] = dy_ref[...]
        dbuf[tm:, :] = jnp.where(i < nt - 1, dyn_ref[...], 0.0)
        ubuf[0:HALO, :] = jnp.where(i > 0, u0p_ref[...], 0.0)
        ubuf[HALO:, :] = u0_ref[...]
        off = HALO - (CONV_K - 1)
        for c in range(tm // R):
            dy = dbuf[c * R:(c + 1) * R, :]
            acc = jnp.zeros((R, C), F32)
            for j in range(CONV_K):
                acc = acc + w_ref[j:j + 1, :] * dbuf[c * R + CONV_K - 1 - j:c * R + CONV_K - 1 - j + R, :]
                dw_ref[j * SUBLANES:(j + 1) * SUBLANES, :] += _rows8(
                    dy * ubuf[c * R + off + j:c * R + off + j + R, :])
            db_ref[...] += _rows8(dy)
            a = glu_ref[c * R:(c + 1) * R, :C]
            sb = _sigmoid(glu_ref[c * R:(c + 1) * R, C:])
            dga_ref[c * R:(c + 1) * R, :] = (acc * sb).astype(BF16)
            dgb_ref[c * R:(c + 1) * R, :] = (acc * a * sb * (1.0 - sb)).astype(BF16)

    return pl.pallas_call(
        body, name="conv_bwd_taps", grid=(nt,),
        in_specs=[_row_spec(tm, C),
                  pl.BlockSpec((HALO, C), lambda i: (jnp.minimum((i + 1) * per, T // HALO - 1), 0)),
                  _row_spec(tm, C),
                  pl.BlockSpec((HALO, C), lambda i: (jnp.maximum(i * per - 1, 0), 0)),
                  _row_spec(tm, 2 * C), _full_spec((CONV_K, C))],
        out_specs=[_row_spec(tm, C), _row_spec(tm, C), _full_spec((CONV_K * SUBLANES, C)),
                   _full_spec((SUBLANES, C))],
        out_shape=[jax.ShapeDtypeStruct((T, C), BF16), jax.ShapeDtypeStruct((T, C), BF16),
                   jax.ShapeDtypeStruct((CONV_K * SUBLANES, C), F32), jax.ShapeDtypeStruct((SUBLANES, C), F32)],
        scratch_shapes=[pltpu.VMEM((tm + HALO, C), F32), pltpu.VMEM((tm + HALO, C), F32)],
        compiler_params=_params("arbitrary"),
    )(dyc, dyc, u0, u0, rest, cw)


def _attn_bwd(qkv, dattn, sinks):
    T = qkv.shape[0]
    tq = _tile(T, 512)
    nblk = tq // BLOCK
    scale = 1.0 / math.sqrt(HEAD_DIM)

    def body(sink_ref, cur_ref, prev_ref, do_ref, dq_ref, hi_ref, lo_ref, ds_ref, kv_buf):
        i = pl.program_id(0)

        @pl.when(i == 0)
        def _():
            ds_ref[...] = jnp.zeros_like(ds_ref)

        kv_buf[0:BLOCK, :] = prev_ref[:, ATTN_W:]
        kv_buf[BLOCK:, :] = cur_ref[:, ATTN_W:]
        distf, valid, first_valid = _attn_masks(i == 0)
        for j in range(nblk):
            rows = slice(j * BLOCK, (j + 1) * BLOCK)
            band = kv_buf[j * BLOCK:(j + 2) * BLOCK, :]
            q = cur_ref[rows, :ATTN_W]
            do = do_ref[rows, :]
            dqs, dks, dvs = [], [], []
            for kh in range(N_KV):
                k = band[:, kh * HEAD_DIM:(kh + 1) * HEAD_DIM]
                v = band[:, KV_W + kh * HEAD_DIM:KV_W + (kh + 1) * HEAD_DIM]
                ps, dss, qs, dos = [], [], [], []
                for g in range(GROUP):
                    h = kh * GROUP + g
                    qh = q[:, h * HEAD_DIM:(h + 1) * HEAD_DIM]
                    doh = do[:, h * HEAD_DIM:(h + 1) * HEAD_DIM]
                    p, psink = _attn_probs(qh, k, h, sink_ref[h], distf, first_valid if j == 0 else valid)
                    pdp = p * _dot_nt(doh, v)
                    delta = jnp.sum(pdp, axis=-1, keepdims=True)
                    dsb = ((pdp - p * delta) * scale).astype(BF16)
                    sink_terms = _rows8(psink * pdp)
                    ds_ref[h * SUBLANES:(h + 1) * SUBLANES, :] += sink_terms[:, :BLOCK] + sink_terms[:, BLOCK:]
                    dqs.append(_dot(dsb, k))
                    ps.append(p.astype(BF16))
                    dss.append(dsb)
                    qs.append(qh)
                    dos.append(doh)
                dks.append(_dot_tn(jnp.concatenate(dss, axis=0), jnp.concatenate(qs, axis=0)))
                dvs.append(_dot_tn(jnp.concatenate(ps, axis=0), jnp.concatenate(dos, axis=0)))
            dq_ref[rows, :] = jnp.concatenate(dqs, axis=1).astype(BF16)
            dkv = jnp.concatenate(dks + dvs, axis=1)
            lo_ref[rows, :] = dkv[:BLOCK, :]
            hi_ref[rows, :] = dkv[BLOCK:, :]

    return pl.pallas_call(
        body, name="attn_bwd", grid=(T // tq,),
        in_specs=[pl.BlockSpec(memory_space=pltpu.SMEM),
                  _row_spec(tq, QKV_W),
                  pl.BlockSpec((BLOCK, QKV_W), lambda i: (jnp.maximum(i * nblk - 1, 0), 0)),
                  _row_spec(tq, ATTN_W)],
        out_specs=[_row_spec(tq, ATTN_W), _row_spec(tq, 2 * KV_W), _row_spec(tq, 2 * KV_W),
                   _full_spec((N_Q * SUBLANES, BLOCK))],
        out_shape=[jax.ShapeDtypeStruct((T, ATTN_W), BF16), jax.ShapeDtypeStruct((T, 2 * KV_W), F32),
                   jax.ShapeDtypeStruct((T, 2 * KV_W), F32), jax.ShapeDtypeStruct((N_Q * SUBLANES, BLOCK), F32)],
        scratch_shapes=[pltpu.VMEM((tq + BLOCK, 2 * KV_W), BF16)],
        compiler_params=_params("arbitrary"),
    )(sinks, qkv, qkv, dattn)


def _inproj_bwd(dq, hi, lo, dglu_a, dglu_b, dga, dgc, x, g, w, dx1):
    T, D = x.shape
    C = CONV_C
    tm = _tile(T, 256)
    per = tm // BLOCK
    nt = T // tm
    kv2 = 2 * KV_W

    def body(dq_ref, hi_ref, lo_ref, lon_ref, da_ref, db_ref, dga_ref, dgc_ref, x_ref, g_ref, w_ref, dx1_ref,
             dp_ref, dx_ref, dg_ref, dbias_ref):
        i = pl.program_id(0)

        @pl.when(i == 0)
        def _():
            dg_ref[...] = jnp.zeros_like(dg_ref)
            dbias_ref[...] = jnp.zeros_like(dbias_ref)

        dp_ref[:, :ATTN_W] = dq_ref[...]
        lo_next = jnp.where(i < nt - 1, lon_ref[...], 0.0)
        if tm > BLOCK:
            lo_shift = jnp.concatenate([lo_ref[BLOCK:, :], lo_next], axis=0)
        else:
            lo_shift = lo_next
        dp_ref[:, ATTN_W:QKV_W] = (hi_ref[...] + lo_shift).astype(BF16)
        dp_ref[:, QKV_W:QKV_W + C] = da_ref[...]
        dp_ref[:, QKV_W + C:QKV_W + 2 * C] = db_ref[...]
        dp_ref[:, QKV_W + 2 * C:QKV_W + 2 * C + D] = dga_ref[...]
        dp_ref[:, QKV_W + 2 * C + D:] = dgc_ref[...]
        dp = dp_ref[...]
        dbias_ref[...] += _rows8(dp.astype(F32))
        dh = _dot_nt(dp, w_ref[...])
        dx, dg = _rms_bwd(x_ref[...], g_ref[...], dh)
        dx_ref[...] = dx1_ref[...] + dx
        dg_ref[...] += _rows8(dg)

    return pl.pallas_call(
        body, name="inproj_bwd", grid=(nt,),
        in_specs=[_row_spec(tm, ATTN_W), _row_spec(tm, kv2), _row_spec(tm, kv2),
                  pl.BlockSpec((BLOCK, kv2), lambda i: (jnp.minimum((i + 1) * per, T // BLOCK - 1), 0)),
                  _row_spec(tm, C), _row_spec(tm, C), _row_spec(tm, D), _row_spec(tm, D),
                  _row_spec(tm, D), _full_spec((1, D)), _full_spec((D, IN_W)), _row_spec(tm, D)],
        out_specs=[_row_spec(tm, IN_W), _row_spec(tm, D), _full_spec((SUBLANES, D)), _full_spec((SUBLANES, IN_W))],
        out_shape=[jax.ShapeDtypeStruct((T, IN_W), BF16), jax.ShapeDtypeStruct((T, D), F32),
                   jax.ShapeDtypeStruct((SUBLANES, D), F32), jax.ShapeDtypeStruct((SUBLANES, IN_W), F32)],
        compiler_params=_params("arbitrary"),
    )(dq, hi, lo, lo, dglu_a, dglu_b, dga, dgc, x, g, w, dx1)


def _mesh_pos():
    return lax.axis_index("x"), lax.axis_index("y"), lax.axis_index("c")


def _all_gather(arrs, name):
    n = len(arrs)

    def body(*refs):
        ins, outs = refs[:n], refs[n:2 * n]
        send_sems, recv_sems, local_sems = refs[2 * n:]
        x, y, c = _mesh_pos()
        me, sibling = (x, y, c), (x, y, 1 - c)
        chips = [(1 - x, y), (x, 1 - y), (1 - x, 1 - y)]

        def block(p, dev):
            return outs[p].at[4 * dev[0] + 2 * dev[1] + dev[2]]

        def copy(p, k, dev, to, src=None):
            return pltpu.make_async_remote_copy(
                src_ref=block(p, dev) if src is None else src, dst_ref=block(p, dev),
                send_sem=send_sems.at[7 * p + k], recv_sem=recv_sems.at[7 * p + k],
                device_id=to, device_id_type=MESH)

        mine = [pltpu.make_async_copy(ins[p], block(p, me), local_sems.at[p]) for p in range(n)]
        for cp in mine:
            cp.start()
        first = []
        for p in range(n):
            first.append(copy(p, 0, me, sibling, src=ins[p]))
            first += [copy(p, 1 + j, me, (*chip, c), src=ins[p]) for j, chip in enumerate(chips)]
        for cp in first:
            cp.start()
        passed = []
        for j, chip in enumerate(chips):
            for p in range(n):
                copy(p, 1 + j, (*chip, c), me).wait_recv()
                fwd = copy(p, 4 + j, (*chip, c), sibling)
                fwd.start()
                passed.append(fwd)
        for p in range(n):
            copy(p, 0, sibling, me).wait_recv()
        for j, chip in enumerate(chips):
            for p in range(n):
                copy(p, 4 + j, (*chip, 1 - c), me).wait_recv()
        for cp in first + passed:
            cp.wait_send()
        for cp in mine:
            cp.wait()

    any_spec = pl.BlockSpec(memory_space=pl.ANY)
    return pl.pallas_call(
        body, name=name,
        in_specs=[any_spec] * n, out_specs=[any_spec] * n,
        out_shape=[jax.ShapeDtypeStruct((N_DEV,) + a.shape, a.dtype) for a in arrs],
        scratch_shapes=[pltpu.SemaphoreType.DMA((7 * n,)), pltpu.SemaphoreType.DMA((7 * n,)),
                        pltpu.SemaphoreType.DMA((n,))],
    )(*arrs)


def _all_to_all(arrs, name):
    n = len(arrs)

    def body(*refs):
        ins, outs = refs[:n], refs[n:2 * n]
        send_sems, recv_sems, local_sems = refs[2 * n:]
        x, y, c = _mesh_pos()
        me = 4 * x + 2 * y + c
        peers = []
        for k in range(1, N_DEV):
            px = x if not k & 4 else 1 - x
            py = y if not k & 2 else 1 - y
            pc = c if not k & 1 else 1 - c
            peers.append((px, py, pc))

        def copy(p, k, peer):
            pid = 4 * peer[0] + 2 * peer[1] + peer[2]
            return pltpu.make_async_remote_copy(
                src_ref=ins[p].at[pid], dst_ref=outs[p].at[me],
                send_sem=send_sems.at[7 * p + k], recv_sem=recv_sems.at[7 * p + k],
                device_id=peer, device_id_type=MESH)

        def landed(p, k, peer):
            pid = 4 * peer[0] + 2 * peer[1] + peer[2]
            return pltpu.make_async_remote_copy(
                src_ref=ins[p].at[pid], dst_ref=outs[p].at[pid],
                send_sem=send_sems.at[7 * p + k], recv_sem=recv_sems.at[7 * p + k],
                device_id=peer, device_id_type=MESH)

        mine = [pltpu.make_async_copy(ins[p].at[me], outs[p].at[me], local_sems.at[p]) for p in range(n)]
        for cp in mine:
            cp.start()
        sends = [copy(p, k, peer) for p in range(n) for k, peer in enumerate(peers)]
        for cp in sends:
            cp.start()
        for p in range(n):
            for k, peer in enumerate(peers):
                landed(p, k, peer).wait_recv()
        for cp in sends:
            cp.wait_send()
        for cp in mine:
            cp.wait()

    any_spec = pl.BlockSpec(memory_space=pl.ANY)
    return pl.pallas_call(
        body, name=name,
        in_specs=[any_spec] * n, out_specs=[any_spec] * n,
        out_shape=[jax.ShapeDtypeStruct(a.shape, a.dtype) for a in arrs],
        scratch_shapes=[pltpu.SemaphoreType.DMA((7 * n,)), pltpu.SemaphoreType.DMA((7 * n,)),
                        pltpu.SemaphoreType.DMA((n,))],
    )(*arrs)


def _adamw(parts, w, m, v, name):
    R, N = w.shape
    tr = _tile(R, 256) if R % SUBLANES == 0 else R
    c1 = 1.0 / (1.0 - ADAM_B1 ** ADAM_STEP)
    c2 = 1.0 / (1.0 - ADAM_B2 ** ADAM_STEP)

    def body(p_ref, w_ref, m_ref, v_ref, g_ref, d_ref, mo_ref, vo_ref):
        g = p_ref[0]
        for s in range(1, N_DEV):
            g = g + p_ref[s]
        g_ref[...] = g
        mn = ADAM_B1 * m_ref[...] + (1.0 - ADAM_B1) * g
        vn = ADAM_B2 * v_ref[...] + (1.0 - ADAM_B2) * (g * g)
        mo_ref[...] = mn
        vo_ref[...] = vn
        d_ref[...] = -ADAM_LR * ((mn * c1) / (jnp.sqrt(vn * c2) + ADAM_EPS) + ADAM_WD * w_ref[...])

    spec = pl.BlockSpec((tr, N), lambda i: (i, 0))
    out = jax.ShapeDtypeStruct((R, N), F32)
    return pl.pallas_call(
        body, name=name, grid=(R // tr,),
        in_specs=[pl.BlockSpec((N_DEV, tr, N), lambda i: (0, i, 0)), spec, spec, spec],
        out_specs=[spec] * 4, out_shape=[out] * 4,
        compiler_params=_params("parallel"),
    )(parts, w, m, v)


_SHARDED = ("w_in", "conv_w", "w_attn_proj", "w_conv_proj", "w_out", "w_mlp1", "w_mlp2")
_ROW_SHARDED = ("w_out", "w_mlp2")
_SMALL = ("mix_norm_g", "b_in", "sinks", "conv_b", "conv_ln_g", "conv_ln_b", "b_conv_proj", "mlp_norm_g",
          "final_norm_g")
_ORDER = ("mix_norm_g", "w_in", "b_in", "sinks", "conv_w", "conv_b", "conv_ln_g", "conv_ln_b", "w_attn_proj",
          "w_conv_proj", "b_conv_proj", "w_out", "mlp_norm_g", "w_mlp1", "w_mlp2", "final_norm_g")
_PACK = 1024


def _full_weight(name, gathered):
    _, depth, a, b = gathered.shape
    if name in _ROW_SHARDED:
        return gathered.transpose(1, 0, 2, 3).reshape(depth, N_DEV * a, b)
    return gathered.transpose(1, 2, 0, 3).reshape(depth, a, N_DEV * b)


def _grad_blocks(name, grad):
    depth, a, b = grad.shape
    if name in _ROW_SHARDED:
        return grad.reshape(depth, N_DEV, a // N_DEV, b).transpose(1, 0, 2, 3)
    return grad.reshape(depth, a, N_DEV, b // N_DEV).transpose(2, 0, 1, 3)


def _pack(arrs):
    flat = []
    for a in arrs:
        a = a.reshape(-1)
        flat.append(jnp.pad(a, (0, -a.size % _PACK)))
    return jnp.concatenate(flat).reshape(-1, BLOCK)


def _unpack(packed, shapes):
    flat = packed.reshape(-1)
    out, off = [], 0
    for s in shapes:
        n = math.prod(s)
        out.append(flat[off:off + n].reshape(s))
        off += n + (-n % _PACK)
    return out


def _layer_fwd(x, lw):
    h, qkv, rest = _inproj_fwd(x, lw["mix_norm_g"], lw["w_in"], lw["b_in"])
    attn = _attn_fwd(qkv, lw["sinks"])
    u0, yc, u = _conv_fwd(rest, lw["conv_w"], lw["conv_b"], lw["conv_ln_g"], lw["conv_ln_b"])
    merged, x1 = _merge_fwd(attn, u, rest, x, lw["w_attn_proj"], lw["w_conv_proj"], lw["b_conv_proj"], lw["w_out"])
    h2, z, x2 = _mlp_fwd(x1, lw["mlp_norm_g"], lw["w_mlp1"], lw["w_mlp2"])
    saved = dict(x=x, h=h, qkv=qkv, rest=rest, attn=attn, u0=u0, yc=yc, u=u, merged=merged, x1=x1, h2=h2, z=z)
    return x2, saved


def _layer_bwd(dx2, lw, s):
    g = {}
    dx1, dz, dg2 = _mlp_bwd(dx2, s["x1"], s["z"], lw["mlp_norm_g"], lw["w_mlp1"], lw["w_mlp2"])
    g["mlp_norm_g"] = jnp.sum(dg2, axis=0)
    g["w_mlp1"] = _tn_matmul(s["h2"], dz, "dw_mlp1")
    g["w_mlp2"] = _tn_matmul(s["z"], dx2, "dw_mlp2", relu_sq=True)
    dattn, du, dga, dgc, dbra, dbrc, dbc = _merge_bwd(
        dx1, s["attn"], s["u"], s["rest"], lw["w_attn_proj"], lw["w_conv_proj"], lw["b_conv_proj"], lw["w_out"])
    g["b_conv_proj"] = jnp.sum(dbc, axis=0)
    g["w_out"] = _tn_matmul(s["merged"], dx1, "dw_out")
    g["w_attn_proj"] = _tn_matmul(s["attn"], dbra, "dw_attn_proj")
    g["w_conv_proj"] = _tn_matmul(s["u"], dbrc, "dw_conv_proj")
    dyc, dlg, dlb = _conv_bwd_norm(du, s["yc"], lw["conv_ln_g"], lw["conv_ln_b"])
    g["conv_ln_g"] = jnp.sum(dlg, axis=0)
    g["conv_ln_b"] = jnp.sum(dlb, axis=0)
    dglu_a, dglu_b, dcw, dcb = _conv_bwd_taps(dyc, s["u0"], s["rest"], lw["conv_w"])
    g["conv_w"] = jnp.sum(dcw.reshape(CONV_K, SUBLANES, CONV_C), axis=1)
    g["conv_b"] = jnp.sum(dcb, axis=0)
    dq, hi, lo, dsk = _attn_bwd(s["qkv"], dattn, lw["sinks"])
    g["sinks"] = -jnp.sum(dsk.reshape(N_Q, SUBLANES * BLOCK), axis=1)
    dproj, dx, dg1, dbin = _inproj_bwd(dq, hi, lo, dglu_a, dglu_b, dga, dgc, s["x"], lw["mix_norm_g"], lw["w_in"],
                                       dx1)
    g["mix_norm_g"] = jnp.sum(dg1, axis=0)
    g["b_in"] = jnp.sum(dbin, axis=0)
    g["w_in"] = _tn_matmul(s["h"], dproj, "dw_in")
    return dx, g


def kernel(x, mix_norm_g, w_in, b_in, sinks, conv_w, conv_b, conv_ln_g, conv_ln_b, w_attn_proj, w_conv_proj, b_conv_proj, w_out, mlp_norm_g, w_mlp1, w_mlp2, final_norm_g, loss_target, m_mix_norm_g, m_w_in, m_b_in, m_sinks, m_conv_w, m_conv_b, m_conv_ln_g, m_conv_ln_b, m_w_attn_proj, m_w_conv_proj, m_b_conv_proj, m_w_out, m_mlp_norm_g, m_w_mlp1, m_w_mlp2, m_final_norm_g, v_mix_norm_g, v_w_in, v_b_in, v_sinks, v_conv_w, v_conv_b, v_conv_ln_g, v_conv_ln_b, v_w_attn_proj, v_w_conv_proj, v_b_conv_proj, v_w_out, v_mlp_norm_g, v_w_mlp1, v_w_mlp2, v_final_norm_g):
    w = dict(mix_norm_g=mix_norm_g, w_in=w_in, b_in=b_in, sinks=sinks, conv_w=conv_w, conv_b=conv_b,
             conv_ln_g=conv_ln_g, conv_ln_b=conv_ln_b, w_attn_proj=w_attn_proj, w_conv_proj=w_conv_proj,
             b_conv_proj=b_conv_proj, w_out=w_out, mlp_norm_g=mlp_norm_g, w_mlp1=w_mlp1, w_mlp2=w_mlp2,
             final_norm_g=final_norm_g)
    m = dict(mix_norm_g=m_mix_norm_g, w_in=m_w_in, b_in=m_b_in, sinks=m_sinks, conv_w=m_conv_w, conv_b=m_conv_b,
             conv_ln_g=m_conv_ln_g, conv_ln_b=m_conv_ln_b, w_attn_proj=m_w_attn_proj, w_conv_proj=m_w_conv_proj,
             b_conv_proj=m_b_conv_proj, w_out=m_w_out, mlp_norm_g=m_mlp_norm_g, w_mlp1=m_w_mlp1, w_mlp2=m_w_mlp2,
             final_norm_g=m_final_norm_g)
    v = dict(mix_norm_g=v_mix_norm_g, w_in=v_w_in, b_in=v_b_in, sinks=v_sinks, conv_w=v_conv_w, conv_b=v_conv_b,
             conv_ln_g=v_conv_ln_g, conv_ln_b=v_conv_ln_b, w_attn_proj=v_w_attn_proj, w_conv_proj=v_w_conv_proj,
             b_conv_proj=v_b_conv_proj, w_out=v_w_out, mlp_norm_g=v_mlp_norm_g, w_mlp1=v_w_mlp1, w_mlp2=v_w_mlp2,
             final_norm_g=v_final_norm_g)
    T = x.shape[1]
    xs = x.reshape(T, D_MODEL)
    target = loss_target.reshape(T, D_MODEL)

    shards = [w[n] if n == "conv_w" else w[n].astype(BF16) for n in _SHARDED]
    full = {n: _full_weight(n, a) for n, a in zip(_SHARDED, _all_gather(shards, "gather_weights"))}

    def layer_weights(l):
        lw = {n: full[n][l] for n in _SHARDED}
        for n in _SMALL:
            if n != "final_norm_g":
                lw[n] = w[n][l] if n == "sinks" else w[n][l].reshape(1, -1)
        return lw

    acts = xs
    saved = []
    for l in range(DEPTH):
        acts, s = _layer_fwd(acts, layer_weights(l))
        saved.append(s)
    lterms, dx, dgf = _final_loss(acts, final_norm_g.reshape(1, -1), target)
    grads = [None] * DEPTH
    for l in reversed(range(DEPTH)):
        dx, grads[l] = _layer_bwd(dx, layer_weights(l), saved[l])
    grad = {n: jnp.stack([grads[l][n] for l in range(DEPTH)]) for n in _ORDER if n != "final_norm_g"}
    grad["final_norm_g"] = jnp.sum(dgf, axis=0)

    blocks = [_grad_blocks(n, grad[n]) for n in _SHARDED]
    received = dict(zip(_SHARDED, _all_to_all(blocks, "scatter_grads")))
    small_shapes = [w[n].shape for n in _SMALL] + [(1,)]
    small = _pack([grad[n] for n in _SMALL] + [jnp.sum(lterms).reshape(1)])
    small_parts = _all_gather([small], "gather_small")[0]

    out_g, out_d, out_m, out_v = {}, {}, {}, {}
    for n in _SHARDED:
        shape = w[n].shape
        two_d = (shape[0] * shape[1], shape[2])
        res = _adamw(received[n].reshape((N_DEV,) + two_d), w[n].reshape(two_d), m[n].reshape(two_d),
                     v[n].reshape(two_d), "adamw_" + n)
        out_g[n], out_d[n], out_m[n], out_v[n] = (r.reshape(shape) for r in res)
    zero = jnp.zeros((1,), F32)
    res = _adamw(small_parts, _pack([w[n] for n in _SMALL] + [zero]), _pack([m[n] for n in _SMALL] + [zero]),
                 _pack([v[n] for n in _SMALL] + [zero]), "adamw_small")
    unpacked = [_unpack(r, small_shapes) for r in res]
    for i, n in enumerate(_SMALL):
        out_g[n], out_d[n], out_m[n], out_v[n] = (u[i] for u in unpacked)
    loss = unpacked[0][-1].reshape(())
    return (loss, dx.reshape(x.shape), *[out_g[n] for n in _ORDER], *[out_d[n] for n in _ORDER],
            *[out_m[n] for n in _ORDER], *[out_v[n] for n in _ORDER])
```

```python
import functools
import math

import jax
import jax.numpy as jnp
from jax import lax
from jax.experimental import pallas as pl
from jax.experimental.pallas import tpu as pltpu

D_MODEL = 1024
SEQ = 16384
DEPTH = 2
N_Q = 8
N_KV = 2
GROUP = N_Q // N_KV
HEAD_DIM = 64
ATTN_W = N_Q * HEAD_DIM
KV_W = N_KV * HEAD_DIM
BLOCK = 128
CONV_C = D_MODEL // 2
CONV_K = 31
D_FF = 4 * D_MODEL
QKV_W = ATTN_W + 2 * KV_W
IN_W = QKV_W + 2 * CONV_C + 2 * D_MODEL
EPS = 1e-6
NEG = -1e30
N_DEV = 8

ADAM_LR = 0.001
ADAM_B1 = 0.9
ADAM_B2 = 0.999
ADAM_EPS = 1e-08
ADAM_WD = 0.01
ADAM_STEP = 10

F32 = jnp.float32
BF16 = jnp.bfloat16
MESH = pl.DeviceIdType.MESH

SUBLANES = 8
HALO = 32
CONV_ROWS = 64
VMEM_LIMIT = 52 * 1024 * 1024

_NT = (((1,), (1,)), ((), ()))
_TN = (((0,), (0,)), ((), ()))


def _params(*sem):
    return pltpu.CompilerParams(dimension_semantics=sem, vmem_limit_bytes=VMEM_LIMIT)


def _tile(n, pref):
    t = min(n, pref)
    assert n % t == 0, (n, t)
    return t


def _sigmoid(v):
    return 1.0 / (1.0 + jnp.exp(-v))


def _rows8(v):
    r, n = v.shape
    return jnp.sum(v.reshape(r // SUBLANES, SUBLANES, n), axis=0)


def _dot(a, b):
    return jnp.dot(a, b, preferred_element_type=F32)


def _dot_nt(a, b):
    return lax.dot_general(a, b, _NT, preferred_element_type=F32)


def _dot_tn(a, b):
    return lax.dot_general(a, b, _TN, preferred_element_type=F32)


def _rms_bwd(xv, g, dh):
    r = lax.rsqrt(jnp.mean(xv * xv, axis=-1, keepdims=True) + EPS)
    xhat = xv * r
    dxhat = dh * g
    dx = r * (dxhat - xhat * jnp.mean(dxhat * xhat, axis=-1, keepdims=True))
    return dx, dh * xhat


def _row_spec(tm, n, col=0):
    return pl.BlockSpec((tm, n), lambda i: (i, col))


def _full_spec(shape):
    return pl.BlockSpec(shape, lambda *_: (0,) * len(shape))


def _mesh_pos():
    return lax.axis_index("x"), lax.axis_index("y"), lax.axis_index("c")


def _dev_index(dev):
    return 4 * dev[0] + 2 * dev[1] + dev[2]


class _Exchange:
    middle_at = None

    def __init__(self, arrs):
        self.arrays = list(arrs)

    def out_shape(self):
        return [jax.ShapeDtypeStruct(a.shape, a.dtype) for a in self.arrays]

    def scratch(self):
        n = len(self.arrays)
        return [pltpu.SemaphoreType.DMA((7 * n,)), pltpu.SemaphoreType.DMA((7 * n,)), pltpu.SemaphoreType.DMA((n,))]

    def _copies(self, ins, outs, sems):
        send_sems, recv_sems, local_sems = sems
        x, y, c = _mesh_pos()
        me = _dev_index((x, y, c))
        mine, sends, arrivals = [], [], []
        for p in range(len(self.arrays)):
            mine.append(pltpu.make_async_copy(ins[p].at[me], outs[p].at[me], local_sems.at[p]))
            for k in range(1, N_DEV):
                peer = (1 - x if k & 4 else x, 1 - y if k & 2 else y, 1 - c if k & 1 else c)
                pid = _dev_index(peer)
                pair = dict(send_sem=send_sems.at[7 * p + k - 1], recv_sem=recv_sems.at[7 * p + k - 1],
                            device_id=peer, device_id_type=MESH)
                sends.append(pltpu.make_async_remote_copy(src_ref=ins[p].at[pid], dst_ref=outs[p].at[me], **pair))
                arrivals.append(pltpu.make_async_remote_copy(src_ref=ins[p].at[pid], dst_ref=outs[p].at[pid], **pair))
        return mine, sends, arrivals

    def start(self, ins, outs, sems):
        mine, sends, _ = self._copies(ins, outs, sems)
        for cp in mine + sends:
            cp.start()

    def finish(self, ins, outs, sems):
        mine, sends, arrivals = self._copies(ins, outs, sems)
        for cp in arrivals:
            cp.wait_recv()
        for cp in sends:
            cp.wait_send()
        for cp in mine:
            cp.wait()


class _Gather:
    middle_at = 0.75

    def __init__(self, arrs):
        self.arrays = list(arrs)

    def out_shape(self):
        return [jax.ShapeDtypeStruct((N_DEV,) + a.shape, a.dtype) for a in self.arrays]

    def scratch(self):
        n = len(self.arrays)
        return [pltpu.SemaphoreType.DMA((7 * n,)), pltpu.SemaphoreType.DMA((7 * n,)), pltpu.SemaphoreType.DMA((n,))]

    def _copies(self, ins, outs, sems):
        send_sems, recv_sems, local_sems = sems
        x, y, c = _mesh_pos()
        me, sibling = (x, y, c), (x, y, 1 - c)
        chips = [(1 - x, y), (x, 1 - y), (1 - x, 1 - y)]
        n = len(self.arrays)

        def copy(p, k, dev, to, src=None):
            block = outs[p].at[_dev_index(dev)]
            return pltpu.make_async_remote_copy(
                src_ref=block if src is None else src, dst_ref=block,
                send_sem=send_sems.at[7 * p + k], recv_sem=recv_sems.at[7 * p + k],
                device_id=to, device_id_type=MESH)

        cp = dict(mine=[pltpu.make_async_copy(ins[p], outs[p].at[_dev_index(me)], local_sems.at[p])
                        for p in range(n)])
        cp["first"] = [copy(p, 0, me, sibling, src=ins[p]) for p in range(n)]
        cp["first"] += [copy(p, 1 + j, me, (*chip, c), src=ins[p]) for p in range(n) for j, chip in enumerate(chips)]
        cp["over_ici"] = [copy(p, 1 + j, (*chip, c), me) for j, chip in enumerate(chips) for p in range(n)]
        cp["passed"] = [copy(p, 4 + j, (*chip, c), sibling) for j, chip in enumerate(chips) for p in range(n)]
        cp["from_sibling"] = [copy(p, 0, sibling, me) for p in range(n)]
        cp["from_sibling"] += [copy(p, 4 + j, (*chip, 1 - c), me) for j, chip in enumerate(chips) for p in range(n)]
        return cp

    def start(self, ins, outs, sems):
        cp = self._copies(ins, outs, sems)
        for d in cp["mine"] + cp["first"]:
            d.start()

    def middle(self, ins, outs, sems):
        cp = self._copies(ins, outs, sems)
        for arrived, onward in zip(cp["over_ici"], cp["passed"]):
            arrived.wait_recv()
            onward.start()

    def finish(self, ins, outs, sems):
        cp = self._copies(ins, outs, sems)
        for d in cp["from_sibling"]:
            d.wait_recv()
        for d in cp["first"] + cp["passed"]:
            d.wait_send()
        for d in cp["mine"]:
            d.wait()


def _run_comm(comm, name):
    n = len(comm.arrays)

    def body(*refs):
        ins, outs, sems = refs[:n], refs[n:2 * n], refs[2 * n:]
        comm.start(ins, outs, sems)
        if comm.middle_at is not None:
            comm.middle(ins, outs, sems)
        comm.finish(ins, outs, sems)

    any_spec = pl.BlockSpec(memory_space=pl.ANY)
    return pl.pallas_call(
        body, name=name, in_specs=[any_spec] * n, out_specs=[any_spec] * n, out_shape=comm.out_shape(),
        scratch_shapes=comm.scratch(),
    )(*comm.arrays)


def _pallas(body, *, name, grid, in_specs, out_specs, out_shape, args, sem, scratch_shapes=(), comm=None):
    if comm is None:
        outs = pl.pallas_call(
            body, name=name, grid=grid, in_specs=in_specs, out_specs=out_specs, out_shape=out_shape,
            scratch_shapes=list(scratch_shapes), compiler_params=_params(*sem),
        )(*args)
        return outs, None
    n_in, n_out, n_scr, n_c = len(in_specs), len(out_specs), len(scratch_shapes), len(comm.arrays)
    steps = grid[0]
    middle = None if comm.middle_at is None else min(steps - 1, int(steps * comm.middle_at))

    def carried(*refs):
        ins, refs = refs[:n_in], refs[n_in:]
        cins, refs = refs[:n_c], refs[n_c:]
        outs, refs = refs[:n_out], refs[n_out:]
        couts, refs = refs[:n_c], refs[n_c:]
        scr, csems = refs[:n_scr], refs[n_scr:]
        step = pl.program_id(0)

        @pl.when(step == 0)
        def _():
            comm.start(cins, couts, csems)

        body(*ins, *outs, *scr)

        if middle is not None:
            @pl.when(step == middle)
            def _():
                comm.middle(cins, couts, csems)

        @pl.when(step == steps - 1)
        def _():
            comm.finish(cins, couts, csems)

    any_spec = pl.BlockSpec(memory_space=pl.ANY)
    res = pl.pallas_call(
        carried, name=name, grid=grid,
        in_specs=list(in_specs) + [any_spec] * n_c, out_specs=list(out_specs) + [any_spec] * n_c,
        out_shape=list(out_shape) + comm.out_shape(),
        scratch_shapes=list(scratch_shapes) + comm.scratch(),
        compiler_params=_params(*(("arbitrary",) + tuple(sem[1:]))),
    )(*args, *comm.arrays)
    return res[:n_out], res[n_out:]


def _inproj_fwd(x, g, w, b):
    T, D = x.shape
    rest_w = IN_W - QKV_W
    tm = _tile(T, 512)

    def body(x_ref, g_ref, w_ref, b_ref, h_ref, qkv_ref, rest_ref):
        xv = x_ref[...]
        r = lax.rsqrt(jnp.mean(xv * xv, axis=-1, keepdims=True) + EPS)
        h = (xv * r * g_ref[...]).astype(BF16)
        h_ref[...] = h
        qkv_ref[...] = (_dot(h, w_ref[:, :QKV_W]) + b_ref[:, :QKV_W]).astype(BF16)
        rest_ref[...] = _dot(h, w_ref[:, QKV_W:]) + b_ref[:, QKV_W:]

    return pl.pallas_call(
        body, name="inproj_fwd", grid=(T // tm,),
        in_specs=[_row_spec(tm, D), _full_spec((1, D)), _full_spec((D, IN_W)), _full_spec((1, IN_W))],
        out_specs=[_row_spec(tm, D), _row_spec(tm, QKV_W), _row_spec(tm, rest_w)],
        out_shape=[jax.ShapeDtypeStruct((T, D), BF16), jax.ShapeDtypeStruct((T, QKV_W), BF16),
                   jax.ShapeDtypeStruct((T, rest_w), F32)],
        compiler_params=_params("parallel"),
    )(x, g, w, b)


def _attn_masks(first):
    row = lax.broadcasted_iota(jnp.int32, (BLOCK, 2 * BLOCK), 0)
    col = lax.broadcasted_iota(jnp.int32, (BLOCK, 2 * BLOCK), 1)
    dist = row + BLOCK - col
    valid = (dist >= 0) & (dist < BLOCK)
    first_valid = valid & (col >= jnp.where(first, BLOCK, 0))
    return dist.astype(F32), valid, first_valid


def _attn_probs(qh, k, h, sink, distf, valid):
    scale = 1.0 / math.sqrt(HEAD_DIM)
    s = _dot_nt(qh, k) * scale - (2.0 ** (-8.0 * (h + 1) / N_Q)) * distf
    s = jnp.where(valid, s, NEG)
    m = jnp.maximum(jnp.max(s, axis=-1, keepdims=True), sink)
    p = jnp.exp(s - m)
    inv = 1.0 / (jnp.sum(p, axis=-1, keepdims=True) + jnp.exp(sink - m))
    return p * inv, jnp.exp(sink - m) * inv


def _attn_fwd(qkv, sinks):
    T = qkv.shape[0]
    tq = _tile(T, 512)
    nblk = tq // BLOCK

    def body(sink_ref, cur_ref, prev_ref, o_ref, kv_buf):
        i = pl.program_id(0)
        kv_buf[0:BLOCK, :] = prev_ref[:, ATTN_W:]
        kv_buf[BLOCK:, :] = cur_ref[:, ATTN_W:]
        distf, valid, first_valid = _attn_masks(i == 0)
        for j in range(nblk):
            band = kv_buf[j * BLOCK:(j + 2) * BLOCK, :]
            q = cur_ref[j * BLOCK:(j + 1) * BLOCK, :ATTN_W]
            outs = []
            for h in range(N_Q):
                kh = h // GROUP
                k = band[:, kh * HEAD_DIM:(kh + 1) * HEAD_DIM]
                v = band[:, KV_W + kh * HEAD_DIM:KV_W + (kh + 1) * HEAD_DIM]
                p, _ = _attn_probs(q[:, h * HEAD_DIM:(h + 1) * HEAD_DIM], k, h, sink_ref[h], distf,
                                   first_valid if j == 0 else valid)
                outs.append(_dot(p.astype(BF16), v))
            o_ref[j * BLOCK:(j + 1) * BLOCK, :] = jnp.concatenate(outs, axis=1).astype(BF16)

    return pl.pallas_call(
        body, name="attn_fwd", grid=(T // tq,),
        in_specs=[pl.BlockSpec(memory_space=pltpu.SMEM),
                  _row_spec(tq, QKV_W),
                  pl.BlockSpec((BLOCK, QKV_W), lambda i: (jnp.maximum(i * nblk - 1, 0), 0))],
        out_specs=_row_spec(tq, ATTN_W),
        out_shape=jax.ShapeDtypeStruct((T, ATTN_W), BF16),
        scratch_shapes=[pltpu.VMEM((tq + BLOCK, 2 * KV_W), BF16)],
        compiler_params=_params("parallel"),
    )(sinks, qkv, qkv)


def _conv_fwd(rest, cw, cb, lg, lb):
    T = rest.shape[0]
    C = CONV_C
    tm = _tile(T, 256)
    R = _tile(tm, CONV_ROWS)
    per = tm // HALO

    def body(cur_ref, prev_ref, w_ref, cb_ref, g_ref, b_ref, u0_ref, yc_ref, u_ref, ubuf):
        i = pl.program_id(0)
        up = prev_ref[:, :C] * _sigmoid(prev_ref[:, C:])
        ubuf[0:HALO, :] = jnp.where(i > 0, up, 0.0)
        u0 = cur_ref[:, :C] * _sigmoid(cur_ref[:, C:])
        ubuf[HALO:, :] = u0
        u0_ref[...] = u0
        off = HALO - (CONV_K - 1)
        for c in range(tm // R):
            acc = jnp.broadcast_to(cb_ref[...], (R, C))
            for j in range(CONV_K):
                acc = acc + w_ref[j:j + 1, :] * ubuf[c * R + off + j:c * R + off + j + R, :]
            yc_ref[c * R:(c + 1) * R, :] = acc
            xc = acc - jnp.mean(acc, axis=-1, keepdims=True)
            ln = xc * lax.rsqrt(jnp.mean(xc * xc, axis=-1, keepdims=True) + EPS) * g_ref[...] + b_ref[...]
            u_ref[c * R:(c + 1) * R, :] = (ln * _sigmoid(ln)).astype(BF16)

    return pl.pallas_call(
        body, name="conv_fwd", grid=(T // tm,),
        in_specs=[_row_spec(tm, 2 * C),
                  pl.BlockSpec((HALO, 2 * C), lambda i: (jnp.maximum(i * per - 1, 0), 0)),
                  _full_spec((CONV_K, C)), _full_spec((1, C)), _full_spec((1, C)), _full_spec((1, C))],
        out_specs=[_row_spec(tm, C), _row_spec(tm, C), _row_spec(tm, C)],
        out_shape=[jax.ShapeDtypeStruct((T, C), F32), jax.ShapeDtypeStruct((T, C), F32),
                   jax.ShapeDtypeStruct((T, C), BF16)],
        scratch_shapes=[pltpu.VMEM((tm + HALO, C), F32)],
        compiler_params=_params("parallel"),
    )(rest, rest, cw, cb, lg, lb)


def _merge_fwd(attn, u, rest, x, wa, wc, bc, wo):
    T, D = x.shape
    tm = _tile(T, 512)
    gcol = 2 * CONV_C // D

    def body(attn_ref, u_ref, ga_ref, gc_ref, x_ref, wa_ref, wc_ref, bc_ref, wo_ref, m_ref, x1_ref):
        bra = _dot(attn_ref[...], wa_ref[...])
        brc = _dot(u_ref[...], wc_ref[...]) + bc_ref[...]
        mb = (_sigmoid(ga_ref[...]) * bra + _sigmoid(gc_ref[...]) * brc).astype(BF16)
        m_ref[...] = mb
        x1_ref[...] = x_ref[...] + _dot(mb, wo_ref[...])

    return pl.pallas_call(
        body, name="merge_fwd", grid=(T // tm,),
        in_specs=[_row_spec(tm, ATTN_W), _row_spec(tm, CONV_C), _row_spec(tm, D, gcol), _row_spec(tm, D, gcol + 1),
                  _row_spec(tm, D), _full_spec((ATTN_W, D)), _full_spec((CONV_C, D)), _full_spec((1, D)),
                  _full_spec((D, D))],
        out_specs=[_row_spec(tm, D), _row_spec(tm, D)],
        out_shape=[jax.ShapeDtypeStruct((T, D), BF16), jax.ShapeDtypeStruct((T, D), F32)],
        compiler_params=_params("parallel"),
    )(attn, u, rest, rest, x, wa, wc, bc, wo)


def _mlp_fwd(x1, g, w1, w2, comm=None):
    T, D = x1.shape
    tm = _tile(T, 256)

    def body(x_ref, g_ref, w1_ref, w2_ref, h_ref, z_ref, o_ref):
        xv = x_ref[...]
        r = lax.rsqrt(jnp.mean(xv * xv, axis=-1, keepdims=True) + EPS)
        h = (xv * r * g_ref[...]).astype(BF16)
        h_ref[...] = h
        z = _dot(h, w1_ref[...])
        z_ref[...] = z.astype(BF16)
        a = jnp.square(jnp.maximum(z, 0.0)).astype(BF16)
        o_ref[...] = xv + _dot(a, w2_ref[...])

    return _pallas(
        body, name="mlp_fwd", grid=(T // tm,),
        in_specs=[_row_spec(tm, D), _full_spec((1, D)), _full_spec((D, D_FF)), _full_spec((D_FF, D))],
        out_specs=[_row_spec(tm, D), _row_spec(tm, D_FF), _row_spec(tm, D)],
        out_shape=[jax.ShapeDtypeStruct((T, D), BF16), jax.ShapeDtypeStruct((T, D_FF), BF16),
                   jax.ShapeDtypeStruct((T, D), F32)],
        args=(x1, g, w1, w2), sem=("parallel",), comm=comm)


def _final_loss(x, g, target):
    T, D = x.shape
    tm = _tile(T, 512)

    def body(x_ref, g_ref, t_ref, l_ref, dx_ref, dg_ref):
        @pl.when(pl.program_id(0) == 0)
        def _():
            l_ref[...] = jnp.zeros_like(l_ref)
            dg_ref[...] = jnp.zeros_like(dg_ref)

        xv = x_ref[...]
        r = lax.rsqrt(jnp.mean(xv * xv, axis=-1, keepdims=True) + EPS)
        e = xv * r * g_ref[...] - t_ref[...]
        l_ref[...] += _rows8(e * e) * (0.5 / D)
        dx, dg = _rms_bwd(xv, g_ref[...], e * (1.0 / D))
        dx_ref[...] = dx
        dg_ref[...] += _rows8(dg)

    return pl.pallas_call(
        body, name="final_loss", grid=(T // tm,),
        in_specs=[_row_spec(tm, D), _full_spec((1, D)), _row_spec(tm, D)],
        out_specs=[_full_spec((SUBLANES, D)), _row_spec(tm, D), _full_spec((SUBLANES, D))],
        out_shape=[jax.ShapeDtypeStruct((SUBLANES, D), F32), jax.ShapeDtypeStruct((T, D), F32),
                   jax.ShapeDtypeStruct((SUBLANES, D), F32)],
        compiler_params=_params("arbitrary"),
    )(x, g, target)


def _mlp_bwd(dx2, x1, z, g, w1, w2, comm=None):
    T, D = x1.shape
    tm = _tile(T, 256)

    def body(dx2_ref, x_ref, z_ref, g_ref, w1_ref, w2_ref, dx1_ref, dz_ref, dg_ref):
        @pl.when(pl.program_id(0) == 0)
        def _():
            dg_ref[...] = jnp.zeros_like(dg_ref)

        dxo = dx2_ref[...]
        da = _dot_nt(dxo.astype(BF16), w2_ref[...])
        dz = (da * (2.0 * jnp.maximum(z_ref[...].astype(F32), 0.0))).astype(BF16)
        dz_ref[...] = dz
        dh = _dot_nt(dz, w1_ref[...])
        dx, dg = _rms_bwd(x_ref[...], g_ref[...], dh)
        dx1_ref[...] = dxo + dx
        dg_ref[...] += _rows8(dg)

    return _pallas(
        body, name="mlp_bwd", grid=(T // tm,),
        in_specs=[_row_spec(tm, D), _row_spec(tm, D), _row_spec(tm, D_FF), _full_spec((1, D)),
                  _full_spec((D, D_FF)), _full_spec((D_FF, D))],
        out_specs=[_row_spec(tm, D), _row_spec(tm, D_FF), _full_spec((SUBLANES, D))],
        out_shape=[jax.ShapeDtypeStruct((T, D), F32), jax.ShapeDtypeStruct((T, D_FF), BF16),
                   jax.ShapeDtypeStruct((SUBLANES, D), F32)],
        args=(dx2, x1, z, g, w1, w2), sem=("arbitrary",), comm=comm)


def _tn_blocks(a, b, name, col_sharded, relu_sq=False):
    T, M = a.shape
    N = b.shape[1]
    tk = _tile(T, 1024)
    tm = _tile(M, 512 if col_sharded else 1024)
    nb = N // N_DEV
    last = T // tk - 1

    def body(a_ref, b_ref, o_ref, acc_ref):
        k = pl.program_id(1)

        @pl.when(k == 0)
        def _():
            acc_ref[...] = jnp.zeros_like(acc_ref)

        av = a_ref[...]
        if relu_sq:
            av = jnp.square(jnp.maximum(av.astype(F32), 0.0))
        acc_ref[...] += _dot_tn(av.astype(BF16), b_ref[...].astype(BF16))

        @pl.when(k == last)
        def _():
            if col_sharded:
                for d in range(N_DEV):
                    o_ref[d] = acc_ref[:, d * nb:(d + 1) * nb].astype(BF16)
            else:
                o_ref[...] = acc_ref[...].astype(BF16)

    if col_sharded:
        out_spec = pl.BlockSpec((N_DEV, tm, nb), lambda i, k: (0, i, 0))
        out_shape = jax.ShapeDtypeStruct((N_DEV, M, nb), BF16)
    else:
        out_spec = pl.BlockSpec((tm, N), lambda i, k: (i, 0))
        out_shape = jax.ShapeDtypeStruct((M, N), BF16)
    out = pl.pallas_call(
        body, name=name, grid=(M // tm, T // tk),
        in_specs=[pl.BlockSpec((tk, tm), lambda i, k: (k, i)), pl.BlockSpec((tk, N), lambda i, k: (k, 0))],
        out_specs=out_spec, out_shape=out_shape,
        scratch_shapes=[pltpu.VMEM((tm, N), F32)],
        compiler_params=_params("parallel", "arbitrary"),
    )(a, b)
    return out if col_sharded else out.reshape(N_DEV, M // N_DEV, N)


def _merge_bwd(dx1, attn, u, rest, wa, wc, bc, wo):
    T, D = dx1.shape
    tm = _tile(T, 256)
    gcol = 2 * CONV_C // D

    def body(dx_ref, attn_ref, u_ref, ga_ref, gc_ref, wa_ref, wc_ref, bc_ref, wo_ref,
             dattn_ref, du_ref, dga_ref, dgc_ref, dbra_ref, dbrc_ref, dbc_ref):
        @pl.when(pl.program_id(0) == 0)
        def _():
            dbc_ref[...] = jnp.zeros_like(dbc_ref)

        dm = _dot_nt(dx_ref[...].astype(BF16), wo_ref[...])
        bra = _dot(attn_ref[...], wa_ref[...])
        brc = _dot(u_ref[...], wc_ref[...]) + bc_ref[...]
        sa = _sigmoid(ga_ref[...])
        sc = _sigmoid(gc_ref[...])
        dbra = dm * sa
        dbrc = dm * sc
        dga_ref[...] = (dm * bra * sa * (1.0 - sa)).astype(BF16)
        dgc_ref[...] = (dm * brc * sc * (1.0 - sc)).astype(BF16)
        dbra_b = dbra.astype(BF16)
        dbrc_b = dbrc.astype(BF16)
        dbra_ref[...] = dbra_b
        dbrc_ref[...] = dbrc_b
        dbc_ref[...] += _rows8(dbrc)
        dattn_ref[...] = _dot_nt(dbra_b, wa_ref[...]).astype(BF16)
        du_ref[...] = _dot_nt(dbrc_b, wc_ref[...])

    return pl.pallas_call(
        body, name="merge_bwd", grid=(T // tm,),
        in_specs=[_row_spec(tm, D), _row_spec(tm, ATTN_W), _row_spec(tm, CONV_C), _row_spec(tm, D, gcol),
                  _row_spec(tm, D, gcol + 1), _full_spec((ATTN_W, D)), _full_spec((CONV_C, D)), _full_spec((1, D)),
                  _full_spec((D, D))],
        out_specs=[_row_spec(tm, ATTN_W), _row_spec(tm, CONV_C), _row_spec(tm, D), _row_spec(tm, D),
                   _row_spec(tm, D), _row_spec(tm, D), _full_spec((SUBLANES, D))],
        out_shape=[jax.ShapeDtypeStruct((T, ATTN_W), BF16), jax.ShapeDtypeStruct((T, CONV_C), F32),
                   jax.ShapeDtypeStruct((T, D), BF16), jax.ShapeDtypeStruct((T, D), BF16),
                   jax.ShapeDtypeStruct((T, D), BF16), jax.ShapeDtypeStruct((T, D), BF16),
                   jax.ShapeDtypeStruct((SUBLANES, D), F32)],
        compiler_params=_params("arbitrary"),
    )(dx1, attn, u, rest, rest, wa, wc, bc, wo)


def _conv_bwd_norm(du, yc, lg, lb):
    T, C = yc.shape
    tm = _tile(T, 512)

    def body(du_ref, yc_ref, g_ref, b_ref, dyc_ref, dg_ref, db_ref):
        @pl.when(pl.program_id(0) == 0)
        def _():
            dg_ref[...] = jnp.zeros_like(dg_ref)
            db_ref[...] = jnp.zeros_like(db_ref)

        yv = yc_ref[...]
        xc = yv - jnp.mean(yv, axis=-1, keepdims=True)
        rstd = lax.rsqrt(jnp.mean(xc * xc, axis=-1, keepdims=True) + EPS)
        xn = xc * rstd
        ln = xn * g_ref[...] + b_ref[...]
        sg = _sigmoid(ln)
        dln = du_ref[...] * sg * (1.0 + ln * (1.0 - sg))
        dg_ref[...] += _rows8(dln * xn)
        db_ref[...] += _rows8(dln)
        dxn = dln * g_ref[...]
        dyc_ref[...] = rstd * (dxn - jnp.mean(dxn, axis=-1, keepdims=True)
                               - xn * jnp.mean(dxn * xn, axis=-1, keepdims=True))

    return pl.pallas_call(
        body, name="conv_bwd_norm", grid=(T // tm,),
        in_specs=[_row_spec(tm, C), _row_spec(tm, C), _full_spec((1, C)), _full_spec((1, C))],
        out_specs=[_row_spec(tm, C), _full_spec((SUBLANES, C)), _full_spec((SUBLANES, C))],
        out_shape=[jax.ShapeDtypeStruct((T, C), F32), jax.ShapeDtypeStruct((SUBLANES, C), F32),
                   jax.ShapeDtypeStruct((SUBLANES, C), F32)],
        compiler_params=_params("arbitrary"),
    )(du, yc, lg, lb)


def _conv_bwd_taps(dyc, u0, rest, cw, comm=None):
    T, C = dyc.shape
    tm = _tile(T, 256)
    R = _tile(tm, CONV_ROWS)
    per = tm // HALO
    nt = T // tm

    def body(dy_ref, dyn_ref, u0_ref, u0p_ref, glu_ref, w_ref, dga_ref, dgb_ref, dw_ref, db_ref, dbuf, ubuf):
        i = pl.program_id(0)

        @pl.when(i == 0)
        def _():
            dw_ref[...] = jnp.zeros_like(dw_ref)
            db_ref[...] = jnp.zeros_like(db_ref)

        dbuf[0:tm, :] = dy_ref[...]
        dbuf[tm:, :] = jnp.where(i < nt - 1, dyn_ref[...], 0.0)
        ubuf[0:HALO, :] = jnp.where(i > 0, u0p_ref[...], 0.0)
        ubuf[HALO:, :] = u0_ref[...]
        off = HALO - (CONV_K - 1)
        for c in range(tm // R):
            dy = dbuf[c * R:(c + 1) * R, :]
            acc = jnp.zeros((R, C), F32)
            for j in range(CONV_K):
                acc = acc + w_ref[j:j + 1, :] * dbuf[c * R + CONV_K - 1 - j:c * R + CONV_K - 1 - j + R, :]
                dw_ref[j * SUBLANES:(j + 1) * SUBLANES, :] += _rows8(
                    dy * ubuf[c * R + off + j:c * R + off + j + R, :])
            db_ref[...] += _rows8(dy)
            a = glu_ref[c * R:(c + 1) * R, :C]
            sb = _sigmoid(glu_ref[c * R:(c + 1) * R, C:])
            dga_ref[c * R:(c + 1) * R, :] = (acc * sb).astype(BF16)
            dgb_ref[c * R:(c + 1) * R, :] = (acc * a * sb * (1.0 - sb)).astype(BF16)

    return _pallas(
        body, name="conv_bwd_taps", grid=(nt,),
        in_specs=[_row_spec(tm, C),
                  pl.BlockSpec((HALO, C), lambda i: (jnp.minimum((i + 1) * per, T // HALO - 1), 0)),
                  _row_spec(tm, C),
                  pl.BlockSpec((HALO, C), lambda i: (jnp.maximum(i * per - 1, 0), 0)),
                  _row_spec(tm, 2 * C), _full_spec((CONV_K, C))],
        out_specs=[_row_spec(tm, C), _row_spec(tm, C), _full_spec((CONV_K * SUBLANES, C)),
                   _full_spec((SUBLANES, C))],
        out_shape=[jax.ShapeDtypeStruct((T, C), BF16), jax.ShapeDtypeStruct((T, C), BF16),
                   jax.ShapeDtypeStruct((CONV_K * SUBLANES, C), F32), jax.ShapeDtypeStruct((SUBLANES, C), F32)],
        scratch_shapes=[pltpu.VMEM((tm + HALO, C), F32), pltpu.VMEM((tm + HALO, C), F32)],
        args=(dyc, dyc, u0, u0, rest, cw), sem=("arbitrary",), comm=comm)


def _attn_bwd(qkv, dattn, sinks, comm=None):
    T = qkv.shape[0]
    tq = _tile(T, 512)
    nblk = tq // BLOCK
    scale = 1.0 / math.sqrt(HEAD_DIM)

    def body(sink_ref, cur_ref, prev_ref, do_ref, dq_ref, hi_ref, lo_ref, ds_ref, kv_buf):
        i = pl.program_id(0)

        @pl.when(i == 0)
        def _():
            ds_ref[...] = jnp.zeros_like(ds_ref)

        kv_buf[0:BLOCK, :] = prev_ref[:, ATTN_W:]
        kv_buf[BLOCK:, :] = cur_ref[:, ATTN_W:]
        distf, valid, first_valid = _attn_masks(i == 0)
        for j in range(nblk):
            rows = slice(j * BLOCK, (j + 1) * BLOCK)
            band = kv_buf[j * BLOCK:(j + 2) * BLOCK, :]
            q = cur_ref[rows, :ATTN_W]
            do = do_ref[rows, :]
            dqs, dks, dvs = [], [], []
            for kh in range(N_KV):
                k = band[:, kh * HEAD_DIM:(kh + 1) * HEAD_DIM]
                v = band[:, KV_W + kh * HEAD_DIM:KV_W + (kh + 1) * HEAD_DIM]
                ps, dss, qs, dos = [], [], [], []
                for g in range(GROUP):
                    h = kh * GROUP + g
                    qh = q[:, h * HEAD_DIM:(h + 1) * HEAD_DIM]
                    doh = do[:, h * HEAD_DIM:(h + 1) * HEAD_DIM]
                    p, psink = _attn_probs(qh, k, h, sink_ref[h], distf, first_valid if j == 0 else valid)
                    pdp = p * _dot_nt(doh, v)
                    delta = jnp.sum(pdp, axis=-1, keepdims=True)
                    dsb = ((pdp - p * delta) * scale).astype(BF16)
                    sink_terms = _rows8(psink * pdp)
                    ds_ref[h * SUBLANES:(h + 1) * SUBLANES, :] += sink_terms[:, :BLOCK] + sink_terms[:, BLOCK:]
                    dqs.append(_dot(dsb, k))
                    ps.append(p.astype(BF16))
                    dss.append(dsb)
                    qs.append(qh)
                    dos.append(doh)
                dks.append(_dot_tn(jnp.concatenate(dss, axis=0), jnp.concatenate(qs, axis=0)))
                dvs.append(_dot_tn(jnp.concatenate(ps, axis=0), jnp.concatenate(dos, axis=0)))
            dq_ref[rows, :] = jnp.concatenate(dqs, axis=1).astype(BF16)
            dkv = jnp.concatenate(dks + dvs, axis=1)
            lo_ref[rows, :] = dkv[:BLOCK, :]
            hi_ref[rows, :] = dkv[BLOCK:, :]

    return _pallas(
        body, name="attn_bwd", grid=(T // tq,),
        in_specs=[pl.BlockSpec(memory_space=pltpu.SMEM),
                  _row_spec(tq, QKV_W),
                  pl.BlockSpec((BLOCK, QKV_W), lambda i: (jnp.maximum(i * nblk - 1, 0), 0)),
                  _row_spec(tq, ATTN_W)],
        out_specs=[_row_spec(tq, ATTN_W), _row_spec(tq, 2 * KV_W), _row_spec(tq, 2 * KV_W),
                   _full_spec((N_Q * SUBLANES, BLOCK))],
        out_shape=[jax.ShapeDtypeStruct((T, ATTN_W), BF16), jax.ShapeDtypeStruct((T, 2 * KV_W), F32),
                   jax.ShapeDtypeStruct((T, 2 * KV_W), F32), jax.ShapeDtypeStruct((N_Q * SUBLANES, BLOCK), F32)],
        scratch_shapes=[pltpu.VMEM((tq + BLOCK, 2 * KV_W), BF16)],
        args=(sinks, qkv, qkv, dattn), sem=("arbitrary",), comm=comm)


def _inproj_bwd(dq, hi, lo, dglu_a, dglu_b, dga, dgc, x, g, w, dx1):
    T, D = x.shape
    C = CONV_C
    tm = _tile(T, 256)
    per = tm // BLOCK
    nt = T // tm
    kv2 = 2 * KV_W

    def body(dq_ref, hi_ref, lo_ref, lon_ref, da_ref, db_ref, dga_ref, dgc_ref, x_ref, g_ref, w_ref, dx1_ref,
             dp_ref, dx_ref, dg_ref, dbias_ref):
        i = pl.program_id(0)

        @pl.when(i == 0)
        def _():
            dg_ref[...] = jnp.zeros_like(dg_ref)
            dbias_ref[...] = jnp.zeros_like(dbias_ref)

        dp_ref[:, :ATTN_W] = dq_ref[...]
        lo_next = jnp.where(i < nt - 1, lon_ref[...], 0.0)
        if tm > BLOCK:
            lo_shift = jnp.concatenate([lo_ref[BLOCK:, :], lo_next], axis=0)
        else:
            lo_shift = lo_next
        dp_ref[:, ATTN_W:QKV_W] = (hi_ref[...] + lo_shift).astype(BF16)
        dp_ref[:, QKV_W:QKV_W + C] = da_ref[...]
        dp_ref[:, QKV_W + C:QKV_W + 2 * C] = db_ref[...]
        dp_ref[:, QKV_W + 2 * C:QKV_W + 2 * C + D] = dga_ref[...]
        dp_ref[:, QKV_W + 2 * C + D:] = dgc_ref[...]
        dp = dp_ref[...]
        dbias_ref[...] += _rows8(dp.astype(F32))
        dh = _dot_nt(dp, w_ref[...])
        dx, dg = _rms_bwd(x_ref[...], g_ref[...], dh)
        dx_ref[...] = dx1_ref[...] + dx
        dg_ref[...] += _rows8(dg)

    return pl.pallas_call(
        body, name="inproj_bwd", grid=(nt,),
        in_specs=[_row_spec(tm, ATTN_W), _row_spec(tm, kv2), _row_spec(tm, kv2),
                  pl.BlockSpec((BLOCK, kv2), lambda i: (jnp.minimum((i + 1) * per, T // BLOCK - 1), 0)),
                  _row_spec(tm, C), _row_spec(tm, C), _row_spec(tm, D), _row_spec(tm, D),
                  _row_spec(tm, D), _full_spec((1, D)), _full_spec((D, IN_W)), _row_spec(tm, D)],
        out_specs=[_row_spec(tm, IN_W), _row_spec(tm, D), _full_spec((SUBLANES, D)), _full_spec((SUBLANES, IN_W))],
        out_shape=[jax.ShapeDtypeStruct((T, IN_W), BF16), jax.ShapeDtypeStruct((T, D), F32),
                   jax.ShapeDtypeStruct((SUBLANES, D), F32), jax.ShapeDtypeStruct((SUBLANES, IN_W), F32)],
        compiler_params=_params("arbitrary"),
    )(dq, hi, lo, lo, dglu_a, dglu_b, dga, dgc, x, g, w, dx1)


def _adamw_math(g, w, m, v):
    c1 = 1.0 / (1.0 - ADAM_B1 ** ADAM_STEP)
    c2 = 1.0 / (1.0 - ADAM_B2 ** ADAM_STEP)
    mn = ADAM_B1 * m + (1.0 - ADAM_B1) * g
    vn = ADAM_B2 * v + (1.0 - ADAM_B2) * (g * g)
    return -ADAM_LR * ((mn * c1) / (jnp.sqrt(vn * c2) + ADAM_EPS) + ADAM_WD * w), mn, vn


def _adamw_sharded(parts, w, m, v, name):
    depth, a, b = w.shape
    tr = _tile(a, 256) if a % SUBLANES == 0 else a
    nr = a // tr

    def body(*refs):
        p_refs, (w_ref, m_ref, v_ref, g_ref, d_ref, mo_ref, vo_ref) = refs[:depth], refs[depth:]
        layer = pl.program_id(0)
        for l in range(depth):
            @pl.when(layer == l)
            def _(l=l):
                g = p_refs[l][0].astype(F32)
                for s in range(1, N_DEV):
                    g = g + p_refs[l][s].astype(F32)
                g_ref[...] = g
                d_ref[...], mo_ref[...], vo_ref[...] = _adamw_math(g, w_ref[...], m_ref[...], v_ref[...])

    def part_spec(l):
        return pl.BlockSpec((N_DEV, tr, b),
                            lambda k, i: (0, jnp.where(k == l, i, jnp.where(k < l, 0, nr - 1)), 0))

    spec = pl.BlockSpec((None, tr, b), lambda k, i: (k, i, 0))
    out = jax.ShapeDtypeStruct((depth, a, b), F32)
    return pl.pallas_call(
        body, name=name, grid=(depth, nr),
        in_specs=[part_spec(l) for l in range(depth)] + [spec] * 3,
        out_specs=[spec] * 4, out_shape=[out] * 4,
        compiler_params=_params("arbitrary", "arbitrary"),
    )(*parts, w, m, v)


def _adamw_small(parts, w, m, v):
    R, N = w.shape

    def body(p_ref, w_ref, m_ref, v_ref, g_ref, d_ref, mo_ref, vo_ref):
        g = p_ref[0]
        for s in range(1, N_DEV):
            g = g + p_ref[s]
        g_ref[...] = g
        d_ref[...], mo_ref[...], vo_ref[...] = _adamw_math(g, w_ref[...], m_ref[...], v_ref[...])

    out = jax.ShapeDtypeStruct((R, N), F32)
    return pl.pallas_call(
        body, name="adamw_small", grid=(1,),
        in_specs=[_full_spec((N_DEV, R, N))] + [_full_spec((R, N))] * 3,
        out_specs=[_full_spec((R, N))] * 4, out_shape=[out] * 4,
        compiler_params=_params("arbitrary"),
    )(parts, w, m, v)


_SHARDED = ("w_in", "conv_w", "w_attn_proj", "w_conv_proj", "w_out", "w_mlp1", "w_mlp2")
_ROW_SHARDED = ("w_out", "w_mlp2")
_SMALL = ("mix_norm_g", "b_in", "sinks", "conv_b", "conv_ln_g", "conv_ln_b", "b_conv_proj", "mlp_norm_g",
          "final_norm_g")
_ORDER = ("mix_norm_g", "w_in", "b_in", "sinks", "conv_w", "conv_b", "conv_ln_g", "conv_ln_b", "w_attn_proj",
          "w_conv_proj", "b_conv_proj", "w_out", "mlp_norm_g", "w_mlp1", "w_mlp2", "final_norm_g")
_PACK = 1024


def _full_weight(name, gathered):
    _, a, b = gathered.shape
    if name in _ROW_SHARDED:
        return gathered.reshape(N_DEV * a, b)
    return gathered.transpose(1, 0, 2).reshape(a, N_DEV * b)


def _pack(arrs):
    flat = []
    for a in arrs:
        a = a.reshape(-1)
        flat.append(jnp.pad(a, (0, -a.size % _PACK)))
    return jnp.concatenate(flat).reshape(-1, BLOCK)


def _unpack(packed, shapes):
    flat = packed.reshape(-1)
    out, off = [], 0
    for s in shapes:
        n = math.prod(s)
        out.append(flat[off:off + n].reshape(s))
        off += n + (-n % _PACK)
    return out


def _layer_fwd(x, lw, comm=None):
    h, qkv, rest = _inproj_fwd(x, lw["mix_norm_g"], lw["w_in"], lw["b_in"])
    attn = _attn_fwd(qkv, lw["sinks"])
    u0, yc, u = _conv_fwd(rest, lw["conv_w"], lw["conv_b"], lw["conv_ln_g"], lw["conv_ln_b"])
    merged, x1 = _merge_fwd(attn, u, rest, x, lw["w_attn_proj"], lw["w_conv_proj"], lw["b_conv_proj"], lw["w_out"])
    (h2, z, x2), gathered = _mlp_fwd(x1, lw["mlp_norm_g"], lw["w_mlp1"], lw["w_mlp2"], comm)
    saved = dict(x=x, h=h, qkv=qkv, rest=rest, attn=attn, u0=u0, yc=yc, u=u, merged=merged, x1=x1, h2=h2, z=z)
    return x2, saved, gathered


_EARLY = ("w_mlp1", "w_mlp2")
_MIDDLE = ("w_out", "w_attn_proj", "w_conv_proj")
_LATE = ("w_in", "conv_w")


def _layer_bwd(dx2, lw, s, late_blocks=None):
    g, recv = {}, {}
    late = None if late_blocks is None else _Exchange(late_blocks)
    (dx1, dz, dg2), late_recv = _mlp_bwd(dx2, s["x1"], s["z"], lw["mlp_norm_g"], lw["w_mlp1"], lw["w_mlp2"], late)
    g["mlp_norm_g"] = jnp.sum(dg2, axis=0)
    early = [_tn_blocks(s["h2"], dz, "dw_mlp1", True), _tn_blocks(s["z"], dx2, "dw_mlp2", False, relu_sq=True)]
    dattn, du, dga, dgc, dbra, dbrc, dbc = _merge_bwd(
        dx1, s["attn"], s["u"], s["rest"], lw["w_attn_proj"], lw["w_conv_proj"], lw["b_conv_proj"], lw["w_out"])
    g["b_conv_proj"] = jnp.sum(dbc, axis=0)
    middle = [_tn_blocks(s["merged"], dx1, "dw_out", False), _tn_blocks(s["attn"], dbra, "dw_attn_proj", True),
              _tn_blocks(s["u"], dbrc, "dw_conv_proj", True)]
    dyc, dlg, dlb = _conv_bwd_norm(du, s["yc"], lw["conv_ln_g"], lw["conv_ln_b"])
    g["conv_ln_g"] = jnp.sum(dlg, axis=0)
    g["conv_ln_b"] = jnp.sum(dlb, axis=0)
    (dglu_a, dglu_b, dcw, dcb), early_recv = _conv_bwd_taps(dyc, s["u0"], s["rest"], lw["conv_w"], _Exchange(early))
    recv.update(zip(_EARLY, early_recv))
    dconv_w = jnp.sum(dcw.reshape(CONV_K, SUBLANES, CONV_C), axis=1)
    g["conv_b"] = jnp.sum(dcb, axis=0)
    (dq, hi, lo, dsk), middle_recv = _attn_bwd(s["qkv"], dattn, lw["sinks"], _Exchange(middle))
    recv.update(zip(_MIDDLE, middle_recv))
    g["sinks"] = -jnp.sum(dsk.reshape(N_Q, SUBLANES * BLOCK), axis=1)
    dproj, dx, dg1, dbin = _inproj_bwd(dq, hi, lo, dglu_a, dglu_b, dga, dgc, s["x"], lw["mix_norm_g"], lw["w_in"],
                                       dx1)
    g["mix_norm_g"] = jnp.sum(dg1, axis=0)
    g["b_in"] = jnp.sum(dbin, axis=0)
    own_late = [_tn_blocks(s["h"], dproj, "dw_in", True),
                dconv_w.reshape(CONV_K, N_DEV, CONV_C // N_DEV).transpose(1, 0, 2)]
    return dx, g, recv, late_recv, own_late


def kernel(x, mix_norm_g, w_in, b_in, sinks, conv_w, conv_b, conv_ln_g, conv_ln_b, w_attn_proj, w_conv_proj, b_conv_proj, w_out, mlp_norm_g, w_mlp1, w_mlp2, final_norm_g, loss_target, m_mix_norm_g, m_w_in, m_b_in, m_sinks, m_conv_w, m_conv_b, m_conv_ln_g, m_conv_ln_b, m_w_attn_proj, m_w_conv_proj, m_b_conv_proj, m_w_out, m_mlp_norm_g, m_w_mlp1, m_w_mlp2, m_final_norm_g, v_mix_norm_g, v_w_in, v_b_in, v_sinks, v_conv_w, v_conv_b, v_conv_ln_g, v_conv_ln_b, v_w_attn_proj, v_w_conv_proj, v_b_conv_proj, v_w_out, v_mlp_norm_g, v_w_mlp1, v_w_mlp2, v_final_norm_g):
    w = dict(mix_norm_g=mix_norm_g, w_in=w_in, b_in=b_in, sinks=sinks, conv_w=conv_w, conv_b=conv_b,
             conv_ln_g=conv_ln_g, conv_ln_b=conv_ln_b, w_attn_proj=w_attn_proj, w_conv_proj=w_conv_proj,
             b_conv_proj=b_conv_proj, w_out=w_out, mlp_norm_g=mlp_norm_g, w_mlp1=w_mlp1, w_mlp2=w_mlp2,
             final_norm_g=final_norm_g)
    m = dict(mix_norm_g=m_mix_norm_g, w_in=m_w_in, b_in=m_b_in, sinks=m_sinks, conv_w=m_conv_w, conv_b=m_conv_b,
             conv_ln_g=m_conv_ln_g, conv_ln_b=m_conv_ln_b, w_attn_proj=m_w_attn_proj, w_conv_proj=m_w_conv_proj,
             b_conv_proj=m_b_conv_proj, w_out=m_w_out, mlp_norm_g=m_mlp_norm_g, w_mlp1=m_w_mlp1, w_mlp2=m_w_mlp2,
             final_norm_g=m_final_norm_g)
    v = dict(mix_norm_g=v_mix_norm_g, w_in=v_w_in, b_in=v_b_in, sinks=v_sinks, conv_w=v_conv_w, conv_b=v_conv_b,
             conv_ln_g=v_conv_ln_g, conv_ln_b=v_conv_ln_b, w_attn_proj=v_w_attn_proj, w_conv_proj=v_w_conv_proj,
             b_conv_proj=v_b_conv_proj, w_out=v_w_out, mlp_norm_g=v_mlp_norm_g, w_mlp1=v_w_mlp1, w_mlp2=v_w_mlp2,
             final_norm_g=v_final_norm_g)
    T = x.shape[1]
    xs = x.reshape(T, D_MODEL)
    target = loss_target.reshape(T, D_MODEL)

    def gather_of(l):
        return _Gather([w[n][l] if n == "conv_w" else w[n][l].astype(BF16) for n in _SHARDED])

    def layer_weights(l, gathered):
        lw = {n: _full_weight(n, a) for n, a in zip(_SHARDED, gathered)}
        for n in _SMALL:
            if n != "final_norm_g":
                lw[n] = w[n][l] if n == "sinks" else w[n][l].reshape(1, -1)
        return lw

    acts = xs
    saved, weights = [], []
    gathered = _run_comm(gather_of(0), "gather_weights")
    for l in range(DEPTH):
        weights.append(layer_weights(l, gathered))
        acts, s, gathered = _layer_fwd(acts, weights[l], gather_of(l + 1) if l + 1 < DEPTH else None)
        saved.append(s)
    lterms, dx, dgf = _final_loss(acts, final_norm_g.reshape(1, -1), target)
    grads, received = [None] * DEPTH, [None] * DEPTH
    late = None
    for l in reversed(range(DEPTH)):
        dx, grads[l], received[l], late_recv, late = _layer_bwd(dx, weights[l], saved[l], late)
        if late_recv is not None:
            received[l + 1].update(zip(_LATE, late_recv))
    received[0].update(zip(_LATE, _run_comm(_Exchange(late), "scatter_late")))
    grad = {n: jnp.stack([grads[l][n] for l in range(DEPTH)]) for n in _SMALL if n != "final_norm_g"}
    grad["final_norm_g"] = jnp.sum(dgf, axis=0)

    small_shapes = [w[n].shape for n in _SMALL] + [(1,)]
    small = _pack([grad[n] for n in _SMALL] + [jnp.sum(lterms).reshape(1)])
    small_parts = _run_comm(_Gather([small]), "gather_small")[0]

    out_g, out_d, out_m, out_v = {}, {}, {}, {}
    for n in _SHARDED:
        out_g[n], out_d[n], out_m[n], out_v[n] = _adamw_sharded(
            [received[l][n] for l in range(DEPTH)], w[n], m[n], v[n], "adamw_" + n)
    zero = jnp.zeros((1,), F32)
    res = _adamw_small(small_parts, _pack([w[n] for n in _SMALL] + [zero]), _pack([m[n] for n in _SMALL] + [zero]),
                       _pack([v[n] for n in _SMALL] + [zero]))
    unpacked = [_unpack(r, small_shapes) for r in res]
    for i, n in enumerate(_SMALL):
        out_g[n], out_d[n], out_m[n], out_v[n] = (u[i] for u in unpacked)
    loss = unpacked[0][-1].reshape(())
    return (loss, dx.reshape(x.shape), *[out_g[n] for n in _ORDER], *[out_d[n] for n in _ORDER],
            *[out_m[n] for n in _ORDER], *[out_v[n] for n in _ORDER])
```

```python
import functools
import math

import jax
import jax.numpy as jnp
from jax import lax
from jax.experimental import pallas as pl
from jax.experimental.pallas import tpu as pltpu

D_MODEL = 1024
SEQ = 16384
DEPTH = 2
N_Q = 8
N_KV = 2
GROUP = N_Q // N_KV
HEAD_DIM = 64
ATTN_W = N_Q * HEAD_DIM
KV_W = N_KV * HEAD_DIM
BLOCK = 128
CONV_C = D_MODEL // 2
CONV_K = 31
D_FF = 4 * D_MODEL
QKV_W = ATTN_W + 2 * KV_W
IN_W = QKV_W + 2 * CONV_C + 2 * D_MODEL
EPS = 1e-6
NEG = -1e30
N_DEV = 8

ADAM_LR = 0.001
ADAM_B1 = 0.9
ADAM_B2 = 0.999
ADAM_EPS = 1e-08
ADAM_WD = 0.01
ADAM_STEP = 10

F32 = jnp.float32
BF16 = jnp.bfloat16
MESH = pl.DeviceIdType.MESH

SUBLANES = 8
HALO = 32
CONV_ROWS = 64
VMEM_LIMIT = 52 * 1024 * 1024

_NT = (((1,), (1,)), ((), ()))
_TN = (((0,), (0,)), ((), ()))


def _params(*sem):
    return pltpu.CompilerParams(dimension_semantics=sem, vmem_limit_bytes=VMEM_LIMIT)


def _tile(n, pref):
    t = min(n, pref)
    assert n % t == 0, (n, t)
    return t


def _sigmoid(v):
    return 1.0 / (1.0 + jnp.exp(-v))


def _rows8(v):
    r, n = v.shape
    return jnp.sum(v.reshape(r // SUBLANES, SUBLANES, n), axis=0)


def _dot(a, b):
    return jnp.dot(a, b, preferred_element_type=F32)


def _dot_nt(a, b):
    return lax.dot_general(a, b, _NT, preferred_element_type=F32)


def _dot_tn(a, b):
    return lax.dot_general(a, b, _TN, preferred_element_type=F32)


def _rms_bwd(xv, g, dh):
    r = lax.rsqrt(jnp.mean(xv * xv, axis=-1, keepdims=True) + EPS)
    xhat = xv * r
    dxhat = dh * g
    dx = r * (dxhat - xhat * jnp.mean(dxhat * xhat, axis=-1, keepdims=True))
    return dx, dh * xhat


def _row_spec(tm, n, col=0):
    return pl.BlockSpec((tm, n), lambda i: (i, col))


def _full_spec(shape):
    return pl.BlockSpec(shape, lambda *_: (0,) * len(shape))


def _mesh_pos():
    return lax.axis_index("x"), lax.axis_index("y"), lax.axis_index("c")


def _dev_index(dev):
    return 4 * dev[0] + 2 * dev[1] + dev[2]


class _Exchange:
    middle_at = None

    def __init__(self, arrs):
        self.arrays = list(arrs)

    def out_shape(self):
        return [jax.ShapeDtypeStruct(a.shape, a.dtype) for a in self.arrays]

    def scratch(self):
        n = len(self.arrays)
        return [pltpu.SemaphoreType.DMA((7 * n,)), pltpu.SemaphoreType.DMA((7 * n,)), pltpu.SemaphoreType.DMA((n,))]

    def _copies(self, ins, outs, sems):
        send_sems, recv_sems, local_sems = sems
        x, y, c = _mesh_pos()
        me = _dev_index((x, y, c))
        mine, sends, arrivals = [], [], []
        for p in range(len(self.arrays)):
            mine.append(pltpu.make_async_copy(ins[p].at[me], outs[p].at[me], local_sems.at[p]))
            for k in range(1, N_DEV):
                peer = (1 - x if k & 4 else x, 1 - y if k & 2 else y, 1 - c if k & 1 else c)
                pid = _dev_index(peer)
                pair = dict(send_sem=send_sems.at[7 * p + k - 1], recv_sem=recv_sems.at[7 * p + k - 1],
                            device_id=peer, device_id_type=MESH)
                sends.append(pltpu.make_async_remote_copy(src_ref=ins[p].at[pid], dst_ref=outs[p].at[me], **pair))
                arrivals.append(pltpu.make_async_remote_copy(src_ref=ins[p].at[pid], dst_ref=outs[p].at[pid], **pair))
        return mine, sends, arrivals

    def start(self, ins, outs, sems):
        mine, sends, _ = self._copies(ins, outs, sems)
        for cp in mine + sends:
            cp.start()

    def finish(self, ins, outs, sems):
        mine, sends, arrivals = self._copies(ins, outs, sems)
        for cp in arrivals:
            cp.wait_recv()
        for cp in sends:
            cp.wait_send()
        for cp in mine:
            cp.wait()


class _Gather:
    middle_at = 0.75

    def __init__(self, arrs):
        self.arrays = list(arrs)

    def out_shape(self):
        return [jax.ShapeDtypeStruct((N_DEV,) + a.shape, a.dtype) for a in self.arrays]

    def scratch(self):
        n = len(self.arrays)
        return [pltpu.SemaphoreType.DMA((7 * n,)), pltpu.SemaphoreType.DMA((7 * n,)), pltpu.SemaphoreType.DMA((n,))]

    def _copies(self, ins, outs, sems):
        send_sems, recv_sems, local_sems = sems
        x, y, c = _mesh_pos()
        me, sibling = (x, y, c), (x, y, 1 - c)
        chips = [(1 - x, y), (x, 1 - y), (1 - x, 1 - y)]
        n = len(self.arrays)

        def copy(p, k, dev, to, src=None):
            block = outs[p].at[_dev_index(dev)]
            return pltpu.make_async_remote_copy(
                src_ref=block if src is None else src, dst_ref=block,
                send_sem=send_sems.at[7 * p + k], recv_sem=recv_sems.at[7 * p + k],
                device_id=to, device_id_type=MESH)

        cp = dict(mine=[pltpu.make_async_copy(ins[p], outs[p].at[_dev_index(me)], local_sems.at[p])
                        for p in range(n)])
        cp["first"] = [copy(p, 0, me, sibling, src=ins[p]) for p in range(n)]
        cp["first"] += [copy(p, 1 + j, me, (*chip, c), src=ins[p]) for p in range(n) for j, chip in enumerate(chips)]
        cp["over_ici"] = [copy(p, 1 + j, (*chip, c), me) for j, chip in enumerate(chips) for p in range(n)]
        cp["passed"] = [copy(p, 4 + j, (*chip, c), sibling) for j, chip in enumerate(chips) for p in range(n)]
        cp["from_sibling"] = [copy(p, 0, sibling, me) for p in range(n)]
        cp["from_sibling"] += [copy(p, 4 + j, (*chip, 1 - c), me) for j, chip in enumerate(chips) for p in range(n)]
        return cp

    def start(self, ins, outs, sems):
        cp = self._copies(ins, outs, sems)
        for d in cp["mine"] + cp["first"]:
            d.start()

    def middle(self, ins, outs, sems):
        cp = self._copies(ins, outs, sems)
        for arrived, onward in zip(cp["over_ici"], cp["passed"]):
            arrived.wait_recv()
            onward.start()

    def finish(self, ins, outs, sems):
        cp = self._copies(ins, outs, sems)
        for d in cp["from_sibling"]:
            d.wait_recv()
        for d in cp["first"] + cp["passed"]:
            d.wait_send()
        for d in cp["mine"]:
            d.wait()


def _run_comm(comm, name):
    n = len(comm.arrays)

    def body(*refs):
        ins, outs, sems = refs[:n], refs[n:2 * n], refs[2 * n:]
        comm.start(ins, outs, sems)
        if comm.middle_at is not None:
            comm.middle(ins, outs, sems)
        comm.finish(ins, outs, sems)

    any_spec = pl.BlockSpec(memory_space=pl.ANY)
    return pl.pallas_call(
        body, name=name, in_specs=[any_spec] * n, out_specs=[any_spec] * n, out_shape=comm.out_shape(),
        scratch_shapes=comm.scratch(),
    )(*comm.arrays)


def _pallas(body, *, name, grid, in_specs, out_specs, out_shape, args, sem, scratch_shapes=(), comm=None):
    if comm is None:
        outs = pl.pallas_call(
            body, name=name, grid=grid, in_specs=in_specs, out_specs=out_specs, out_shape=out_shape,
            scratch_shapes=list(scratch_shapes), compiler_params=_params(*sem),
        )(*args)
        return outs, None
    n_in, n_out, n_scr, n_c = len(in_specs), len(out_specs), len(scratch_shapes), len(comm.arrays)
    steps = grid[0]
    middle = None if comm.middle_at is None else min(steps - 1, int(steps * comm.middle_at))

    def carried(*refs):
        ins, refs = refs[:n_in], refs[n_in:]
        cins, refs = refs[:n_c], refs[n_c:]
        outs, refs = refs[:n_out], refs[n_out:]
        couts, refs = refs[:n_c], refs[n_c:]
        scr, csems = refs[:n_scr], refs[n_scr:]
        step = pl.program_id(0)

        @pl.when(step == 0)
        def _():
            comm.start(cins, couts, csems)

        body(*ins, *outs, *scr)

        if middle is not None:
            @pl.when(step == middle)
            def _():
                comm.middle(cins, couts, csems)

        @pl.when(step == steps - 1)
        def _():
            comm.finish(cins, couts, csems)

    any_spec = pl.BlockSpec(memory_space=pl.ANY)
    res = pl.pallas_call(
        carried, name=name, grid=grid,
        in_specs=list(in_specs) + [any_spec] * n_c, out_specs=list(out_specs) + [any_spec] * n_c,
        out_shape=list(out_shape) + comm.out_shape(),
        scratch_shapes=list(scratch_shapes) + comm.scratch(),
        compiler_params=_params(*(("arbitrary",) + tuple(sem[1:]))),
    )(*args, *comm.arrays)
    return res[:n_out], res[n_out:]


def _inproj_fwd(x, g, w, b):
    T, D = x.shape
    rest_w = IN_W - QKV_W
    tm = _tile(T, 512)

    def body(x_ref, g_ref, w_ref, b_ref, h_ref, qkv_ref, rest_ref):
        xv = x_ref[...]
        r = lax.rsqrt(jnp.mean(xv * xv, axis=-1, keepdims=True) + EPS)
        h = (xv * r * g_ref[...]).astype(BF16)
        h_ref[...] = h
        qkv_ref[...] = (_dot(h, w_ref[:, :QKV_W]) + b_ref[:, :QKV_W]).astype(BF16)
        rest_ref[...] = _dot(h, w_ref[:, QKV_W:]) + b_ref[:, QKV_W:]

    return pl.pallas_call(
        body, name="inproj_fwd", grid=(T // tm,),
        in_specs=[_row_spec(tm, D), _full_spec((1, D)), _full_spec((D, IN_W)), _full_spec((1, IN_W))],
        out_specs=[_row_spec(tm, D), _row_spec(tm, QKV_W), _row_spec(tm, rest_w)],
        out_shape=[jax.ShapeDtypeStruct((T, D), BF16), jax.ShapeDtypeStruct((T, QKV_W), BF16),
                   jax.ShapeDtypeStruct((T, rest_w), F32)],
        compiler_params=_params("parallel"),
    )(x, g, w, b)


def _fold_masks(first):
    row = lax.broadcasted_iota(jnp.int32, (BLOCK, BLOCK), 0)
    col = lax.broadcasted_iota(jnp.int32, (BLOCK, BLOCK), 1)
    upper = col > row
    dist = jnp.where(upper, row + BLOCK - col, row - col)
    keep = col <= row + jnp.where(first, 0, BLOCK)
    return upper, dist.astype(F32), keep


def _fold(band, upper):
    return jnp.where(upper, band[:, :BLOCK], band[:, BLOCK:])


def _unfold(folded, upper):
    return jnp.concatenate([jnp.where(upper, folded, 0.0), jnp.where(upper, 0.0, folded)], axis=1)


def _fill_kv(kv_buf, cur_ref, prev_ref):
    scale = 1.0 / math.sqrt(HEAD_DIM)
    assert math.frexp(scale)[0] == 0.5
    for r0, ref in ((0, prev_ref), (BLOCK, cur_ref)):
        rows = ref.shape[0]
        kv_buf[r0:r0 + rows, :KV_W] = ref[:, ATTN_W:ATTN_W + KV_W] * scale
        kv_buf[r0:r0 + rows, KV_W:] = ref[:, ATTN_W + KV_W:]


def _group_rows(x, kh):
    return jnp.concatenate([x[:, h * HEAD_DIM:(h + 1) * HEAD_DIM] for h in range(kh * GROUP, (kh + 1) * GROUP)],
                           axis=0)


def _attn_probs(scores, h, sink, upper, distf, keep):
    s = _fold(scores, upper) - (2.0 ** (-8.0 * (h + 1) / N_Q)) * distf
    if keep is not None:
        s = jnp.where(keep, s, NEG)
    m = jnp.maximum(jnp.max(s, axis=-1, keepdims=True), sink)
    p = jnp.exp(s - m)
    e = jnp.exp(sink - m)
    inv = 1.0 / (jnp.sum(p, axis=-1, keepdims=True) + e)
    return p * inv, e * inv


def _attn_fwd(qkv, sinks):
    T = qkv.shape[0]
    tq = _tile(T, 512)
    nblk = tq // BLOCK

    def body(sink_ref, cur_ref, prev_ref, o_ref, kv_buf):
        _fill_kv(kv_buf, cur_ref, prev_ref)
        upper, distf, keep = _fold_masks(pl.program_id(0) == 0)
        units = [(j, kh) for j in range(nblk) for kh in range(N_KV)]

        def key_band(j, kh):
            return kv_buf[j * BLOCK:(j + 2) * BLOCK, kh * HEAD_DIM:(kh + 1) * HEAD_DIM]

        def value_band(j, kh):
            return kv_buf[j * BLOCK:(j + 2) * BLOCK, KV_W + kh * HEAD_DIM:KV_W + (kh + 1) * HEAD_DIM]

        scores = {(j, kh): _dot_nt(_group_rows(cur_ref[j * BLOCK:(j + 1) * BLOCK, :ATTN_W], kh), key_band(j, kh))
                  for j, kh in units}
        probs = {}
        for j, kh in units:
            ps = []
            for g in range(GROUP):
                h = kh * GROUP + g
                p, _ = _attn_probs(scores[j, kh][g * BLOCK:(g + 1) * BLOCK, :], h, sink_ref[h], upper, distf,
                                   keep if j == 0 else None)
                ps.append(_unfold(p, upper).astype(BF16))
            probs[j, kh] = jnp.concatenate(ps, axis=0)
        outs = {u: _dot(probs[u], value_band(*u)) for u in units}
        for j in range(nblk):
            heads = [outs[j, kh][g * BLOCK:(g + 1) * BLOCK, :] for kh in range(N_KV) for g in range(GROUP)]
            o_ref[j * BLOCK:(j + 1) * BLOCK, :] = jnp.concatenate(heads, axis=1).astype(BF16)

    return pl.pallas_call(
        body, name="attn_fwd", grid=(T // tq,),
        in_specs=[pl.BlockSpec(memory_space=pltpu.SMEM),
                  _row_spec(tq, QKV_W),
                  pl.BlockSpec((BLOCK, QKV_W), lambda i: (jnp.maximum(i * nblk - 1, 0), 0))],
        out_specs=_row_spec(tq, ATTN_W),
        out_shape=jax.ShapeDtypeStruct((T, ATTN_W), BF16),
        scratch_shapes=[pltpu.VMEM((tq + BLOCK, 2 * KV_W), BF16)],
        compiler_params=_params("parallel"),
    )(sinks, qkv, qkv)


def _shifted_copies(buf):
    n = buf.shape[1] - SUBLANES
    for s in range(1, SUBLANES):
        buf[s, 0:n, :] = buf[0, s:s + n, :]


def _shifted_rows(buf, start, rows):
    s = start % SUBLANES
    return buf[s, start - s:start - s + rows, :]


def _conv_fwd(rest, cw, cb, lg, lb):
    T = rest.shape[0]
    C = CONV_C
    tm = _tile(T, 256)
    R = _tile(tm, CONV_ROWS)
    per = tm // HALO

    def body(cur_ref, prev_ref, w_ref, cb_ref, g_ref, b_ref, u0_ref, yc_ref, u_ref, ubuf):
        i = pl.program_id(0)
        up = prev_ref[:, :C] * _sigmoid(prev_ref[:, C:])
        ubuf[0, 0:HALO, :] = jnp.where(i > 0, up, 0.0)
        u0 = cur_ref[:, :C] * _sigmoid(cur_ref[:, C:])
        ubuf[0, HALO:, :] = u0
        u0_ref[...] = u0
        _shifted_copies(ubuf)
        off = HALO - (CONV_K - 1)
        for c in range(tm // R):
            acc = jnp.broadcast_to(cb_ref[...], (R, C))
            for j in range(CONV_K):
                acc = acc + w_ref[j:j + 1, :] * _shifted_rows(ubuf, c * R + off + j, R)
            yc_ref[c * R:(c + 1) * R, :] = acc
            xc = acc - jnp.mean(acc, axis=-1, keepdims=True)
            ln = xc * lax.rsqrt(jnp.mean(xc * xc, axis=-1, keepdims=True) + EPS) * g_ref[...] + b_ref[...]
            u_ref[c * R:(c + 1) * R, :] = (ln * _sigmoid(ln)).astype(BF16)

    return pl.pallas_call(
        body, name="conv_fwd", grid=(T // tm,),
        in_specs=[_row_spec(tm, 2 * C),
                  pl.BlockSpec((HALO, 2 * C), lambda i: (jnp.maximum(i * per - 1, 0), 0)),
                  _full_spec((CONV_K, C)), _full_spec((1, C)), _full_spec((1, C)), _full_spec((1, C))],
        out_specs=[_row_spec(tm, C), _row_spec(tm, C), _row_spec(tm, C)],
        out_shape=[jax.ShapeDtypeStruct((T, C), F32), jax.ShapeDtypeStruct((T, C), F32),
                   jax.ShapeDtypeStruct((T, C), BF16)],
        scratch_shapes=[pltpu.VMEM((SUBLANES, tm + HALO, C), F32)],
        compiler_params=_params("parallel"),
    )(rest, rest, cw, cb, lg, lb)


def _merge_fwd(attn, u, rest, x, wa, wc, bc, wo):
    T, D = x.shape
    tm = _tile(T, 512)
    gcol = 2 * CONV_C // D

    def body(attn_ref, u_ref, ga_ref, gc_ref, x_ref, wa_ref, wc_ref, bc_ref, wo_ref, m_ref, x1_ref):
        bra = _dot(attn_ref[...], wa_ref[...])
        brc = _dot(u_ref[...], wc_ref[...]) + bc_ref[...]
        mb = (_sigmoid(ga_ref[...]) * bra + _sigmoid(gc_ref[...]) * brc).astype(BF16)
        m_ref[...] = mb
        x1_ref[...] = x_ref[...] + _dot(mb, wo_ref[...])

    return pl.pallas_call(
        body, name="merge_fwd", grid=(T // tm,),
        in_specs=[_row_spec(tm, ATTN_W), _row_spec(tm, CONV_C), _row_spec(tm, D, gcol), _row_spec(tm, D, gcol + 1),
                  _row_spec(tm, D), _full_spec((ATTN_W, D)), _full_spec((CONV_C, D)), _full_spec((1, D)),
                  _full_spec((D, D))],
        out_specs=[_row_spec(tm, D), _row_spec(tm, D)],
        out_shape=[jax.ShapeDtypeStruct((T, D), BF16), jax.ShapeDtypeStruct((T, D), F32)],
        compiler_params=_params("parallel"),
    )(attn, u, rest, rest, x, wa, wc, bc, wo)


def _mlp_fwd(x1, g, w1, w2, comm=None):
    T, D = x1.shape
    tm = _tile(T, 256)

    def body(x_ref, g_ref, w1_ref, w2_ref, h_ref, z_ref, o_ref):
        xv = x_ref[...]
        r = lax.rsqrt(jnp.mean(xv * xv, axis=-1, keepdims=True) + EPS)
        h = (xv * r * g_ref[...]).astype(BF16)
        h_ref[...] = h
        z = _dot(h, w1_ref[...])
        z_ref[...] = z.astype(BF16)
        a = jnp.square(jnp.maximum(z, 0.0)).astype(BF16)
        o_ref[...] = xv + _dot(a, w2_ref[...])

    return _pallas(
        body, name="mlp_fwd", grid=(T // tm,),
        in_specs=[_row_spec(tm, D), _full_spec((1, D)), _full_spec((D, D_FF)), _full_spec((D_FF, D))],
        out_specs=[_row_spec(tm, D), _row_spec(tm, D_FF), _row_spec(tm, D)],
        out_shape=[jax.ShapeDtypeStruct((T, D), BF16), jax.ShapeDtypeStruct((T, D_FF), BF16),
                   jax.ShapeDtypeStruct((T, D), F32)],
        args=(x1, g, w1, w2), sem=("parallel",), comm=comm)


def _final_loss(x, g, target):
    T, D = x.shape
    tm = _tile(T, 512)

    def body(x_ref, g_ref, t_ref, l_ref, dx_ref, dg_ref):
        @pl.when(pl.program_id(0) == 0)
        def _():
            l_ref[...] = jnp.zeros_like(l_ref)
            dg_ref[...] = jnp.zeros_like(dg_ref)

        xv = x_ref[...]
        r = lax.rsqrt(jnp.mean(xv * xv, axis=-1, keepdims=True) + EPS)
        e = xv * r * g_ref[...] - t_ref[...]
        l_ref[...] += _rows8(e * e) * (0.5 / D)
        dx, dg = _rms_bwd(xv, g_ref[...], e * (1.0 / D))
        dx_ref[...] = dx
        dg_ref[...] += _rows8(dg)

    return pl.pallas_call(
        body, name="final_loss", grid=(T // tm,),
        in_specs=[_row_spec(tm, D), _full_spec((1, D)), _row_spec(tm, D)],
        out_specs=[_full_spec((SUBLANES, D)), _row_spec(tm, D), _full_spec((SUBLANES, D))],
        out_shape=[jax.ShapeDtypeStruct((SUBLANES, D), F32), jax.ShapeDtypeStruct((T, D), F32),
                   jax.ShapeDtypeStruct((SUBLANES, D), F32)],
        compiler_params=_params("arbitrary"),
    )(x, g, target)


def _mlp_bwd(dx2, x1, z, g, w1, w2, comm=None):
    T, D = x1.shape
    tm = _tile(T, 256)

    def body(dx2_ref, x_ref, z_ref, g_ref, w1_ref, w2_ref, dx1_ref, dz_ref, dg_ref):
        @pl.when(pl.program_id(0) == 0)
        def _():
            dg_ref[...] = jnp.zeros_like(dg_ref)

        dxo = dx2_ref[...]
        da = _dot_nt(dxo.astype(BF16), w2_ref[...])
        dz = (da * (2.0 * jnp.maximum(z_ref[...].astype(F32), 0.0))).astype(BF16)
        dz_ref[...] = dz
        dh = _dot_nt(dz, w1_ref[...])
        dx, dg = _rms_bwd(x_ref[...], g_ref[...], dh)
        dx1_ref[...] = dxo + dx
        dg_ref[...] += _rows8(dg)

    return _pallas(
        body, name="mlp_bwd", grid=(T // tm,),
        in_specs=[_row_spec(tm, D), _row_spec(tm, D), _row_spec(tm, D_FF), _full_spec((1, D)),
                  _full_spec((D, D_FF)), _full_spec((D_FF, D))],
        out_specs=[_row_spec(tm, D), _row_spec(tm, D_FF), _full_spec((SUBLANES, D))],
        out_shape=[jax.ShapeDtypeStruct((T, D), F32), jax.ShapeDtypeStruct((T, D_FF), BF16),
                   jax.ShapeDtypeStruct((SUBLANES, D), F32)],
        args=(dx2, x1, z, g, w1, w2), sem=("arbitrary",), comm=comm)


def _tn_blocks(a, b, name, col_sharded, relu_sq=False):
    T, M = a.shape
    N = b.shape[1]
    tk = _tile(T, 1024)
    tm = _tile(M, 512 if col_sharded else 1024)
    nb = N // N_DEV
    last = T // tk - 1

    def body(a_ref, b_ref, o_ref, acc_ref):
        k = pl.program_id(1)

        @pl.when(k == 0)
        def _():
            acc_ref[...] = jnp.zeros_like(acc_ref)

        av = a_ref[...]
        if relu_sq:
            av = jnp.square(jnp.maximum(av.astype(F32), 0.0))
        acc_ref[...] += _dot_tn(av.astype(BF16), b_ref[...].astype(BF16))

        @pl.when(k == last)
        def _():
            if col_sharded:
                for d in range(N_DEV):
                    o_ref[d] = acc_ref[:, d * nb:(d + 1) * nb].astype(BF16)
            else:
                o_ref[...] = acc_ref[...].astype(BF16)

    if col_sharded:
        out_spec = pl.BlockSpec((N_DEV, tm, nb), lambda i, k: (0, i, 0))
        out_shape = jax.ShapeDtypeStruct((N_DEV, M, nb), BF16)
    else:
        out_spec = pl.BlockSpec((tm, N), lambda i, k: (i, 0))
        out_shape = jax.ShapeDtypeStruct((M, N), BF16)
    out = pl.pallas_call(
        body, name=name, grid=(M // tm, T // tk),
        in_specs=[pl.BlockSpec((tk, tm), lambda i, k: (k, i)), pl.BlockSpec((tk, N), lambda i, k: (k, 0))],
        out_specs=out_spec, out_shape=out_shape,
        scratch_shapes=[pltpu.VMEM((tm, N), F32)],
        compiler_params=_params("parallel", "arbitrary"),
    )(a, b)
    return out if col_sharded else out.reshape(N_DEV, M // N_DEV, N)


def _merge_bwd(dx1, attn, u, rest, wa, wc, bc, wo):
    T, D = dx1.shape
    tm = _tile(T, 256)
    gcol = 2 * CONV_C // D

    def body(dx_ref, attn_ref, u_ref, ga_ref, gc_ref, wa_ref, wc_ref, bc_ref, wo_ref,
             dattn_ref, du_ref, dga_ref, dgc_ref, dbra_ref, dbrc_ref, dbc_ref):
        @pl.when(pl.program_id(0) == 0)
        def _():
            dbc_ref[...] = jnp.zeros_like(dbc_ref)

        dm = _dot_nt(dx_ref[...].astype(BF16), wo_ref[...])
        bra = _dot(attn_ref[...], wa_ref[...])
        brc = _dot(u_ref[...], wc_ref[...]) + bc_ref[...]
        sa = _sigmoid(ga_ref[...])
        sc = _sigmoid(gc_ref[...])
        dbra = dm * sa
        dbrc = dm * sc
        dga_ref[...] = (dm * bra * sa * (1.0 - sa)).astype(BF16)
        dgc_ref[...] = (dm * brc * sc * (1.0 - sc)).astype(BF16)
        dbra_b = dbra.astype(BF16)
        dbrc_b = dbrc.astype(BF16)
        dbra_ref[...] = dbra_b
        dbrc_ref[...] = dbrc_b
        dbc_ref[...] += _rows8(dbrc)
        dattn_ref[...] = _dot_nt(dbra_b, wa_ref[...]).astype(BF16)
        du_ref[...] = _dot_nt(dbrc_b, wc_ref[...])

    return pl.pallas_call(
        body, name="merge_bwd", grid=(T // tm,),
        in_specs=[_row_spec(tm, D), _row_spec(tm, ATTN_W), _row_spec(tm, CONV_C), _row_spec(tm, D, gcol),
                  _row_spec(tm, D, gcol + 1), _full_spec((ATTN_W, D)), _full_spec((CONV_C, D)), _full_spec((1, D)),
                  _full_spec((D, D))],
        out_specs=[_row_spec(tm, ATTN_W), _row_spec(tm, CONV_C), _row_spec(tm, D), _row_spec(tm, D),
                   _row_spec(tm, D), _row_spec(tm, D), _full_spec((SUBLANES, D))],
        out_shape=[jax.ShapeDtypeStruct((T, ATTN_W), BF16), jax.ShapeDtypeStruct((T, CONV_C), F32),
                   jax.ShapeDtypeStruct((T, D), BF16), jax.ShapeDtypeStruct((T, D), BF16),
                   jax.ShapeDtypeStruct((T, D), BF16), jax.ShapeDtypeStruct((T, D), BF16),
                   jax.ShapeDtypeStruct((SUBLANES, D), F32)],
        compiler_params=_params("arbitrary"),
    )(dx1, attn, u, rest, rest, wa, wc, bc, wo)


def _conv_bwd_norm(du, yc, lg, lb):
    T, C = yc.shape
    tm = _tile(T, 512)

    def body(du_ref, yc_ref, g_ref, b_ref, dyc_ref, dg_ref, db_ref):
        @pl.when(pl.program_id(0) == 0)
        def _():
            dg_ref[...] = jnp.zeros_like(dg_ref)
            db_ref[...] = jnp.zeros_like(db_ref)

        yv = yc_ref[...]
        xc = yv - jnp.mean(yv, axis=-1, keepdims=True)
        rstd = lax.rsqrt(jnp.mean(xc * xc, axis=-1, keepdims=True) + EPS)
        xn = xc * rstd
        ln = xn * g_ref[...] + b_ref[...]
        sg = _sigmoid(ln)
        dln = du_ref[...] * sg * (1.0 + ln * (1.0 - sg))
        dg_ref[...] += _rows8(dln * xn)
        db_ref[...] += _rows8(dln)
        dxn = dln * g_ref[...]
        dyc_ref[...] = rstd * (dxn - jnp.mean(dxn, axis=-1, keepdims=True)
                               - xn * jnp.mean(dxn * xn, axis=-1, keepdims=True))

    return pl.pallas_call(
        body, name="conv_bwd_norm", grid=(T // tm,),
        in_specs=[_row_spec(tm, C), _row_spec(tm, C), _full_spec((1, C)), _full_spec((1, C))],
        out_specs=[_row_spec(tm, C), _full_spec((SUBLANES, C)), _full_spec((SUBLANES, C))],
        out_shape=[jax.ShapeDtypeStruct((T, C), F32), jax.ShapeDtypeStruct((SUBLANES, C), F32),
                   jax.ShapeDtypeStruct((SUBLANES, C), F32)],
        compiler_params=_params("arbitrary"),
    )(du, yc, lg, lb)


def _conv_bwd_taps(dyc, u0, rest, cw, comm=None):
    T, C = dyc.shape
    tm = _tile(T, 256)
    R = _tile(tm, CONV_ROWS)
    per = tm // HALO
    nt = T // tm

    def body(dy_ref, dyn_ref, u0_ref, u0p_ref, glu_ref, w_ref, dga_ref, dgb_ref, dw_ref, db_ref, dbuf, ubuf):
        i = pl.program_id(0)

        @pl.when(i == 0)
        def _():
            dw_ref[...] = jnp.zeros_like(dw_ref)
            db_ref[...] = jnp.zeros_like(db_ref)

        dbuf[0, 0:tm, :] = dy_ref[...]
        dbuf[0, tm:, :] = jnp.where(i < nt - 1, dyn_ref[...], 0.0)
        ubuf[0, 0:HALO, :] = jnp.where(i > 0, u0p_ref[...], 0.0)
        ubuf[0, HALO:, :] = u0_ref[...]
        _shifted_copies(dbuf)
        _shifted_copies(ubuf)
        off = HALO - (CONV_K - 1)
        for c in range(tm // R):
            dy = dbuf[0, c * R:(c + 1) * R, :]
            acc = jnp.zeros((R, C), F32)
            for j in range(CONV_K):
                acc = acc + w_ref[j:j + 1, :] * _shifted_rows(dbuf, c * R + CONV_K - 1 - j, R)
                dw_ref[j * SUBLANES:(j + 1) * SUBLANES, :] += _rows8(dy * _shifted_rows(ubuf, c * R + off + j, R))
            db_ref[...] += _rows8(dy)
            a = glu_ref[c * R:(c + 1) * R, :C]
            sb = _sigmoid(glu_ref[c * R:(c + 1) * R, C:])
            dga_ref[c * R:(c + 1) * R, :] = (acc * sb).astype(BF16)
            dgb_ref[c * R:(c + 1) * R, :] = (acc * a * sb * (1.0 - sb)).astype(BF16)

    return _pallas(
        body, name="conv_bwd_taps", grid=(nt,),
        in_specs=[_row_spec(tm, C),
                  pl.BlockSpec((HALO, C), lambda i: (jnp.minimum((i + 1) * per, T // HALO - 1), 0)),
                  _row_spec(tm, C),
                  pl.BlockSpec((HALO, C), lambda i: (jnp.maximum(i * per - 1, 0), 0)),
                  _row_spec(tm, 2 * C), _full_spec((CONV_K, C))],
        out_specs=[_row_spec(tm, C), _row_spec(tm, C), _full_spec((CONV_K * SUBLANES, C)),
                   _full_spec((SUBLANES, C))],
        out_shape=[jax.ShapeDtypeStruct((T, C), BF16), jax.ShapeDtypeStruct((T, C), BF16),
                   jax.ShapeDtypeStruct((CONV_K * SUBLANES, C), F32), jax.ShapeDtypeStruct((SUBLANES, C), F32)],
        scratch_shapes=[pltpu.VMEM((SUBLANES, tm + HALO, C), F32), pltpu.VMEM((SUBLANES, tm + HALO, C), F32)],
        args=(dyc, dyc, u0, u0, rest, cw), sem=("arbitrary",), comm=comm)


def _attn_bwd(qkv, dattn, sinks, comm=None):
    T = qkv.shape[0]
    tq = _tile(T, 512)
    nblk = tq // BLOCK
    scale = 1.0 / math.sqrt(HEAD_DIM)

    def body(sink_ref, cur_ref, prev_ref, do_ref, dq_ref, hi_ref, lo_ref, ds_ref, kv_buf):
        i = pl.program_id(0)

        @pl.when(i == 0)
        def _():
            ds_ref[...] = jnp.zeros_like(ds_ref)

        _fill_kv(kv_buf, cur_ref, prev_ref)
        upper, distf, keep = _fold_masks(i == 0)
        for j in range(nblk):
            rows = slice(j * BLOCK, (j + 1) * BLOCK)
            band = kv_buf[j * BLOCK:(j + 2) * BLOCK, :]
            q = cur_ref[rows, :ATTN_W]
            do = do_ref[rows, :]
            dqs, dks, dvs = [], [], []
            for kh in range(N_KV):
                k = band[:, kh * HEAD_DIM:(kh + 1) * HEAD_DIM]
                v = band[:, KV_W + kh * HEAD_DIM:KV_W + (kh + 1) * HEAD_DIM]
                qg = _group_rows(q, kh)
                dog = _group_rows(do, kh)
                scores = _dot_nt(qg, k)
                dps = _dot_nt(dog, v)
                ps, dss = [], []
                for g in range(GROUP):
                    h = kh * GROUP + g
                    head = slice(g * BLOCK, (g + 1) * BLOCK)
                    p, psink = _attn_probs(scores[head, :], h, sink_ref[h], upper, distf, keep if j == 0 else None)
                    pdp = p * _fold(dps[head, :], upper)
                    delta = jnp.sum(pdp, axis=-1, keepdims=True)
                    ds_ref[h * SUBLANES:(h + 1) * SUBLANES, :] += _rows8(psink * pdp)
                    dss.append(_unfold(pdp - p * delta, upper).astype(BF16))
                    ps.append(_unfold(p, upper).astype(BF16))
                dsg = jnp.concatenate(dss, axis=0)
                dqg = _dot(dsg, k)
                dqs += [dqg[g * BLOCK:(g + 1) * BLOCK, :] for g in range(GROUP)]
                dks.append(_dot_tn(dsg, qg) * scale)
                dvs.append(_dot_tn(jnp.concatenate(ps, axis=0), dog))
            dq_ref[rows, :] = jnp.concatenate(dqs, axis=1).astype(BF16)
            dkv = jnp.concatenate(dks + dvs, axis=1)
            lo_ref[rows, :] = dkv[:BLOCK, :]
            hi_ref[rows, :] = dkv[BLOCK:, :]

    return _pallas(
        body, name="attn_bwd", grid=(T // tq,),
        in_specs=[pl.BlockSpec(memory_space=pltpu.SMEM),
                  _row_spec(tq, QKV_W),
                  pl.BlockSpec((BLOCK, QKV_W), lambda i: (jnp.maximum(i * nblk - 1, 0), 0)),
                  _row_spec(tq, ATTN_W)],
        out_specs=[_row_spec(tq, ATTN_W), _row_spec(tq, 2 * KV_W), _row_spec(tq, 2 * KV_W),
                   _full_spec((N_Q * SUBLANES, BLOCK))],
        out_shape=[jax.ShapeDtypeStruct((T, ATTN_W), BF16), jax.ShapeDtypeStruct((T, 2 * KV_W), F32),
                   jax.ShapeDtypeStruct((T, 2 * KV_W), F32), jax.ShapeDtypeStruct((N_Q * SUBLANES, BLOCK), F32)],
        scratch_shapes=[pltpu.VMEM((tq + BLOCK, 2 * KV_W), BF16)],
        args=(sinks, qkv, qkv, dattn), sem=("arbitrary",), comm=comm)


def _inproj_bwd(dq, hi, lo, dglu_a, dglu_b, dga, dgc, x, g, w, dx1):
    T, D = x.shape
    C = CONV_C
    tm = _tile(T, 256)
    per = tm // BLOCK
    nt = T // tm
    kv2 = 2 * KV_W

    def body(dq_ref, hi_ref, lo_ref, lon_ref, da_ref, db_ref, dga_ref, dgc_ref, x_ref, g_ref, w_ref, dx1_ref,
             dp_ref, dx_ref, dg_ref, dbias_ref):
        i = pl.program_id(0)

        @pl.when(i == 0)
        def _():
            dg_ref[...] = jnp.zeros_like(dg_ref)
            dbias_ref[...] = jnp.zeros_like(dbias_ref)

        dp_ref[:, :ATTN_W] = dq_ref[...]
        lo_next = jnp.where(i < nt - 1, lon_ref[...], 0.0)
        if tm > BLOCK:
            lo_shift = jnp.concatenate([lo_ref[BLOCK:, :], lo_next], axis=0)
        else:
            lo_shift = lo_next
        dp_ref[:, ATTN_W:QKV_W] = (hi_ref[...] + lo_shift).astype(BF16)
        dp_ref[:, QKV_W:QKV_W + C] = da_ref[...]
        dp_ref[:, QKV_W + C:QKV_W + 2 * C] = db_ref[...]
        dp_ref[:, QKV_W + 2 * C:QKV_W + 2 * C + D] = dga_ref[...]
        dp_ref[:, QKV_W + 2 * C + D:] = dgc_ref[...]
        dp = dp_ref[...]
        dbias_ref[...] += _rows8(dp.astype(F32))
        dh = _dot_nt(dp, w_ref[...])
        dx, dg = _rms_bwd(x_ref[...], g_ref[...], dh)
        dx_ref[...] = dx1_ref[...] + dx
        dg_ref[...] += _rows8(dg)

    return pl.pallas_call(
        body, name="inproj_bwd", grid=(nt,),
        in_specs=[_row_spec(tm, ATTN_W), _row_spec(tm, kv2), _row_spec(tm, kv2),
                  pl.BlockSpec((BLOCK, kv2), lambda i: (jnp.minimum((i + 1) * per, T // BLOCK - 1), 0)),
                  _row_spec(tm, C), _row_spec(tm, C), _row_spec(tm, D), _row_spec(tm, D),
                  _row_spec(tm, D), _full_spec((1, D)), _full_spec((D, IN_W)), _row_spec(tm, D)],
        out_specs=[_row_spec(tm, IN_W), _row_spec(tm, D), _full_spec((SUBLANES, D)), _full_spec((SUBLANES, IN_W))],
        out_shape=[jax.ShapeDtypeStruct((T, IN_W), BF16), jax.ShapeDtypeStruct((T, D), F32),
                   jax.ShapeDtypeStruct((SUBLANES, D), F32), jax.ShapeDtypeStruct((SUBLANES, IN_W), F32)],
        compiler_params=_params("arbitrary"),
    )(dq, hi, lo, lo, dglu_a, dglu_b, dga, dgc, x, g, w, dx1)


def _adamw_math(g, w, m, v):
    c1 = 1.0 / (1.0 - ADAM_B1 ** ADAM_STEP)
    c2 = 1.0 / (1.0 - ADAM_B2 ** ADAM_STEP)
    mn = ADAM_B1 * m + (1.0 - ADAM_B1) * g
    vn = ADAM_B2 * v + (1.0 - ADAM_B2) * (g * g)
    return -ADAM_LR * ((mn * c1) / (jnp.sqrt(vn * c2) + ADAM_EPS) + ADAM_WD * w), mn, vn


def _adamw_sharded(parts, w, m, v, name):
    depth, a, b = w.shape
    tr = _tile(a, 256) if a % SUBLANES == 0 else a
    nr = a // tr

    def body(*refs):
        p_refs, (w_ref, m_ref, v_ref, g_ref, d_ref, mo_ref, vo_ref) = refs[:depth], refs[depth:]
        layer = pl.program_id(0)
        for l in range(depth):
            @pl.when(layer == l)
            def _(l=l):
                g = p_refs[l][0].astype(F32)
                for s in range(1, N_DEV):
                    g = g + p_refs[l][s].astype(F32)
                g_ref[...] = g
                d_ref[...], mo_ref[...], vo_ref[...] = _adamw_math(g, w_ref[...], m_ref[...], v_ref[...])

    def part_spec(l):
        return pl.BlockSpec((N_DEV, tr, b),
                            lambda k, i: (0, jnp.where(k == l, i, jnp.where(k < l, 0, nr - 1)), 0))

    spec = pl.BlockSpec((None, tr, b), lambda k, i: (k, i, 0))
    out = jax.ShapeDtypeStruct((depth, a, b), F32)
    return pl.pallas_call(
        body, name=name, grid=(depth, nr),
        in_specs=[part_spec(l) for l in range(depth)] + [spec] * 3,
        out_specs=[spec] * 4, out_shape=[out] * 4,
        compiler_params=_params("arbitrary", "arbitrary"),
    )(*parts, w, m, v)


def _adamw_small(parts, w, m, v):
    R, N = w.shape

    def body(p_ref, w_ref, m_ref, v_ref, g_ref, d_ref, mo_ref, vo_ref):
        g = p_ref[0]
        for s in range(1, N_DEV):
            g = g + p_ref[s]
        g_ref[...] = g
        d_ref[...], mo_ref[...], vo_ref[...] = _adamw_math(g, w_ref[...], m_ref[...], v_ref[...])

    out = jax.ShapeDtypeStruct((R, N), F32)
    return pl.pallas_call(
        body, name="adamw_small", grid=(1,),
        in_specs=[_full_spec((N_DEV, R, N))] + [_full_spec((R, N))] * 3,
        out_specs=[_full_spec((R, N))] * 4, out_shape=[out] * 4,
        compiler_params=_params("arbitrary"),
    )(parts, w, m, v)


_SHARDED = ("w_in", "conv_w", "w_attn_proj", "w_conv_proj", "w_out", "w_mlp1", "w_mlp2")
_ROW_SHARDED = ("w_out", "w_mlp2")
_SMALL = ("mix_norm_g", "b_in", "sinks", "conv_b", "conv_ln_g", "conv_ln_b", "b_conv_proj", "mlp_norm_g",
          "final_norm_g")
_ORDER = ("mix_norm_g", "w_in", "b_in", "sinks", "conv_w", "conv_b", "conv_ln_g", "conv_ln_b", "w_attn_proj",
          "w_conv_proj", "b_conv_proj", "w_out", "mlp_norm_g", "w_mlp1", "w_mlp2", "final_norm_g")
_PACK = 1024


def _full_weight(name, gathered):
    _, a, b = gathered.shape
    if name in _ROW_SHARDED:
        return gathered.reshape(N_DEV * a, b)
    return gathered.transpose(1, 0, 2).reshape(a, N_DEV * b)


def _pack(arrs):
    flat = []
    for a in arrs:
        a = a.reshape(-1)
        flat.append(jnp.pad(a, (0, -a.size % _PACK)))
    return jnp.concatenate(flat).reshape(-1, BLOCK)


def _unpack(packed, shapes):
    flat = packed.reshape(-1)
    out, off = [], 0
    for s in shapes:
        n = math.prod(s)
        out.append(flat[off:off + n].reshape(s))
        off += n + (-n % _PACK)
    return out


def _layer_fwd(x, lw, comm=None):
    h, qkv, rest = _inproj_fwd(x, lw["mix_norm_g"], lw["w_in"], lw["b_in"])
    attn = _attn_fwd(qkv, lw["sinks"])
    u0, yc, u = _conv_fwd(rest, lw["conv_w"], lw["conv_b"], lw["conv_ln_g"], lw["conv_ln_b"])
    merged, x1 = _merge_fwd(attn, u, rest, x, lw["w_attn_proj"], lw["w_conv_proj"], lw["b_conv_proj"], lw["w_out"])
    (h2, z, x2), gathered = _mlp_fwd(x1, lw["mlp_norm_g"], lw["w_mlp1"], lw["w_mlp2"], comm)
    saved = dict(x=x, h=h, qkv=qkv, rest=rest, attn=attn, u0=u0, yc=yc, u=u, merged=merged, x1=x1, h2=h2, z=z)
    return x2, saved, gathered


_EARLY = ("w_mlp1", "w_mlp2")
_MIDDLE = ("w_out", "w_attn_proj", "w_conv_proj")
_LATE = ("w_in", "conv_w")


def _layer_bwd(dx2, lw, s, late_blocks=None):
    g, recv = {}, {}
    late = None if late_blocks is None else _Exchange(late_blocks)
    (dx1, dz, dg2), late_recv = _mlp_bwd(dx2, s["x1"], s["z"], lw["mlp_norm_g"], lw["w_mlp1"], lw["w_mlp2"], late)
    g["mlp_norm_g"] = jnp.sum(dg2, axis=0)
    early = [_tn_blocks(s["h2"], dz, "dw_mlp1", True), _tn_blocks(s["z"], dx2, "dw_mlp2", False, relu_sq=True)]
    dattn, du, dga, dgc, dbra, dbrc, dbc = _merge_bwd(
        dx1, s["attn"], s["u"], s["rest"], lw["w_attn_proj"], lw["w_conv_proj"], lw["b_conv_proj"], lw["w_out"])
    g["b_conv_proj"] = jnp.sum(dbc, axis=0)
    middle = [_tn_blocks(s["merged"], dx1, "dw_out", False), _tn_blocks(s["attn"], dbra, "dw_attn_proj", True),
              _tn_blocks(s["u"], dbrc, "dw_conv_proj", True)]
    dyc, dlg, dlb = _conv_bwd_norm(du, s["yc"], lw["conv_ln_g"], lw["conv_ln_b"])
    g["conv_ln_g"] = jnp.sum(dlg, axis=0)
    g["conv_ln_b"] = jnp.sum(dlb, axis=0)
    (dglu_a, dglu_b, dcw, dcb), early_recv = _conv_bwd_taps(dyc, s["u0"], s["rest"], lw["conv_w"], _Exchange(early))
    recv.update(zip(_EARLY, early_recv))
    dconv_w = jnp.sum(dcw.reshape(CONV_K, SUBLANES, CONV_C), axis=1)
    g["conv_b"] = jnp.sum(dcb, axis=0)
    (dq, hi, lo, dsk), middle_recv = _attn_bwd(s["qkv"], dattn, lw["sinks"], _Exchange(middle))
    recv.update(zip(_MIDDLE, middle_recv))
    g["sinks"] = -jnp.sum(dsk.reshape(N_Q, SUBLANES * BLOCK), axis=1)
    dproj, dx, dg1, dbin = _inproj_bwd(dq, hi, lo, dglu_a, dglu_b, dga, dgc, s["x"], lw["mix_norm_g"], lw["w_in"],
                                       dx1)
    g["mix_norm_g"] = jnp.sum(dg1, axis=0)
    g["b_in"] = jnp.sum(dbin, axis=0)
    own_late = [_tn_blocks(s["h"], dproj, "dw_in", True),
                dconv_w.reshape(CONV_K, N_DEV, CONV_C // N_DEV).transpose(1, 0, 2)]
    return dx, g, recv, late_recv, own_late


def kernel(x, mix_norm_g, w_in, b_in, sinks, conv_w, conv_b, conv_ln_g, conv_ln_b, w_attn_proj, w_conv_proj, b_conv_proj, w_out, mlp_norm_g, w_mlp1, w_mlp2, final_norm_g, loss_target, m_mix_norm_g, m_w_in, m_b_in, m_sinks, m_conv_w, m_conv_b, m_conv_ln_g, m_conv_ln_b, m_w_attn_proj, m_w_conv_proj, m_b_conv_proj, m_w_out, m_mlp_norm_g, m_w_mlp1, m_w_mlp2, m_final_norm_g, v_mix_norm_g, v_w_in, v_b_in, v_sinks, v_conv_w, v_conv_b, v_conv_ln_g, v_conv_ln_b, v_w_attn_proj, v_w_conv_proj, v_b_conv_proj, v_w_out, v_mlp_norm_g, v_w_mlp1, v_w_mlp2, v_final_norm_g):
    w = dict(mix_norm_g=mix_norm_g, w_in=w_in, b_in=b_in, sinks=sinks, conv_w=conv_w, conv_b=conv_b,
             conv_ln_g=conv_ln_g, conv_ln_b=conv_ln_b, w_attn_proj=w_attn_proj, w_conv_proj=w_conv_proj,
             b_conv_proj=b_conv_proj, w_out=w_out, mlp_norm_g=mlp_norm_g, w_mlp1=w_mlp1, w_mlp2=w_mlp2,
             final_norm_g=final_norm_g)
    m = dict(mix_norm_g=m_mix_norm_g, w_in=m_w_in, b_in=m_b_in, sinks=m_sinks, conv_w=m_conv_w, conv_b=m_conv_b,
             conv_ln_g=m_conv_ln_g, conv_ln_b=m_conv_ln_b, w_attn_proj=m_w_attn_proj, w_conv_proj=m_w_conv_proj,
             b_conv_proj=m_b_conv_proj, w_out=m_w_out, mlp_norm_g=m_mlp_norm_g, w_mlp1=m_w_mlp1, w_mlp2=m_w_mlp2,
             final_norm_g=m_final_norm_g)
    v = dict(mix_norm_g=v_mix_norm_g, w_in=v_w_in, b_in=v_b_in, sinks=v_sinks, conv_w=v_conv_w, conv_b=v_conv_b,
             conv_ln_g=v_conv_ln_g, conv_ln_b=v_conv_ln_b, w_attn_proj=v_w_attn_proj, w_conv_proj=v_w_conv_proj,
             b_conv_proj=v_b_conv_proj, w_out=v_w_out, mlp_norm_g=v_mlp_norm_g, w_mlp1=v_w_mlp1, w_mlp2=v_w_mlp2,
             final_norm_g=v_final_norm_g)
    T = x.shape[1]
    xs = x.reshape(T, D_MODEL)
    target = loss_target.reshape(T, D_MODEL)

    def gather_of(l):
        return _Gather([w[n][l] if n == "conv_w" else w[n][l].astype(BF16) for n in _SHARDED])

    def layer_weights(l, gathered):
        lw = {n: _full_weight(n, a) for n, a in zip(_SHARDED, gathered)}
        for n in _SMALL:
            if n != "final_norm_g":
                lw[n] = w[n][l] if n == "sinks" else w[n][l].reshape(1, -1)
        return lw

    acts = xs
    saved, weights = [], []
    gathered = _run_comm(gather_of(0), "gather_weights")
    for l in range(DEPTH):
        weights.append(layer_weights(l, gathered))
        acts, s, gathered = _layer_fwd(acts, weights[l], gather_of(l + 1) if l + 1 < DEPTH else None)
        saved.append(s)
    lterms, dx, dgf = _final_loss(acts, final_norm_g.reshape(1, -1), target)
    grads, received = [None] * DEPTH, [None] * DEPTH
    late = None
    for l in reversed(range(DEPTH)):
        dx, grads[l], received[l], late_recv, late = _layer_bwd(dx, weights[l], saved[l], late)
        if late_recv is not None:
            received[l + 1].update(zip(_LATE, late_recv))
    received[0].update(zip(_LATE, _run_comm(_Exchange(late), "scatter_late")))
    grad = {n: jnp.stack([grads[l][n] for l in range(DEPTH)]) for n in _SMALL if n != "final_norm_g"}
    grad["final_norm_g"] = jnp.sum(dgf, axis=0)

    small_shapes = [w[n].shape for n in _SMALL] + [(1,)]
    small = _pack([grad[n] for n in _SMALL] + [jnp.sum(lterms).reshape(1)])
    small_parts = _run_comm(_Gather([small]), "gather_small")[0]

    out_g, out_d, out_m, out_v = {}, {}, {}, {}
    for n in _SHARDED:
        out_g[n], out_d[n], out_m[n], out_v[n] = _adamw_sharded(
            [received[l][n] for l in range(DEPTH)], w[n], m[n], v[n], "adamw_" + n)
    zero = jnp.zeros((1,), F32)
    res = _adamw_small(small_parts, _pack([w[n] for n in _SMALL] + [zero]), _pack([m[n] for n in _SMALL] + [zero]),
                       _pack([v[n] for n in _SMALL] + [zero]))
    unpacked = [_unpack(r, small_shapes) for r in res]
    for i, n in enumerate(_SMALL):
        out_g[n], out_d[n], out_m[n], out_v[n] = (u[i] for u in unpacked)
    loss = unpacked[0][-1].reshape(())
    return (loss, dx.reshape(x.shape), *[out_g[n] for n in _ORDER], *[out_d[n] for n in _ORDER],
            *[out_m[n] for n in _ORDER], *[out_v[n] for n in _ORDER])
```

```python
import functools
import math

import jax
import jax.numpy as jnp
from jax import lax
from jax.experimental import pallas as pl
from jax.experimental.pallas import tpu as pltpu

D_MODEL = 1024
SEQ = 16384
DEPTH = 2
N_Q = 8
N_KV = 2
GROUP = N_Q // N_KV
HEAD_DIM = 64
ATTN_W = N_Q * HEAD_DIM
KV_W = N_KV * HEAD_DIM
BLOCK = 128
CONV_C = D_MODEL // 2
CONV_K = 31
D_FF = 4 * D_MODEL
QKV_W = ATTN_W + 2 * KV_W
IN_W = QKV_W + 2 * CONV_C + 2 * D_MODEL
EPS = 1e-6
NEG = -1e30
N_DEV = 8

ADAM_LR = 0.001
ADAM_B1 = 0.9
ADAM_B2 = 0.999
ADAM_EPS = 1e-08
ADAM_WD = 0.01
ADAM_STEP = 10

F32 = jnp.float32
BF16 = jnp.bfloat16
MESH = pl.DeviceIdType.MESH

SUBLANES = 8
HALO = 32
FF_CHUNK = 1024
CONV_ROWS = 32
VMEM_LIMIT = 52 * 1024 * 1024

_NT = (((1,), (1,)), ((), ()))
_TN = (((0,), (0,)), ((), ()))


def _params(*sem):
    return pltpu.CompilerParams(dimension_semantics=sem, vmem_limit_bytes=VMEM_LIMIT)


def _tile(n, pref):
    t = min(n, pref)
    assert n % t == 0, (n, t)
    return t


def _sigmoid(v):
    return 1.0 / (1.0 + jnp.exp(-v))


def _rows8(v):
    r, n = v.shape
    return jnp.sum(v.reshape(r // SUBLANES, SUBLANES, n), axis=0)


def _dot(a, b):
    return jnp.dot(a, b, preferred_element_type=F32)


def _dot_nt(a, b):
    return lax.dot_general(a, b, _NT, preferred_element_type=F32)


def _dot_tn(a, b):
    return lax.dot_general(a, b, _TN, preferred_element_type=F32)


def _rms_bwd(xv, g, dh):
    r = lax.rsqrt(jnp.mean(xv * xv, axis=-1, keepdims=True) + EPS)
    xhat = xv * r
    dxhat = dh * g
    dx = r * (dxhat - xhat * jnp.mean(dxhat * xhat, axis=-1, keepdims=True))
    return dx, dh * xhat


def _row_spec(tm, n, col=0):
    return pl.BlockSpec((tm, n), lambda i: (i, col))


def _full_spec(shape):
    return pl.BlockSpec(shape, lambda *_: (0,) * len(shape))


def _weight_spec(shape):
    return pl.BlockSpec(shape, lambda *_: (0,) * len(shape), pipeline_mode=pl.Buffered(1))


def _mesh_pos():
    return lax.axis_index("x"), lax.axis_index("y"), lax.axis_index("c")


def _dev_index(dev):
    return 4 * dev[0] + 2 * dev[1] + dev[2]


class _Exchange:
    middle_at = None

    def __init__(self, arrs):
        self.arrays = list(arrs)

    def out_shape(self):
        return [jax.ShapeDtypeStruct(a.shape, a.dtype) for a in self.arrays]

    def scratch(self):
        n = len(self.arrays)
        return [pltpu.SemaphoreType.DMA((7 * n,)), pltpu.SemaphoreType.DMA((7 * n,)), pltpu.SemaphoreType.DMA((n,))]

    def _copies(self, ins, outs, sems):
        send_sems, recv_sems, local_sems = sems
        x, y, c = _mesh_pos()
        me = _dev_index((x, y, c))
        mine, sends, arrivals = [], [], []
        for p in range(len(self.arrays)):
            mine.append(pltpu.make_async_copy(ins[p].at[me], outs[p].at[me], local_sems.at[p]))
            for k in range(1, N_DEV):
                peer = (1 - x if k & 4 else x, 1 - y if k & 2 else y, 1 - c if k & 1 else c)
                pid = _dev_index(peer)
                pair = dict(send_sem=send_sems.at[7 * p + k - 1], recv_sem=recv_sems.at[7 * p + k - 1],
                            device_id=peer, device_id_type=MESH)
                sends.append(pltpu.make_async_remote_copy(src_ref=ins[p].at[pid], dst_ref=outs[p].at[me], **pair))
                arrivals.append(pltpu.make_async_remote_copy(src_ref=ins[p].at[pid], dst_ref=outs[p].at[pid], **pair))
        return mine, sends, arrivals

    def start(self, ins, outs, sems):
        mine, sends, _ = self._copies(ins, outs, sems)
        for cp in mine + sends:
            cp.start()

    def finish(self, ins, outs, sems):
        mine, sends, arrivals = self._copies(ins, outs, sems)
        for cp in arrivals:
            cp.wait_recv()
        for cp in sends:
            cp.wait_send()
        for cp in mine:
            cp.wait()


class _Gather:
    middle_at = 0.75

    def __init__(self, arrs):
        self.arrays = list(arrs)

    def out_shape(self):
        return [jax.ShapeDtypeStruct((N_DEV,) + a.shape, a.dtype) for a in self.arrays]

    def scratch(self):
        n = len(self.arrays)
        return [pltpu.SemaphoreType.DMA((7 * n,)), pltpu.SemaphoreType.DMA((7 * n,)), pltpu.SemaphoreType.DMA((n,))]

    def _copies(self, ins, outs, sems):
        send_sems, recv_sems, local_sems = sems
        x, y, c = _mesh_pos()
        me, sibling = (x, y, c), (x, y, 1 - c)
        chips = [(1 - x, y), (x, 1 - y), (1 - x, 1 - y)]
        n = len(self.arrays)

        def copy(p, k, dev, to, src=None):
            block = outs[p].at[_dev_index(dev)]
            return pltpu.make_async_remote_copy(
                src_ref=block if src is None else src, dst_ref=block,
                send_sem=send_sems.at[7 * p + k], recv_sem=recv_sems.at[7 * p + k],
                device_id=to, device_id_type=MESH)

        cp = dict(mine=[pltpu.make_async_copy(ins[p], outs[p].at[_dev_index(me)], local_sems.at[p])
                        for p in range(n)])
        cp["first"] = [copy(p, 0, me, sibling, src=ins[p]) for p in range(n)]
        cp["first"] += [copy(p, 1 + j, me, (*chip, c), src=ins[p]) for p in range(n) for j, chip in enumerate(chips)]
        cp["over_ici"] = [copy(p, 1 + j, (*chip, c), me) for j, chip in enumerate(chips) for p in range(n)]
        cp["passed"] = [copy(p, 4 + j, (*chip, c), sibling) for j, chip in enumerate(chips) for p in range(n)]
        cp["from_sibling"] = [copy(p, 0, sibling, me) for p in range(n)]
        cp["from_sibling"] += [copy(p, 4 + j, (*chip, 1 - c), me) for j, chip in enumerate(chips) for p in range(n)]
        return cp

    def start(self, ins, outs, sems):
        cp = self._copies(ins, outs, sems)
        for d in cp["mine"] + cp["first"]:
            d.start()

    def middle(self, ins, outs, sems):
        cp = self._copies(ins, outs, sems)
        for arrived, onward in zip(cp["over_ici"], cp["passed"]):
            arrived.wait_recv()
            onward.start()

    def finish(self, ins, outs, sems):
        cp = self._copies(ins, outs, sems)
        for d in cp["from_sibling"]:
            d.wait_recv()
        for d in cp["first"] + cp["passed"]:
            d.wait_send()
        for d in cp["mine"]:
            d.wait()


def _run_comm(comm, name):
    n = len(comm.arrays)

    def body(*refs):
        ins, outs, sems = refs[:n], refs[n:2 * n], refs[2 * n:]
        comm.start(ins, outs, sems)
        if comm.middle_at is not None:
            comm.middle(ins, outs, sems)
        comm.finish(ins, outs, sems)

    any_spec = pl.BlockSpec(memory_space=pl.ANY)
    return pl.pallas_call(
        body, name=name, in_specs=[any_spec] * n, out_specs=[any_spec] * n, out_shape=comm.out_shape(),
        scratch_shapes=comm.scratch(),
    )(*comm.arrays)


def _pallas(body, *, name, grid, in_specs, out_specs, out_shape, args, sem, scratch_shapes=(), comm=None):
    if comm is None:
        outs = pl.pallas_call(
            body, name=name, grid=grid, in_specs=in_specs, out_specs=out_specs, out_shape=out_shape,
            scratch_shapes=list(scratch_shapes), compiler_params=_params(*sem),
        )(*args)
        return outs, None
    n_in, n_out, n_scr, n_c = len(in_specs), len(out_specs), len(scratch_shapes), len(comm.arrays)
    steps = grid[0]
    middle = None if comm.middle_at is None else min(steps - 1, int(steps * comm.middle_at))

    def carried(*refs):
        ins, refs = refs[:n_in], refs[n_in:]
        cins, refs = refs[:n_c], refs[n_c:]
        outs, refs = refs[:n_out], refs[n_out:]
        couts, refs = refs[:n_c], refs[n_c:]
        scr, csems = refs[:n_scr], refs[n_scr:]
        step = pl.program_id(0)

        @pl.when(step == 0)
        def _():
            comm.start(cins, couts, csems)

        body(*ins, *outs, *scr)

        if middle is not None:
            @pl.when(step == middle)
            def _():
                comm.middle(cins, couts, csems)

        @pl.when(step == steps - 1)
        def _():
            comm.finish(cins, couts, csems)

    any_spec = pl.BlockSpec(memory_space=pl.ANY)
    res = pl.pallas_call(
        carried, name=name, grid=grid,
        in_specs=list(in_specs) + [any_spec] * n_c, out_specs=list(out_specs) + [any_spec] * n_c,
        out_shape=list(out_shape) + comm.out_shape(),
        scratch_shapes=list(scratch_shapes) + comm.scratch(),
        compiler_params=_params(*(("arbitrary",) + tuple(sem[1:]))),
    )(*args, *comm.arrays)
    return res[:n_out], res[n_out:]


def _inproj_fwd(x, g, w, b, comm=None):
    T, D = x.shape
    rest_w = IN_W - QKV_W
    tm = _tile(T, 512)

    def body(x_ref, g_ref, w_ref, b_ref, h_ref, qkv_ref, rest_ref):
        xv = x_ref[...]
        r = lax.rsqrt(jnp.mean(xv * xv, axis=-1, keepdims=True) + EPS)
        h = (xv * r * g_ref[...]).astype(BF16)
        h_ref[...] = h
        qkv_ref[...] = (_dot(h, w_ref[:, :QKV_W]) + b_ref[:, :QKV_W]).astype(BF16)
        rest_ref[...] = (_dot(h, w_ref[:, QKV_W:]) + b_ref[:, QKV_W:]).astype(BF16)

    return _pallas(
        body, name="inproj_fwd", grid=(T // tm,),
        in_specs=[_row_spec(tm, D), _full_spec((1, D)), _weight_spec((D, IN_W)), _full_spec((1, IN_W))],
        out_specs=[_row_spec(tm, D), _row_spec(tm, QKV_W), _row_spec(tm, rest_w)],
        out_shape=[jax.ShapeDtypeStruct((T, D), BF16), jax.ShapeDtypeStruct((T, QKV_W), BF16),
                   jax.ShapeDtypeStruct((T, rest_w), BF16)],
        args=(x, g, w, b), sem=("parallel",), comm=comm)


def _fold_masks(first):
    row = lax.broadcasted_iota(jnp.int32, (BLOCK, BLOCK), 0)
    col = lax.broadcasted_iota(jnp.int32, (BLOCK, BLOCK), 1)
    upper = col > row
    dist = jnp.where(upper, row + BLOCK - col, row - col)
    keep = col <= row + jnp.where(first, 0, BLOCK)
    return upper, dist.astype(F32), keep


def _fold(band, upper):
    return jnp.where(upper, band[:, :BLOCK], band[:, BLOCK:])


def _unfold(folded, upper):
    return jnp.concatenate([jnp.where(upper, folded, 0.0), jnp.where(upper, 0.0, folded)], axis=1)


def _fill_kv(kv_buf, cur_ref, prev_ref):
    scale = 1.0 / math.sqrt(HEAD_DIM)
    assert math.frexp(scale)[0] == 0.5
    for r0, ref in ((0, prev_ref), (BLOCK, cur_ref)):
        rows = ref.shape[0]
        kv_buf[r0:r0 + rows, :KV_W] = ref[:, ATTN_W:ATTN_W + KV_W] * scale
        kv_buf[r0:r0 + rows, KV_W:] = ref[:, ATTN_W + KV_W:]


def _group_rows(x, kh):
    return jnp.concatenate([x[:, h * HEAD_DIM:(h + 1) * HEAD_DIM] for h in range(kh * GROUP, (kh + 1) * GROUP)],
                           axis=0)


def _attn_probs(scores, h, sink, upper, distf, keep):
    s = _fold(scores, upper) - (2.0 ** (-8.0 * (h + 1) / N_Q)) * distf
    if keep is not None:
        s = jnp.where(keep, s, NEG)
    m = jnp.maximum(jnp.max(s, axis=-1, keepdims=True), sink)
    p = jnp.exp(s - m)
    e = jnp.exp(sink - m)
    inv = 1.0 / (jnp.sum(p, axis=-1, keepdims=True) + e)
    return p * inv, e * inv


def _attn_fwd(qkv, sinks):
    T = qkv.shape[0]
    tq = _tile(T, 512)
    nblk = tq // BLOCK

    def body(sink_ref, cur_ref, prev_ref, o_ref, kv_buf):
        _fill_kv(kv_buf, cur_ref, prev_ref)
        upper, distf, keep = _fold_masks(pl.program_id(0) == 0)
        units = [(j, kh) for j in range(nblk) for kh in range(N_KV)]

        def key_band(j, kh):
            return kv_buf[j * BLOCK:(j + 2) * BLOCK, kh * HEAD_DIM:(kh + 1) * HEAD_DIM]

        def value_band(j, kh):
            return kv_buf[j * BLOCK:(j + 2) * BLOCK, KV_W + kh * HEAD_DIM:KV_W + (kh + 1) * HEAD_DIM]

        scores = {(j, kh): _dot_nt(_group_rows(cur_ref[j * BLOCK:(j + 1) * BLOCK, :ATTN_W], kh), key_band(j, kh))
                  for j, kh in units}
        probs = {}
        for j, kh in units:
            ps = []
            for g in range(GROUP):
                h = kh * GROUP + g
                p, _ = _attn_probs(scores[j, kh][g * BLOCK:(g + 1) * BLOCK, :], h, sink_ref[h], upper, distf,
                                   keep if j == 0 else None)
                ps.append(_unfold(p, upper).astype(BF16))
            probs[j, kh] = jnp.concatenate(ps, axis=0)
        outs = {u: _dot(probs[u], value_band(*u)) for u in units}
        for j in range(nblk):
            heads = [outs[j, kh][g * BLOCK:(g + 1) * BLOCK, :] for kh in range(N_KV) for g in range(GROUP)]
            o_ref[j * BLOCK:(j + 1) * BLOCK, :] = jnp.concatenate(heads, axis=1).astype(BF16)

    return pl.pallas_call(
        body, name="attn_fwd", grid=(T // tq,),
        in_specs=[pl.BlockSpec(memory_space=pltpu.SMEM),
                  _row_spec(tq, QKV_W),
                  pl.BlockSpec((BLOCK, QKV_W), lambda i: (jnp.maximum(i * nblk - 1, 0), 0))],
        out_specs=_row_spec(tq, ATTN_W),
        out_shape=jax.ShapeDtypeStruct((T, ATTN_W), BF16),
        scratch_shapes=[pltpu.VMEM((tq + BLOCK, 2 * KV_W), BF16)],
        compiler_params=_params("parallel"),
    )(sinks, qkv, qkv)


def _shifted_copies(buf):
    n = buf.shape[1] - SUBLANES
    for s in range(1, SUBLANES):
        buf[s, 0:n, :] = buf[0, s:s + n, :]


def _shifted_rows(buf, start, rows):
    s = start % SUBLANES
    return buf[s, start - s:start - s + rows, :]


def _conv_fwd(rest, cw, cb, lg, lb):
    T = rest.shape[0]
    C = CONV_C
    tm = _tile(T, 256)
    R = _tile(tm, CONV_ROWS)
    per = tm // HALO

    def body(cur_ref, prev_ref, w_ref, cb_ref, g_ref, b_ref, u0_ref, yc_ref, u_ref, ubuf):
        i = pl.program_id(0)
        up = prev_ref[:, :C].astype(F32) * _sigmoid(prev_ref[:, C:].astype(F32))
        ubuf[0, 0:HALO, :] = jnp.where(i > 0, up, 0.0)
        u0 = cur_ref[:, :C].astype(F32) * _sigmoid(cur_ref[:, C:].astype(F32))
        ubuf[0, HALO:, :] = u0
        u0_ref[...] = u0
        _shifted_copies(ubuf)
        off = HALO - (CONV_K - 1)
        for c in range(tm // R):
            acc = jnp.broadcast_to(cb_ref[...], (R, C))
            for j in range(CONV_K):
                acc = acc + w_ref[j:j + 1, :] * _shifted_rows(ubuf, c * R + off + j, R)
            yc_ref[c * R:(c + 1) * R, :] = acc
            xc = acc - jnp.mean(acc, axis=-1, keepdims=True)
            ln = xc * lax.rsqrt(jnp.mean(xc * xc, axis=-1, keepdims=True) + EPS) * g_ref[...] + b_ref[...]
            u_ref[c * R:(c + 1) * R, :] = (ln * _sigmoid(ln)).astype(BF16)

    return pl.pallas_call(
        body, name="conv_fwd", grid=(T // tm,),
        in_specs=[_row_spec(tm, 2 * C),
                  pl.BlockSpec((HALO, 2 * C), lambda i: (jnp.maximum(i * per - 1, 0), 0)),
                  _full_spec((CONV_K, C)), _full_spec((1, C)), _full_spec((1, C)), _full_spec((1, C))],
        out_specs=[_row_spec(tm, C), _row_spec(tm, C), _row_spec(tm, C)],
        out_shape=[jax.ShapeDtypeStruct((T, C), F32), jax.ShapeDtypeStruct((T, C), F32),
                   jax.ShapeDtypeStruct((T, C), BF16)],
        scratch_shapes=[pltpu.VMEM((SUBLANES, tm + HALO, C), F32)],
        compiler_params=_params("parallel"),
    )(rest, rest, cw, cb, lg, lb)


def _merge_fwd(attn, u, rest, x, wa, wc, bc, wo):
    T, D = x.shape
    tm = _tile(T, 512)
    gcol = 2 * CONV_C // D

    def body(attn_ref, u_ref, ga_ref, gc_ref, x_ref, wa_ref, wc_ref, bc_ref, wo_ref, m_ref, x1_ref):
        bra = _dot(attn_ref[...], wa_ref[...])
        brc = _dot(u_ref[...], wc_ref[...]) + bc_ref[...]
        mb = (_sigmoid(ga_ref[...].astype(F32)) * bra + _sigmoid(gc_ref[...].astype(F32)) * brc).astype(BF16)
        m_ref[...] = mb
        x1_ref[...] = x_ref[...] + _dot(mb, wo_ref[...])

    return pl.pallas_call(
        body, name="merge_fwd", grid=(T // tm,),
        in_specs=[_row_spec(tm, ATTN_W), _row_spec(tm, CONV_C), _row_spec(tm, D, gcol), _row_spec(tm, D, gcol + 1),
                  _row_spec(tm, D), _weight_spec((ATTN_W, D)), _weight_spec((CONV_C, D)), _full_spec((1, D)),
                  _weight_spec((D, D))],
        out_specs=[_row_spec(tm, D), _row_spec(tm, D)],
        out_shape=[jax.ShapeDtypeStruct((T, D), BF16), jax.ShapeDtypeStruct((T, D), F32)],
        compiler_params=_params("parallel"),
    )(attn, u, rest, rest, x, wa, wc, bc, wo)


def _mlp_fwd(x1, g, w1, w2, comm=None):
    T, D = x1.shape
    tm = _tile(T, 512)
    fc = _tile(D_FF, FF_CHUNK)

    def body(x_ref, g_ref, w1_ref, w2_ref, h_ref, z_ref, o_ref):
        xv = x_ref[...]
        r = lax.rsqrt(jnp.mean(xv * xv, axis=-1, keepdims=True) + EPS)
        h = (xv * r * g_ref[...]).astype(BF16)
        h_ref[...] = h
        acc = xv
        for c in range(D_FF // fc):
            cols = slice(c * fc, (c + 1) * fc)
            z = _dot(h, w1_ref[:, cols])
            z_ref[:, cols] = z.astype(BF16)
            acc = acc + _dot(jnp.square(jnp.maximum(z, 0.0)).astype(BF16), w2_ref[cols, :])
        o_ref[...] = acc

    return _pallas(
        body, name="mlp_fwd", grid=(T // tm,),
        in_specs=[_row_spec(tm, D), _full_spec((1, D)), _weight_spec((D, D_FF)), _weight_spec((D_FF, D))],
        out_specs=[_row_spec(tm, D), _row_spec(tm, D_FF), _row_spec(tm, D)],
        out_shape=[jax.ShapeDtypeStruct((T, D), BF16), jax.ShapeDtypeStruct((T, D_FF), BF16),
                   jax.ShapeDtypeStruct((T, D), F32)],
        args=(x1, g, w1, w2), sem=("parallel",), comm=comm)


def _final_loss(x, g, target):
    T, D = x.shape
    tm = _tile(T, 512)

    def body(x_ref, g_ref, t_ref, l_ref, dx_ref, dg_ref):
        @pl.when(pl.program_id(0) == 0)
        def _():
            l_ref[...] = jnp.zeros_like(l_ref)
            dg_ref[...] = jnp.zeros_like(dg_ref)

        xv = x_ref[...]
        r = lax.rsqrt(jnp.mean(xv * xv, axis=-1, keepdims=True) + EPS)
        e = xv * r * g_ref[...] - t_ref[...]
        l_ref[...] += _rows8(e * e) * (0.5 / D)
        dx, dg = _rms_bwd(xv, g_ref[...], e * (1.0 / D))
        dx_ref[...] = dx
        dg_ref[...] += _rows8(dg)

    return pl.pallas_call(
        body, name="final_loss", grid=(T // tm,),
        in_specs=[_row_spec(tm, D), _full_spec((1, D)), _row_spec(tm, D)],
        out_specs=[_full_spec((SUBLANES, D)), _row_spec(tm, D), _full_spec((SUBLANES, D))],
        out_shape=[jax.ShapeDtypeStruct((SUBLANES, D), F32), jax.ShapeDtypeStruct((T, D), F32),
                   jax.ShapeDtypeStruct((SUBLANES, D), F32)],
        compiler_params=_params("arbitrary"),
    )(x, g, target)


def _mlp_bwd(dx2, x1, z, g, w1, w2, comm=None):
    T, D = x1.shape
    tm = _tile(T, 512)
    fc = _tile(D_FF, FF_CHUNK)

    def body(dx2_ref, x_ref, z_ref, g_ref, w1_ref, w2_ref, dx1_ref, dz_ref, dg_ref):
        @pl.when(pl.program_id(0) == 0)
        def _():
            dg_ref[...] = jnp.zeros_like(dg_ref)

        dxo = dx2_ref[...]
        dxb = dxo.astype(BF16)
        dh = jnp.zeros((tm, D), F32)
        for c in range(D_FF // fc):
            cols = slice(c * fc, (c + 1) * fc)
            da = _dot_nt(dxb, w2_ref[cols, :])
            dz = (da * (2.0 * jnp.maximum(z_ref[:, cols].astype(F32), 0.0))).astype(BF16)
            dz_ref[:, cols] = dz
            dh = dh + _dot_nt(dz, w1_ref[:, cols])
        dx, dg = _rms_bwd(x_ref[...], g_ref[...], dh)
        dx1_ref[...] = dxo + dx
        dg_ref[...] += _rows8(dg)

    return _pallas(
        body, name="mlp_bwd", grid=(T // tm,),
        in_specs=[_row_spec(tm, D), _row_spec(tm, D), _row_spec(tm, D_FF), _full_spec((1, D)),
                  _weight_spec((D, D_FF)), _weight_spec((D_FF, D))],
        out_specs=[_row_spec(tm, D), _row_spec(tm, D_FF), _full_spec((SUBLANES, D))],
        out_shape=[jax.ShapeDtypeStruct((T, D), F32), jax.ShapeDtypeStruct((T, D_FF), BF16),
                   jax.ShapeDtypeStruct((SUBLANES, D), F32)],
        args=(dx2, x1, z, g, w1, w2), sem=("arbitrary",), comm=comm)


def _tn_blocks(a, b, name, col_sharded, relu_sq=False):
    T, M = a.shape
    N = b.shape[1]
    tk = _tile(T, 1024)
    tm = _tile(M, 512 if col_sharded else 1024)
    nb = N // N_DEV
    last = T // tk - 1

    def body(a_ref, b_ref, o_ref, acc_ref):
        k = pl.program_id(1)

        @pl.when(k == 0)
        def _():
            acc_ref[...] = jnp.zeros_like(acc_ref)

        av = a_ref[...]
        if relu_sq:
            av = jnp.square(jnp.maximum(av.astype(F32), 0.0))
        acc_ref[...] += _dot_tn(av.astype(BF16), b_ref[...].astype(BF16))

        @pl.when(k == last)
        def _():
            if col_sharded:
                for d in range(N_DEV):
                    o_ref[d] = acc_ref[:, d * nb:(d + 1) * nb].astype(BF16)
            else:
                o_ref[...] = acc_ref[...].astype(BF16)

    if col_sharded:
        out_spec = pl.BlockSpec((N_DEV, tm, nb), lambda i, k: (0, i, 0))
        out_shape = jax.ShapeDtypeStruct((N_DEV, M, nb), BF16)
    else:
        out_spec = pl.BlockSpec((tm, N), lambda i, k: (i, 0))
        out_shape = jax.ShapeDtypeStruct((M, N), BF16)
    out = pl.pallas_call(
        body, name=name, grid=(M // tm, T // tk),
        in_specs=[pl.BlockSpec((tk, tm), lambda i, k: (k, i)), pl.BlockSpec((tk, N), lambda i, k: (k, 0))],
        out_specs=out_spec, out_shape=out_shape,
        scratch_shapes=[pltpu.VMEM((tm, N), F32)],
        compiler_params=_params("parallel", "arbitrary"),
    )(a, b)
    return out if col_sharded else out.reshape(N_DEV, M // N_DEV, N)


def _merge_bwd(dx1, attn, u, rest, wa, wc, bc, wo):
    T, D = dx1.shape
    tm = _tile(T, 512)
    gcol = 2 * CONV_C // D

    def body(dx_ref, attn_ref, u_ref, ga_ref, gc_ref, wa_ref, wc_ref, bc_ref, wo_ref,
             dattn_ref, du_ref, dga_ref, dgc_ref, dbra_ref, dbrc_ref, dbc_ref):
        @pl.when(pl.program_id(0) == 0)
        def _():
            dbc_ref[...] = jnp.zeros_like(dbc_ref)

        dm = _dot_nt(dx_ref[...].astype(BF16), wo_ref[...])
        bra = _dot(attn_ref[...], wa_ref[...])
        brc = _dot(u_ref[...], wc_ref[...]) + bc_ref[...]
        sa = _sigmoid(ga_ref[...].astype(F32))
        sc = _sigmoid(gc_ref[...].astype(F32))
        dbra = dm * sa
        dbrc = dm * sc
        dga_ref[...] = (dm * bra * sa * (1.0 - sa)).astype(BF16)
        dgc_ref[...] = (dm * brc * sc * (1.0 - sc)).astype(BF16)
        dbra_b = dbra.astype(BF16)
        dbrc_b = dbrc.astype(BF16)
        dbra_ref[...] = dbra_b
        dbrc_ref[...] = dbrc_b
        dbc_ref[...] += _rows8(dbrc)
        dattn_ref[...] = _dot_nt(dbra_b, wa_ref[...]).astype(BF16)
        du_ref[...] = _dot_nt(dbrc_b, wc_ref[...])

    return pl.pallas_call(
        body, name="merge_bwd", grid=(T // tm,),
        in_specs=[_row_spec(tm, D), _row_spec(tm, ATTN_W), _row_spec(tm, CONV_C), _row_spec(tm, D, gcol),
                  _row_spec(tm, D, gcol + 1), _weight_spec((ATTN_W, D)), _weight_spec((CONV_C, D)),
                  _full_spec((1, D)), _weight_spec((D, D))],
        out_specs=[_row_spec(tm, ATTN_W), _row_spec(tm, CONV_C), _row_spec(tm, D), _row_spec(tm, D),
                   _row_spec(tm, D), _row_spec(tm, D), _full_spec((SUBLANES, D))],
        out_shape=[jax.ShapeDtypeStruct((T, ATTN_W), BF16), jax.ShapeDtypeStruct((T, CONV_C), F32),
                   jax.ShapeDtypeStruct((T, D), BF16), jax.ShapeDtypeStruct((T, D), BF16),
                   jax.ShapeDtypeStruct((T, D), BF16), jax.ShapeDtypeStruct((T, D), BF16),
                   jax.ShapeDtypeStruct((SUBLANES, D), F32)],
        compiler_params=_params("arbitrary"),
    )(dx1, attn, u, rest, rest, wa, wc, bc, wo)


def _conv_bwd_norm(du, yc, lg, lb):
    T, C = yc.shape
    tm = _tile(T, 512)

    def body(du_ref, yc_ref, g_ref, b_ref, dyc_ref, dg_ref, db_ref):
        @pl.when(pl.program_id(0) == 0)
        def _():
            dg_ref[...] = jnp.zeros_like(dg_ref)
            db_ref[...] = jnp.zeros_like(db_ref)

        yv = yc_ref[...]
        xc = yv - jnp.mean(yv, axis=-1, keepdims=True)
        rstd = lax.rsqrt(jnp.mean(xc * xc, axis=-1, keepdims=True) + EPS)
        xn = xc * rstd
        ln = xn * g_ref[...] + b_ref[...]
        sg = _sigmoid(ln)
        dln = du_ref[...] * sg * (1.0 + ln * (1.0 - sg))
        dg_ref[...] += _rows8(dln * xn)
        db_ref[...] += _rows8(dln)
        dxn = dln * g_ref[...]
        dyc_ref[...] = rstd * (dxn - jnp.mean(dxn, axis=-1, keepdims=True)
                               - xn * jnp.mean(dxn * xn, axis=-1, keepdims=True))

    return pl.pallas_call(
        body, name="conv_bwd_norm", grid=(T // tm,),
        in_specs=[_row_spec(tm, C), _row_spec(tm, C), _full_spec((1, C)), _full_spec((1, C))],
        out_specs=[_row_spec(tm, C), _full_spec((SUBLANES, C)), _full_spec((SUBLANES, C))],
        out_shape=[jax.ShapeDtypeStruct((T, C), F32), jax.ShapeDtypeStruct((SUBLANES, C), F32),
                   jax.ShapeDtypeStruct((SUBLANES, C), F32)],
        compiler_params=_params("arbitrary"),
    )(du, yc, lg, lb)


def _conv_bwd_taps(dyc, u0, rest, cw, comm=None):
    T, C = dyc.shape
    tm = _tile(T, 256)
    R = _tile(tm, CONV_ROWS)
    per = tm // HALO
    nt = T // tm

    def body(dy_ref, dyn_ref, u0_ref, u0p_ref, glu_ref, w_ref, dga_ref, dgb_ref, dw_ref, db_ref, dbuf, ubuf):
        i = pl.program_id(0)

        @pl.when(i == 0)
        def _():
            dw_ref[...] = jnp.zeros_like(dw_ref)
            db_ref[...] = jnp.zeros_like(db_ref)

        dbuf[0, 0:tm, :] = dy_ref[...]
        dbuf[0, tm:, :] = jnp.where(i < nt - 1, dyn_ref[...], 0.0)
        ubuf[0, 0:HALO, :] = jnp.where(i > 0, u0p_ref[...], 0.0)
        ubuf[0, HALO:, :] = u0_ref[...]
        _shifted_copies(dbuf)
        _shifted_copies(ubuf)
        off = HALO - (CONV_K - 1)
        for c in range(tm // R):
            dy = dbuf[0, c * R:(c + 1) * R, :]
            acc = jnp.zeros((R, C), F32)
            for j in range(CONV_K):
                acc = acc + w_ref[j:j + 1, :] * _shifted_rows(dbuf, c * R + CONV_K - 1 - j, R)
                dw_ref[j * SUBLANES:(j + 1) * SUBLANES, :] += _rows8(dy * _shifted_rows(ubuf, c * R + off + j, R))
            db_ref[...] += _rows8(dy)
            a = glu_ref[c * R:(c + 1) * R, :C].astype(F32)
            sb = _sigmoid(glu_ref[c * R:(c + 1) * R, C:].astype(F32))
            dga_ref[c * R:(c + 1) * R, :] = (acc * sb).astype(BF16)
            dgb_ref[c * R:(c + 1) * R, :] = (acc * a * sb * (1.0 - sb)).astype(BF16)

    return _pallas(
        body, name="conv_bwd_taps", grid=(nt,),
        in_specs=[_row_spec(tm, C),
                  pl.BlockSpec((HALO, C), lambda i: (jnp.minimum((i + 1) * per, T // HALO - 1), 0)),
                  _row_spec(tm, C),
                  pl.BlockSpec((HALO, C), lambda i: (jnp.maximum(i * per - 1, 0), 0)),
                  _row_spec(tm, 2 * C), _full_spec((CONV_K, C))],
        out_specs=[_row_spec(tm, C), _row_spec(tm, C), _full_spec((CONV_K * SUBLANES, C)),
                   _full_spec((SUBLANES, C))],
        out_shape=[jax.ShapeDtypeStruct((T, C), BF16), jax.ShapeDtypeStruct((T, C), BF16),
                   jax.ShapeDtypeStruct((CONV_K * SUBLANES, C), F32), jax.ShapeDtypeStruct((SUBLANES, C), F32)],
        scratch_shapes=[pltpu.VMEM((SUBLANES, tm + HALO, C), F32), pltpu.VMEM((SUBLANES, tm + HALO, C), F32)],
        args=(dyc, dyc, u0, u0, rest, cw), sem=("arbitrary",), comm=comm)


def _attn_bwd(qkv, dattn, sinks, comm=None):
    T = qkv.shape[0]
    tq = _tile(T, 512)
    nblk = tq // BLOCK
    scale = 1.0 / math.sqrt(HEAD_DIM)

    def body(sink_ref, cur_ref, prev_ref, do_ref, dq_ref, hi_ref, lo_ref, ds_ref, kv_buf):
        i = pl.program_id(0)

        @pl.when(i == 0)
        def _():
            ds_ref[...] = jnp.zeros_like(ds_ref)

        _fill_kv(kv_buf, cur_ref, prev_ref)
        upper, distf, keep = _fold_masks(i == 0)
        for j in range(nblk):
            rows = slice(j * BLOCK, (j + 1) * BLOCK)
            band = kv_buf[j * BLOCK:(j + 2) * BLOCK, :]
            q = cur_ref[rows, :ATTN_W]
            do = do_ref[rows, :]
            dqs, dks, dvs = [], [], []
            for kh in range(N_KV):
                k = band[:, kh * HEAD_DIM:(kh + 1) * HEAD_DIM]
                v = band[:, KV_W + kh * HEAD_DIM:KV_W + (kh + 1) * HEAD_DIM]
                qg = _group_rows(q, kh)
                dog = _group_rows(do, kh)
                scores = _dot_nt(qg, k)
                dps = _dot_nt(dog, v)
                ps, dss = [], []
                for g in range(GROUP):
                    h = kh * GROUP + g
                    head = slice(g * BLOCK, (g + 1) * BLOCK)
                    p, psink = _attn_probs(scores[head, :], h, sink_ref[h], upper, distf, keep if j == 0 else None)
                    pdp = p * _fold(dps[head, :], upper)
                    delta = jnp.sum(pdp, axis=-1, keepdims=True)
                    ds_ref[h * SUBLANES:(h + 1) * SUBLANES, :] += _rows8(psink * pdp)
                    dss.append(_unfold(pdp - p * delta, upper).astype(BF16))
                    ps.append(_unfold(p, upper).astype(BF16))
                dsg = jnp.concatenate(dss, axis=0)
                dqg = _dot(dsg, k)
                dqs += [dqg[g * BLOCK:(g + 1) * BLOCK, :] for g in range(GROUP)]
                dks.append(_dot_tn(dsg, qg) * scale)
                dvs.append(_dot_tn(jnp.concatenate(ps, axis=0), dog))
            dq_ref[rows, :] = jnp.concatenate(dqs, axis=1).astype(BF16)
            dkv = jnp.concatenate(dks + dvs, axis=1)
            lo_ref[rows, :] = dkv[:BLOCK, :]
            hi_ref[rows, :] = dkv[BLOCK:, :]

    return _pallas(
        body, name="attn_bwd", grid=(T // tq,),
        in_specs=[pl.BlockSpec(memory_space=pltpu.SMEM),
                  _row_spec(tq, QKV_W),
                  pl.BlockSpec((BLOCK, QKV_W), lambda i: (jnp.maximum(i * nblk - 1, 0), 0)),
                  _row_spec(tq, ATTN_W)],
        out_specs=[_row_spec(tq, ATTN_W), _row_spec(tq, 2 * KV_W), _row_spec(tq, 2 * KV_W),
                   _full_spec((N_Q * SUBLANES, BLOCK))],
        out_shape=[jax.ShapeDtypeStruct((T, ATTN_W), BF16), jax.ShapeDtypeStruct((T, 2 * KV_W), F32),
                   jax.ShapeDtypeStruct((T, 2 * KV_W), F32), jax.ShapeDtypeStruct((N_Q * SUBLANES, BLOCK), F32)],
        scratch_shapes=[pltpu.VMEM((tq + BLOCK, 2 * KV_W), BF16)],
        args=(sinks, qkv, qkv, dattn), sem=("arbitrary",), comm=comm)


def _inproj_bwd(dq, hi, lo, dglu_a, dglu_b, dga, dgc, x, g, w, dx1):
    T, D = x.shape
    C = CONV_C
    tm = _tile(T, 256)
    per = tm // BLOCK
    nt = T // tm
    kv2 = 2 * KV_W

    def body(dq_ref, hi_ref, lo_ref, lon_ref, da_ref, db_ref, dga_ref, dgc_ref, x_ref, g_ref, w_ref, dx1_ref,
             dp_ref, dx_ref, dg_ref, dbias_ref):
        i = pl.program_id(0)

        @pl.when(i == 0)
        def _():
            dg_ref[...] = jnp.zeros_like(dg_ref)
            dbias_ref[...] = jnp.zeros_like(dbias_ref)

        dp_ref[:, :ATTN_W] = dq_ref[...]
        lo_next = jnp.where(i < nt - 1, lon_ref[...], 0.0)
        if tm > BLOCK:
            lo_shift = jnp.concatenate([lo_ref[BLOCK:, :], lo_next], axis=0)
        else:
            lo_shift = lo_next
        dp_ref[:, ATTN_W:QKV_W] = (hi_ref[...] + lo_shift).astype(BF16)
        dp_ref[:, QKV_W:QKV_W + C] = da_ref[...]
        dp_ref[:, QKV_W + C:QKV_W + 2 * C] = db_ref[...]
        dp_ref[:, QKV_W + 2 * C:QKV_W + 2 * C + D] = dga_ref[...]
        dp_ref[:, QKV_W + 2 * C + D:] = dgc_ref[...]
        dp = dp_ref[...]
        dbias_ref[...] += _rows8(dp.astype(F32))
        dh = _dot_nt(dp, w_ref[...])
        dx, dg = _rms_bwd(x_ref[...], g_ref[...], dh)
        dx_ref[...] = dx1_ref[...] + dx
        dg_ref[...] += _rows8(dg)

    return pl.pallas_call(
        body, name="inproj_bwd", grid=(nt,),
        in_specs=[_row_spec(tm, ATTN_W), _row_spec(tm, kv2), _row_spec(tm, kv2),
                  pl.BlockSpec((BLOCK, kv2), lambda i: (jnp.minimum((i + 1) * per, T // BLOCK - 1), 0)),
                  _row_spec(tm, C), _row_spec(tm, C), _row_spec(tm, D), _row_spec(tm, D),
                  _row_spec(tm, D), _full_spec((1, D)), _weight_spec((D, IN_W)), _row_spec(tm, D)],
        out_specs=[_row_spec(tm, IN_W), _row_spec(tm, D), _full_spec((SUBLANES, D)), _full_spec((SUBLANES, IN_W))],
        out_shape=[jax.ShapeDtypeStruct((T, IN_W), BF16), jax.ShapeDtypeStruct((T, D), F32),
                   jax.ShapeDtypeStruct((SUBLANES, D), F32), jax.ShapeDtypeStruct((SUBLANES, IN_W), F32)],
        compiler_params=_params("arbitrary"),
    )(dq, hi, lo, lo, dglu_a, dglu_b, dga, dgc, x, g, w, dx1)


def _adamw_math(g, w, m, v):
    c1 = 1.0 / (1.0 - ADAM_B1 ** ADAM_STEP)
    c2 = 1.0 / (1.0 - ADAM_B2 ** ADAM_STEP)
    mn = ADAM_B1 * m + (1.0 - ADAM_B1) * g
    vn = ADAM_B2 * v + (1.0 - ADAM_B2) * (g * g)
    return -ADAM_LR * ((mn * c1) / (jnp.sqrt(vn * c2) + ADAM_EPS) + ADAM_WD * w), mn, vn


def _adamw_sharded(parts, w, m, v, name):
    depth, a, b = w.shape
    tr = _tile(a, 256) if a % SUBLANES == 0 else a
    nr = a // tr

    def body(*refs):
        p_refs, (w_ref, m_ref, v_ref, g_ref, d_ref, mo_ref, vo_ref) = refs[:depth], refs[depth:]
        layer = pl.program_id(0)
        for l in range(depth):
            @pl.when(layer == l)
            def _(l=l):
                g = p_refs[l][0].astype(F32)
                for s in range(1, N_DEV):
                    g = g + p_refs[l][s].astype(F32)
                g_ref[...] = g
                d_ref[...], mo_ref[...], vo_ref[...] = _adamw_math(g, w_ref[...], m_ref[...], v_ref[...])

    def part_spec(l):
        return pl.BlockSpec((N_DEV, tr, b),
                            lambda k, i: (0, jnp.where(k == l, i, jnp.where(k < l, 0, nr - 1)), 0))

    spec = pl.BlockSpec((None, tr, b), lambda k, i: (k, i, 0))
    out = jax.ShapeDtypeStruct((depth, a, b), F32)
    return pl.pallas_call(
        body, name=name, grid=(depth, nr),
        in_specs=[part_spec(l) for l in range(depth)] + [spec] * 3,
        out_specs=[spec] * 4, out_shape=[out] * 4,
        compiler_params=_params("arbitrary", "arbitrary"),
    )(*parts, w, m, v)


def _adamw_small(parts, w, m, v):
    R, N = w.shape

    def body(p_ref, w_ref, m_ref, v_ref, g_ref, d_ref, mo_ref, vo_ref):
        g = p_ref[0]
        for s in range(1, N_DEV):
            g = g + p_ref[s]
        g_ref[...] = g
        d_ref[...], mo_ref[...], vo_ref[...] = _adamw_math(g, w_ref[...], m_ref[...], v_ref[...])

    out = jax.ShapeDtypeStruct((R, N), F32)
    return pl.pallas_call(
        body, name="adamw_small", grid=(1,),
        in_specs=[_full_spec((N_DEV, R, N))] + [_full_spec((R, N))] * 3,
        out_specs=[_full_spec((R, N))] * 4, out_shape=[out] * 4,
        compiler_params=_params("arbitrary"),
    )(parts, w, m, v)


_SHARDED = ("w_in", "conv_w", "w_attn_proj", "w_conv_proj", "w_out", "w_mlp1", "w_mlp2")
_ROW_SHARDED = ("w_out", "w_mlp2")
_FIRST = ("w_in", "conv_w")
_REST = tuple(n for n in _SHARDED if n not in _FIRST)
_SMALL = ("mix_norm_g", "b_in", "sinks", "conv_b", "conv_ln_g", "conv_ln_b", "b_conv_proj", "mlp_norm_g",
          "final_norm_g")
_ORDER = ("mix_norm_g", "w_in", "b_in", "sinks", "conv_w", "conv_b", "conv_ln_g", "conv_ln_b", "w_attn_proj",
          "w_conv_proj", "b_conv_proj", "w_out", "mlp_norm_g", "w_mlp1", "w_mlp2", "final_norm_g")
_PACK = 1024


def _full_weight(name, gathered):
    _, a, b = gathered.shape
    if name in _ROW_SHARDED:
        return gathered.reshape(N_DEV * a, b)
    return gathered.transpose(1, 0, 2).reshape(a, N_DEV * b)


def _pack(arrs):
    flat = []
    for a in arrs:
        a = a.reshape(-1)
        flat.append(jnp.pad(a, (0, -a.size % _PACK)))
    return jnp.concatenate(flat).reshape(-1, BLOCK)


def _unpack(packed, shapes):
    flat = packed.reshape(-1)
    out, off = [], 0
    for s in shapes:
        n = math.prod(s)
        out.append(flat[off:off + n].reshape(s))
        off += n + (-n % _PACK)
    return out


def _layer_fwd(x, lw, own_rest=None, comm=None):
    (h, qkv, rest), got = _inproj_fwd(x, lw["mix_norm_g"], lw["w_in"], lw["b_in"], own_rest)
    if got is not None:
        lw.update({n: _full_weight(n, a) for n, a in zip(_REST, got)})
    attn = _attn_fwd(qkv, lw["sinks"])
    u0, yc, u = _conv_fwd(rest, lw["conv_w"], lw["conv_b"], lw["conv_ln_g"], lw["conv_ln_b"])
    merged, x1 = _merge_fwd(attn, u, rest, x, lw["w_attn_proj"], lw["w_conv_proj"], lw["b_conv_proj"], lw["w_out"])
    (h2, z, x2), gathered = _mlp_fwd(x1, lw["mlp_norm_g"], lw["w_mlp1"], lw["w_mlp2"], comm)
    saved = dict(x=x, h=h, qkv=qkv, rest=rest, attn=attn, u0=u0, yc=yc, u=u, merged=merged, x1=x1, h2=h2, z=z)
    return x2, saved, gathered


_EARLY = ("w_mlp1", "w_mlp2")
_MIDDLE = ("w_out", "w_attn_proj", "w_conv_proj")
_LATE = ("w_in", "conv_w")


def _layer_bwd(dx2, lw, s, late_blocks=None):
    g, recv = {}, {}
    late = None if late_blocks is None else _Exchange(late_blocks)
    (dx1, dz, dg2), late_recv = _mlp_bwd(dx2, s["x1"], s["z"], lw["mlp_norm_g"], lw["w_mlp1"], lw["w_mlp2"], late)
    g["mlp_norm_g"] = jnp.sum(dg2, axis=0)
    early = [_tn_blocks(s["h2"], dz, "dw_mlp1", True), _tn_blocks(s["z"], dx2, "dw_mlp2", False, relu_sq=True)]
    dattn, du, dga, dgc, dbra, dbrc, dbc = _merge_bwd(
        dx1, s["attn"], s["u"], s["rest"], lw["w_attn_proj"], lw["w_conv_proj"], lw["b_conv_proj"], lw["w_out"])
    g["b_conv_proj"] = jnp.sum(dbc, axis=0)
    middle = [_tn_blocks(s["merged"], dx1, "dw_out", False), _tn_blocks(s["attn"], dbra, "dw_attn_proj", True),
              _tn_blocks(s["u"], dbrc, "dw_conv_proj", True)]
    dyc, dlg, dlb = _conv_bwd_norm(du, s["yc"], lw["conv_ln_g"], lw["conv_ln_b"])
    g["conv_ln_g"] = jnp.sum(dlg, axis=0)
    g["conv_ln_b"] = jnp.sum(dlb, axis=0)
    (dglu_a, dglu_b, dcw, dcb), early_recv = _conv_bwd_taps(dyc, s["u0"], s["rest"], lw["conv_w"], _Exchange(early))
    recv.update(zip(_EARLY, early_recv))
    dconv_w = jnp.sum(dcw.reshape(CONV_K, SUBLANES, CONV_C), axis=1)
    g["conv_b"] = jnp.sum(dcb, axis=0)
    (dq, hi, lo, dsk), middle_recv = _attn_bwd(s["qkv"], dattn, lw["sinks"], _Exchange(middle))
    recv.update(zip(_MIDDLE, middle_recv))
    g["sinks"] = -jnp.sum(dsk.reshape(N_Q, SUBLANES * BLOCK), axis=1)
    dproj, dx, dg1, dbin = _inproj_bwd(dq, hi, lo, dglu_a, dglu_b, dga, dgc, s["x"], lw["mix_norm_g"], lw["w_in"],
                                       dx1)
    g["mix_norm_g"] = jnp.sum(dg1, axis=0)
    g["b_in"] = jnp.sum(dbin, axis=0)
    own_late = [_tn_blocks(s["h"], dproj, "dw_in", True),
                dconv_w.reshape(CONV_K, N_DEV, CONV_C // N_DEV).transpose(1, 0, 2)]
    return dx, g, recv, late_recv, own_late


def kernel(x, mix_norm_g, w_in, b_in, sinks, conv_w, conv_b, conv_ln_g, conv_ln_b, w_attn_proj, w_conv_proj, b_conv_proj, w_out, mlp_norm_g, w_mlp1, w_mlp2, final_norm_g, loss_target, m_mix_norm_g, m_w_in, m_b_in, m_sinks, m_conv_w, m_conv_b, m_conv_ln_g, m_conv_ln_b, m_w_attn_proj, m_w_conv_proj, m_b_conv_proj, m_w_out, m_mlp_norm_g, m_w_mlp1, m_w_mlp2, m_final_norm_g, v_mix_norm_g, v_w_in, v_b_in, v_sinks, v_conv_w, v_conv_b, v_conv_ln_g, v_conv_ln_b, v_w_attn_proj, v_w_conv_proj, v_b_conv_proj, v_w_out, v_mlp_norm_g, v_w_mlp1, v_w_mlp2, v_final_norm_g):
    w = dict(mix_norm_g=mix_norm_g, w_in=w_in, b_in=b_in, sinks=sinks, conv_w=conv_w, conv_b=conv_b,
             conv_ln_g=conv_ln_g, conv_ln_b=conv_ln_b, w_attn_proj=w_attn_proj, w_conv_proj=w_conv_proj,
             b_conv_proj=b_conv_proj, w_out=w_out, mlp_norm_g=mlp_norm_g, w_mlp1=w_mlp1, w_mlp2=w_mlp2,
             final_norm_g=final_norm_g)
    m = dict(mix_norm_g=m_mix_norm_g, w_in=m_w_in, b_in=m_b_in, sinks=m_sinks, conv_w=m_conv_w, conv_b=m_conv_b,
             conv_ln_g=m_conv_ln_g, conv_ln_b=m_conv_ln_b, w_attn_proj=m_w_attn_proj, w_conv_proj=m_w_conv_proj,
             b_conv_proj=m_b_conv_proj, w_out=m_w_out, mlp_norm_g=m_mlp_norm_g, w_mlp1=m_w_mlp1, w_mlp2=m_w_mlp2,
             final_norm_g=m_final_norm_g)
    v = dict(mix_norm_g=v_mix_norm_g, w_in=v_w_in, b_in=v_b_in, sinks=v_sinks, conv_w=v_conv_w, conv_b=v_conv_b,
             conv_ln_g=v_conv_ln_g, conv_ln_b=v_conv_ln_b, w_attn_proj=v_w_attn_proj, w_conv_proj=v_w_conv_proj,
             b_conv_proj=v_b_conv_proj, w_out=v_w_out, mlp_norm_g=v_mlp_norm_g, w_mlp1=v_w_mlp1, w_mlp2=v_w_mlp2,
             final_norm_g=v_final_norm_g)
    T = x.shape[1]
    xs = x.reshape(T, D_MODEL)
    target = loss_target.reshape(T, D_MODEL)

    def gather_of(l, names):
        return _Gather([w[n][l] if n == "conv_w" else w[n][l].astype(BF16) for n in names])

    def layer_weights(l, names, gathered):
        lw = {n: _full_weight(n, a) for n, a in zip(names, gathered)}
        for n in _SMALL:
            if n != "final_norm_g":
                lw[n] = w[n][l] if n == "sinks" else w[n][l].reshape(1, -1)
        return lw

    acts = xs
    saved, weights = [], []
    for l in range(DEPTH):
        following = gather_of(l + 1, _SHARDED) if l + 1 < DEPTH else None
        if l == 0:
            lw = layer_weights(0, _FIRST, _run_comm(gather_of(0, _FIRST), "gather_first"))
            acts, s, gathered = _layer_fwd(acts, lw, gather_of(0, _REST), following)
        else:
            lw = layer_weights(l, _SHARDED, gathered)
            acts, s, gathered = _layer_fwd(acts, lw, None, following)
        weights.append(lw)
        saved.append(s)
    lterms, dx, dgf = _final_loss(acts, final_norm_g.reshape(1, -1), target)
    grads, received = [None] * DEPTH, [None] * DEPTH
    late = None
    for l in reversed(range(DEPTH)):
        dx, grads[l], received[l], late_recv, late = _layer_bwd(dx, weights[l], saved[l], late)
        if late_recv is not None:
            received[l + 1].update(zip(_LATE, late_recv))
    received[0].update(zip(_LATE, _run_comm(_Exchange(late), "scatter_late")))
    grad = {n: jnp.stack([grads[l][n] for l in range(DEPTH)]) for n in _SMALL if n != "final_norm_g"}
    grad["final_norm_g"] = jnp.sum(dgf, axis=0)

    small_shapes = [w[n].shape for n in _SMALL] + [(1,)]
    small = _pack([grad[n] for n in _SMALL] + [jnp.sum(lterms).reshape(1)])
    small_parts = _run_comm(_Gather([small]), "gather_small")[0]

    out_g, out_d, out_m, out_v = {}, {}, {}, {}
    for n in _SHARDED:
        out_g[n], out_d[n], out_m[n], out_v[n] = _adamw_sharded(
            [received[l][n] for l in range(DEPTH)], w[n], m[n], v[n], "adamw_" + n)
    zero = jnp.zeros((1,), F32)
    res = _adamw_small(small_parts, _pack([w[n] for n in _SMALL] + [zero]), _pack([m[n] for n in _SMALL] + [zero]),
                       _pack([v[n] for n in _SMALL] + [zero]))
    unpacked = [_unpack(r, small_shapes) for r in res]
    for i, n in enumerate(_SMALL):
        out_g[n], out_d[n], out_m[n], out_v[n] = (u[i] for u in unpacked)
    loss = unpacked[0][-1].reshape(())
    return (loss, dx.reshape(x.shape), *[out_g[n] for n in _ORDER], *[out_d[n] for n in _ORDER],
            *[out_m[n] for n in _ORDER], *[out_v[n] for n in _ORDER])
```

```python
import functools
import math

import jax
import jax.numpy as jnp
from jax import lax
from jax.experimental import pallas as pl
from jax.experimental.pallas import tpu as pltpu

D_MODEL = 1024
SEQ = 16384
DEPTH = 2
N_Q = 8
N_KV = 2
GROUP = N_Q // N_KV
HEAD_DIM = 64
ATTN_W = N_Q * HEAD_DIM
KV_W = N_KV * HEAD_DIM
BLOCK = 128
CONV_C = D_MODEL // 2
CONV_K = 31
D_FF = 4 * D_MODEL
QKV_W = ATTN_W + 2 * KV_W
IN_W = QKV_W + 2 * CONV_C + 2 * D_MODEL
EPS = 1e-6
NEG = -1e30
N_DEV = 8

ADAM_LR = 0.001
ADAM_B1 = 0.9
ADAM_B2 = 0.999
ADAM_EPS = 1e-08
ADAM_WD = 0.01
ADAM_STEP = 10

F32 = jnp.float32
BF16 = jnp.bfloat16
MESH = pl.DeviceIdType.MESH

SUBLANES = 8
HALO = 32
FF_CHUNK = 1024
CONV_ROWS = 16
VMEM_LIMIT = 52 * 1024 * 1024

_NT = (((1,), (1,)), ((), ()))
_TN = (((0,), (0,)), ((), ()))


def _params(*sem):
    return pltpu.CompilerParams(dimension_semantics=sem, vmem_limit_bytes=VMEM_LIMIT)


def _tile(n, pref):
    t = min(n, pref)
    assert n % t == 0, (n, t)
    return t


def _sigmoid(v):
    return 1.0 / (1.0 + jnp.exp(-v))


def _rows8(v):
    r, n = v.shape
    return jnp.sum(v.reshape(r // SUBLANES, SUBLANES, n), axis=0)


def _dot(a, b):
    return jnp.dot(a, b, preferred_element_type=F32)


def _dot_nt(a, b):
    return lax.dot_general(a, b, _NT, preferred_element_type=F32)


def _dot_tn(a, b):
    return lax.dot_general(a, b, _TN, preferred_element_type=F32)


def _rms_bwd(xv, g, dh):
    r = lax.rsqrt(jnp.mean(xv * xv, axis=-1, keepdims=True) + EPS)
    xhat = xv * r
    dxhat = dh * g
    dx = r * (dxhat - xhat * jnp.mean(dxhat * xhat, axis=-1, keepdims=True))
    return dx, dh * xhat


def _row_spec(tm, n, col=0):
    return pl.BlockSpec((tm, n), lambda i: (i, col))


def _full_spec(shape):
    return pl.BlockSpec(shape, lambda *_: (0,) * len(shape))


def _weight_spec(shape):
    return pl.BlockSpec(shape, lambda *_: (0,) * len(shape), pipeline_mode=pl.Buffered(1))


def _mesh_pos():
    return lax.axis_index("x"), lax.axis_index("y"), lax.axis_index("c")


def _dev_index(dev):
    return 4 * dev[0] + 2 * dev[1] + dev[2]


class _Exchange:
    middle_at = None

    def __init__(self, arrs):
        self.arrays = list(arrs)

    def out_shape(self):
        return [jax.ShapeDtypeStruct(a.shape, a.dtype) for a in self.arrays]

    def scratch(self):
        n = len(self.arrays)
        return [pltpu.SemaphoreType.DMA((7 * n,)), pltpu.SemaphoreType.DMA((7 * n,)), pltpu.SemaphoreType.DMA((n,))]

    def _copies(self, ins, outs, sems):
        send_sems, recv_sems, local_sems = sems
        x, y, c = _mesh_pos()
        me = _dev_index((x, y, c))
        mine, sends, arrivals = [], [], []
        for p in range(len(self.arrays)):
            mine.append(pltpu.make_async_copy(ins[p].at[me], outs[p].at[me], local_sems.at[p]))
            for k in range(1, N_DEV):
                peer = (1 - x if k & 4 else x, 1 - y if k & 2 else y, 1 - c if k & 1 else c)
                pid = _dev_index(peer)
                pair = dict(send_sem=send_sems.at[7 * p + k - 1], recv_sem=recv_sems.at[7 * p + k - 1],
                            device_id=peer, device_id_type=MESH)
                sends.append(pltpu.make_async_remote_copy(src_ref=ins[p].at[pid], dst_ref=outs[p].at[me], **pair))
                arrivals.append(pltpu.make_async_remote_copy(src_ref=ins[p].at[pid], dst_ref=outs[p].at[pid], **pair))
        return mine, sends, arrivals

    def start(self, ins, outs, sems):
        mine, sends, _ = self._copies(ins, outs, sems)
        for cp in mine + sends:
            cp.start()

    def finish(self, ins, outs, sems):
        mine, sends, arrivals = self._copies(ins, outs, sems)
        for cp in arrivals:
            cp.wait_recv()
        for cp in sends:
            cp.wait_send()
        for cp in mine:
            cp.wait()


class _Gather:
    middle_at = 0.75

    def __init__(self, arrs):
        self.arrays = list(arrs)

    def out_shape(self):
        return [jax.ShapeDtypeStruct((N_DEV,) + a.shape, a.dtype) for a in self.arrays]

    def scratch(self):
        n = len(self.arrays)
        return [pltpu.SemaphoreType.DMA((7 * n,)), pltpu.SemaphoreType.DMA((7 * n,)), pltpu.SemaphoreType.DMA((n,))]

    def _copies(self, ins, outs, sems):
        send_sems, recv_sems, local_sems = sems
        x, y, c = _mesh_pos()
        me, sibling = (x, y, c), (x, y, 1 - c)
        chips = [(1 - x, y), (x, 1 - y), (1 - x, 1 - y)]
        n = len(self.arrays)

        def copy(p, k, dev, to, src=None):
            block = outs[p].at[_dev_index(dev)]
            return pltpu.make_async_remote_copy(
                src_ref=block if src is None else src, dst_ref=block,
                send_sem=send_sems.at[7 * p + k], recv_sem=recv_sems.at[7 * p + k],
                device_id=to, device_id_type=MESH)

        cp = dict(mine=[pltpu.make_async_copy(ins[p], outs[p].at[_dev_index(me)], local_sems.at[p])
                        for p in range(n)])
        cp["first"] = [copy(p, 0, me, sibling, src=ins[p]) for p in range(n)]
        cp["first"] += [copy(p, 1 + j, me, (*chip, c), src=ins[p]) for p in range(n) for j, chip in enumerate(chips)]
        cp["over_ici"] = [copy(p, 1 + j, (*chip, c), me) for j, chip in enumerate(chips) for p in range(n)]
        cp["passed"] = [copy(p, 4 + j, (*chip, c), sibling) for j, chip in enumerate(chips) for p in range(n)]
        cp["from_sibling"] = [copy(p, 0, sibling, me) for p in range(n)]
        cp["from_sibling"] += [copy(p, 4 + j, (*chip, 1 - c), me) for j, chip in enumerate(chips) for p in range(n)]
        return cp

    def start(self, ins, outs, sems):
        cp = self._copies(ins, outs, sems)
        for d in cp["mine"] + cp["first"]:
            d.start()

    def middle(self, ins, outs, sems):
        cp = self._copies(ins, outs, sems)
        for arrived, onward in zip(cp["over_ici"], cp["passed"]):
            arrived.wait_recv()
            onward.start()

    def finish(self, ins, outs, sems):
        cp = self._copies(ins, outs, sems)
        for d in cp["from_sibling"]:
            d.wait_recv()
        for d in cp["first"] + cp["passed"]:
            d.wait_send()
        for d in cp["mine"]:
            d.wait()


def _run_comm(comm, name):
    n = len(comm.arrays)

    def body(*refs):
        ins, outs, sems = refs[:n], refs[n:2 * n], refs[2 * n:]
        comm.start(ins, outs, sems)
        if comm.middle_at is not None:
            comm.middle(ins, outs, sems)
        comm.finish(ins, outs, sems)

    any_spec = pl.BlockSpec(memory_space=pl.ANY)
    return pl.pallas_call(
        body, name=name, in_specs=[any_spec] * n, out_specs=[any_spec] * n, out_shape=comm.out_shape(),
        scratch_shapes=comm.scratch(),
    )(*comm.arrays)


def _pallas(body, *, name, grid, in_specs, out_specs, out_shape, args, sem, scratch_shapes=(), comm=None):
    if comm is None:
        outs = pl.pallas_call(
            body, name=name, grid=grid, in_specs=in_specs, out_specs=out_specs, out_shape=out_shape,
            scratch_shapes=list(scratch_shapes), compiler_params=_params(*sem),
        )(*args)
        return outs, None
    n_in, n_out, n_scr, n_c = len(in_specs), len(out_specs), len(scratch_shapes), len(comm.arrays)
    steps = grid[0]
    middle = None if comm.middle_at is None else min(steps - 1, int(steps * comm.middle_at))

    def carried(*refs):
        ins, refs = refs[:n_in], refs[n_in:]
        cins, refs = refs[:n_c], refs[n_c:]
        outs, refs = refs[:n_out], refs[n_out:]
        couts, refs = refs[:n_c], refs[n_c:]
        scr, csems = refs[:n_scr], refs[n_scr:]
        step = pl.program_id(0)

        @pl.when(step == 0)
        def _():
            comm.start(cins, couts, csems)

        body(*ins, *outs, *scr)

        if middle is not None:
            @pl.when(step == middle)
            def _():
                comm.middle(cins, couts, csems)

        @pl.when(step == steps - 1)
        def _():
            comm.finish(cins, couts, csems)

    any_spec = pl.BlockSpec(memory_space=pl.ANY)
    res = pl.pallas_call(
        carried, name=name, grid=grid,
        in_specs=list(in_specs) + [any_spec] * n_c, out_specs=list(out_specs) + [any_spec] * n_c,
        out_shape=list(out_shape) + comm.out_shape(),
        scratch_shapes=list(scratch_shapes) + comm.scratch(),
        compiler_params=_params(*(("arbitrary",) + tuple(sem[1:]))),
    )(*args, *comm.arrays)
    return res[:n_out], res[n_out:]


def _inproj_fwd(x, g, w, b, comm=None):
    T, D = x.shape
    rest_w = IN_W - QKV_W
    tm = _tile(T, 512)

    def body(x_ref, g_ref, w_ref, b_ref, h_ref, qkv_ref, rest_ref):
        xv = x_ref[...]
        r = lax.rsqrt(jnp.mean(xv * xv, axis=-1, keepdims=True) + EPS)
        h = (xv * r * g_ref[...]).astype(BF16)
        h_ref[...] = h
        qkv_ref[...] = (_dot(h, w_ref[:, :QKV_W]) + b_ref[:, :QKV_W]).astype(BF16)
        rest_ref[...] = (_dot(h, w_ref[:, QKV_W:]) + b_ref[:, QKV_W:]).astype(BF16)

    return _pallas(
        body, name="inproj_fwd", grid=(T // tm,),
        in_specs=[_row_spec(tm, D), _full_spec((1, D)), _weight_spec((D, IN_W)), _full_spec((1, IN_W))],
        out_specs=[_row_spec(tm, D), _row_spec(tm, QKV_W), _row_spec(tm, rest_w)],
        out_shape=[jax.ShapeDtypeStruct((T, D), BF16), jax.ShapeDtypeStruct((T, QKV_W), BF16),
                   jax.ShapeDtypeStruct((T, rest_w), BF16)],
        args=(x, g, w, b), sem=("parallel",), comm=comm)


def _fold_masks(first):
    row = lax.broadcasted_iota(jnp.int32, (BLOCK, BLOCK), 0)
    col = lax.broadcasted_iota(jnp.int32, (BLOCK, BLOCK), 1)
    upper = col > row
    dist = jnp.where(upper, row + BLOCK - col, row - col)
    keep = col <= row + jnp.where(first, 0, BLOCK)
    return upper, dist.astype(F32), keep


def _fold(band, upper):
    return jnp.where(upper, band[:, :BLOCK], band[:, BLOCK:])


def _unfold(folded, upper):
    return jnp.concatenate([jnp.where(upper, folded, 0.0), jnp.where(upper, 0.0, folded)], axis=1)


def _fill_kv(kv_buf, cur_ref, prev_ref):
    scale = 1.0 / math.sqrt(HEAD_DIM)
    assert math.frexp(scale)[0] == 0.5
    for r0, ref in ((0, prev_ref), (BLOCK, cur_ref)):
        rows = ref.shape[0]
        kv_buf[r0:r0 + rows, :KV_W] = ref[:, ATTN_W:ATTN_W + KV_W] * scale
        kv_buf[r0:r0 + rows, KV_W:] = ref[:, ATTN_W + KV_W:]


def _group_rows(x, kh):
    return jnp.concatenate([x[:, h * HEAD_DIM:(h + 1) * HEAD_DIM] for h in range(kh * GROUP, (kh + 1) * GROUP)],
                           axis=0)


def _attn_probs(scores, h, sink, upper, distf, keep):
    s = _fold(scores, upper) - (2.0 ** (-8.0 * (h + 1) / N_Q)) * distf
    if keep is not None:
        s = jnp.where(keep, s, NEG)
    m = jnp.maximum(jnp.max(s, axis=-1, keepdims=True), sink)
    p = jnp.exp(s - m)
    e = jnp.exp(sink - m)
    inv = 1.0 / (jnp.sum(p, axis=-1, keepdims=True) + e)
    return p * inv, e * inv


def _attn_fwd(qkv, sinks):
    T = qkv.shape[0]
    tq = _tile(T, 512)
    nblk = tq // BLOCK

    def body(sink_ref, cur_ref, prev_ref, o_ref, kv_buf):
        _fill_kv(kv_buf, cur_ref, prev_ref)
        upper, distf, keep = _fold_masks(pl.program_id(0) == 0)
        units = [(j, kh) for j in range(nblk) for kh in range(N_KV)]

        def key_band(j, kh):
            return kv_buf[j * BLOCK:(j + 2) * BLOCK, kh * HEAD_DIM:(kh + 1) * HEAD_DIM]

        def value_band(j, kh):
            return kv_buf[j * BLOCK:(j + 2) * BLOCK, KV_W + kh * HEAD_DIM:KV_W + (kh + 1) * HEAD_DIM]

        scores = {(j, kh): _dot_nt(_group_rows(cur_ref[j * BLOCK:(j + 1) * BLOCK, :ATTN_W], kh), key_band(j, kh))
                  for j, kh in units}
        probs = {}
        for j, kh in units:
            ps = []
            for g in range(GROUP):
                h = kh * GROUP + g
                p, _ = _attn_probs(scores[j, kh][g * BLOCK:(g + 1) * BLOCK, :], h, sink_ref[h], upper, distf,
                                   keep if j == 0 else None)
                ps.append(_unfold(p, upper).astype(BF16))
            probs[j, kh] = jnp.concatenate(ps, axis=0)
        outs = {u: _dot(probs[u], value_band(*u)) for u in units}
        for j in range(nblk):
            heads = [outs[j, kh][g * BLOCK:(g + 1) * BLOCK, :] for kh in range(N_KV) for g in range(GROUP)]
            o_ref[j * BLOCK:(j + 1) * BLOCK, :] = jnp.concatenate(heads, axis=1).astype(BF16)

    return pl.pallas_call(
        body, name="attn_fwd", grid=(T // tq,),
        in_specs=[pl.BlockSpec(memory_space=pltpu.SMEM),
                  _row_spec(tq, QKV_W),
                  pl.BlockSpec((BLOCK, QKV_W), lambda i: (jnp.maximum(i * nblk - 1, 0), 0))],
        out_specs=_row_spec(tq, ATTN_W),
        out_shape=jax.ShapeDtypeStruct((T, ATTN_W), BF16),
        scratch_shapes=[pltpu.VMEM((tq + BLOCK, 2 * KV_W), BF16)],
        compiler_params=_params("parallel"),
    )(sinks, qkv, qkv)


def _shifted_copies(buf):
    n = buf.shape[1] - SUBLANES
    for s in range(1, SUBLANES):
        buf[s, 0:n, :] = buf[0, s:s + n, :]


def _shifted_rows(buf, start, rows):
    s = start % SUBLANES
    return buf[s, start - s:start - s + rows, :]


def _conv_fwd(rest, cw, cb, lg, lb):
    T = rest.shape[0]
    C = CONV_C
    tm = _tile(T, 256)
    R = _tile(tm, CONV_ROWS)
    per = tm // HALO

    def body(cur_ref, prev_ref, w_ref, cb_ref, g_ref, b_ref, u0_ref, yc_ref, u_ref, ubuf):
        i = pl.program_id(0)
        up = prev_ref[:, :C].astype(F32) * _sigmoid(prev_ref[:, C:].astype(F32))
        ubuf[0, 0:HALO, :] = jnp.where(i > 0, up, 0.0)
        u0 = cur_ref[:, :C].astype(F32) * _sigmoid(cur_ref[:, C:].astype(F32))
        ubuf[0, HALO:, :] = u0
        u0_ref[...] = u0
        _shifted_copies(ubuf)
        off = HALO - (CONV_K - 1)
        for c in range(tm // R):
            acc = jnp.broadcast_to(cb_ref[...], (R, C))
            for j in range(CONV_K):
                acc = acc + w_ref[j:j + 1, :] * _shifted_rows(ubuf, c * R + off + j, R)
            yc_ref[c * R:(c + 1) * R, :] = acc
            xc = acc - jnp.mean(acc, axis=-1, keepdims=True)
            ln = xc * lax.rsqrt(jnp.mean(xc * xc, axis=-1, keepdims=True) + EPS) * g_ref[...] + b_ref[...]
            u_ref[c * R:(c + 1) * R, :] = (ln * _sigmoid(ln)).astype(BF16)

    return pl.pallas_call(
        body, name="conv_fwd", grid=(T // tm,),
        in_specs=[_row_spec(tm, 2 * C),
                  pl.BlockSpec((HALO, 2 * C), lambda i: (jnp.maximum(i * per - 1, 0), 0)),
                  _full_spec((CONV_K, C)), _full_spec((1, C)), _full_spec((1, C)), _full_spec((1, C))],
        out_specs=[_row_spec(tm, C), _row_spec(tm, C), _row_spec(tm, C)],
        out_shape=[jax.ShapeDtypeStruct((T, C), F32), jax.ShapeDtypeStruct((T, C), F32),
                   jax.ShapeDtypeStruct((T, C), BF16)],
        scratch_shapes=[pltpu.VMEM((SUBLANES, tm + HALO, C), F32)],
        compiler_params=_params("parallel"),
    )(rest, rest, cw, cb, lg, lb)


def _merge_fwd(attn, u, rest, x, wa, wc, bc, wo):
    T, D = x.shape
    tm = _tile(T, 512)
    gcol = 2 * CONV_C // D

    def body(attn_ref, u_ref, ga_ref, gc_ref, x_ref, wa_ref, wc_ref, bc_ref, wo_ref, m_ref, x1_ref):
        bra = _dot(attn_ref[...], wa_ref[...])
        brc = _dot(u_ref[...], wc_ref[...]) + bc_ref[...]
        mb = (_sigmoid(ga_ref[...].astype(F32)) * bra + _sigmoid(gc_ref[...].astype(F32)) * brc).astype(BF16)
        m_ref[...] = mb
        x1_ref[...] = x_ref[...] + _dot(mb, wo_ref[...])

    return pl.pallas_call(
        body, name="merge_fwd", grid=(T // tm,),
        in_specs=[_row_spec(tm, ATTN_W), _row_spec(tm, CONV_C), _row_spec(tm, D, gcol), _row_spec(tm, D, gcol + 1),
                  _row_spec(tm, D), _weight_spec((ATTN_W, D)), _weight_spec((CONV_C, D)), _full_spec((1, D)),
                  _weight_spec((D, D))],
        out_specs=[_row_spec(tm, D), _row_spec(tm, D)],
        out_shape=[jax.ShapeDtypeStruct((T, D), BF16), jax.ShapeDtypeStruct((T, D), F32)],
        compiler_params=_params("parallel"),
    )(attn, u, rest, rest, x, wa, wc, bc, wo)


def _mlp_fwd(x1, g, w1, w2, comm=None):
    T, D = x1.shape
    tm = _tile(T, 512)
    fc = _tile(D_FF, FF_CHUNK)

    def body(x_ref, g_ref, w1_ref, w2_ref, h_ref, z_ref, o_ref):
        xv = x_ref[...]
        r = lax.rsqrt(jnp.mean(xv * xv, axis=-1, keepdims=True) + EPS)
        h = (xv * r * g_ref[...]).astype(BF16)
        h_ref[...] = h
        acc = xv
        for c in range(D_FF // fc):
            cols = slice(c * fc, (c + 1) * fc)
            z = _dot(h, w1_ref[:, cols])
            z_ref[:, cols] = z.astype(BF16)
            acc = acc + _dot(jnp.square(jnp.maximum(z, 0.0)).astype(BF16), w2_ref[cols, :])
        o_ref[...] = acc

    return _pallas(
        body, name="mlp_fwd", grid=(T // tm,),
        in_specs=[_row_spec(tm, D), _full_spec((1, D)), _weight_spec((D, D_FF)), _weight_spec((D_FF, D))],
        out_specs=[_row_spec(tm, D), _row_spec(tm, D_FF), _row_spec(tm, D)],
        out_shape=[jax.ShapeDtypeStruct((T, D), BF16), jax.ShapeDtypeStruct((T, D_FF), BF16),
                   jax.ShapeDtypeStruct((T, D), F32)],
        args=(x1, g, w1, w2), sem=("parallel",), comm=comm)


def _final_loss(x, g, target):
    T, D = x.shape
    tm = _tile(T, 512)

    def body(x_ref, g_ref, t_ref, l_ref, dx_ref, dg_ref):
        @pl.when(pl.program_id(0) == 0)
        def _():
            l_ref[...] = jnp.zeros_like(l_ref)
            dg_ref[...] = jnp.zeros_like(dg_ref)

        xv = x_ref[...]
        r = lax.rsqrt(jnp.mean(xv * xv, axis=-1, keepdims=True) + EPS)
        e = xv * r * g_ref[...] - t_ref[...]
        l_ref[...] += _rows8(e * e) * (0.5 / D)
        dx, dg = _rms_bwd(xv, g_ref[...], e * (1.0 / D))
        dx_ref[...] = dx
        dg_ref[...] += _rows8(dg)

    return pl.pallas_call(
        body, name="final_loss", grid=(T // tm,),
        in_specs=[_row_spec(tm, D), _full_spec((1, D)), _row_spec(tm, D)],
        out_specs=[_full_spec((SUBLANES, D)), _row_spec(tm, D), _full_spec((SUBLANES, D))],
        out_shape=[jax.ShapeDtypeStruct((SUBLANES, D), F32), jax.ShapeDtypeStruct((T, D), F32),
                   jax.ShapeDtypeStruct((SUBLANES, D), F32)],
        compiler_params=_params("arbitrary"),
    )(x, g, target)


def _mlp_bwd(dx2, x1, z, g, w1, w2, comm=None):
    T, D = x1.shape
    tm = _tile(T, 512)
    fc = _tile(D_FF, FF_CHUNK)

    def body(dx2_ref, x_ref, z_ref, g_ref, w1_ref, w2_ref, dx1_ref, dz_ref, dg_ref):
        @pl.when(pl.program_id(0) == 0)
        def _():
            dg_ref[...] = jnp.zeros_like(dg_ref)

        dxo = dx2_ref[...]
        dxb = dxo.astype(BF16)
        dh = jnp.zeros((tm, D), F32)
        for c in range(D_FF // fc):
            cols = slice(c * fc, (c + 1) * fc)
            da = _dot_nt(dxb, w2_ref[cols, :])
            dz = (da * (2.0 * jnp.maximum(z_ref[:, cols].astype(F32), 0.0))).astype(BF16)
            dz_ref[:, cols] = dz
            dh = dh + _dot_nt(dz, w1_ref[:, cols])
        dx, dg = _rms_bwd(x_ref[...], g_ref[...], dh)
        dx1_ref[...] = dxo + dx
        dg_ref[...] += _rows8(dg)

    return _pallas(
        body, name="mlp_bwd", grid=(T // tm,),
        in_specs=[_row_spec(tm, D), _row_spec(tm, D), _row_spec(tm, D_FF), _full_spec((1, D)),
                  _weight_spec((D, D_FF)), _weight_spec((D_FF, D))],
        out_specs=[_row_spec(tm, D), _row_spec(tm, D_FF), _full_spec((SUBLANES, D))],
        out_shape=[jax.ShapeDtypeStruct((T, D), F32), jax.ShapeDtypeStruct((T, D_FF), BF16),
                   jax.ShapeDtypeStruct((SUBLANES, D), F32)],
        args=(dx2, x1, z, g, w1, w2), sem=("arbitrary",), comm=comm)


def _tn_blocks(a, b, name, col_sharded, relu_sq=False):
    T, M = a.shape
    N = b.shape[1]
    tk = _tile(T, 1024)
    tm = _tile(M, 512 if col_sharded else 1024)
    nb = N // N_DEV
    last = T // tk - 1

    def body(a_ref, b_ref, o_ref, acc_ref):
        k = pl.program_id(1)

        @pl.when(k == 0)
        def _():
            acc_ref[...] = jnp.zeros_like(acc_ref)

        av = a_ref[...]
        if relu_sq:
            av = jnp.square(jnp.maximum(av, 0.0))
        acc_ref[...] += _dot_tn(av.astype(BF16), b_ref[...].astype(BF16))

        @pl.when(k == last)
        def _():
            if col_sharded:
                for d in range(N_DEV):
                    o_ref[d] = acc_ref[:, d * nb:(d + 1) * nb].astype(BF16)
            else:
                o_ref[...] = acc_ref[...].astype(BF16)

    if col_sharded:
        out_spec = pl.BlockSpec((N_DEV, tm, nb), lambda i, k: (0, i, 0))
        out_shape = jax.ShapeDtypeStruct((N_DEV, M, nb), BF16)
    else:
        out_spec = pl.BlockSpec((tm, N), lambda i, k: (i, 0))
        out_shape = jax.ShapeDtypeStruct((M, N), BF16)
    out = pl.pallas_call(
        body, name=name, grid=(M // tm, T // tk),
        in_specs=[pl.BlockSpec((tk, tm), lambda i, k: (k, i)), pl.BlockSpec((tk, N), lambda i, k: (k, 0))],
        out_specs=out_spec, out_shape=out_shape,
        scratch_shapes=[pltpu.VMEM((tm, N), F32)],
        compiler_params=_params("parallel", "arbitrary"),
    )(a, b)
    return out if col_sharded else out.reshape(N_DEV, M // N_DEV, N)


def _merge_bwd(dx1, attn, u, rest, yc, wa, wc, bc, wo, lg, lb):
    T, D = dx1.shape
    C = CONV_C
    tm = _tile(T, 512)
    gcol = 2 * CONV_C // D

    def body(dx_ref, attn_ref, u_ref, ga_ref, gc_ref, yc_ref, wa_ref, wc_ref, bc_ref, wo_ref, lg_ref, lb_ref,
             dattn_ref, dyc_ref, dga_ref, dgc_ref, dbra_ref, dbrc_ref, dbc_ref, dlg_ref, dlb_ref):
        @pl.when(pl.program_id(0) == 0)
        def _():
            dbc_ref[...] = jnp.zeros_like(dbc_ref)
            dlg_ref[...] = jnp.zeros_like(dlg_ref)
            dlb_ref[...] = jnp.zeros_like(dlb_ref)

        dm = _dot_nt(dx_ref[...].astype(BF16), wo_ref[...])
        bra = _dot(attn_ref[...], wa_ref[...])
        brc = _dot(u_ref[...], wc_ref[...]) + bc_ref[...]
        sa = _sigmoid(ga_ref[...].astype(F32))
        sc = _sigmoid(gc_ref[...].astype(F32))
        dbra = dm * sa
        dbrc = dm * sc
        dga_ref[...] = (dm * bra * sa * (1.0 - sa)).astype(BF16)
        dgc_ref[...] = (dm * brc * sc * (1.0 - sc)).astype(BF16)
        dbra_b = dbra.astype(BF16)
        dbrc_b = dbrc.astype(BF16)
        dbra_ref[...] = dbra_b
        dbrc_ref[...] = dbrc_b
        dbc_ref[...] += _rows8(dbrc)
        dattn_ref[...] = _dot_nt(dbra_b, wa_ref[...]).astype(BF16)
        dyc, dlg, dlb = _swish_norm_bwd(_dot_nt(dbrc_b, wc_ref[...]), yc_ref[...], lg_ref[...], lb_ref[...])
        dyc_ref[...] = dyc
        dlg_ref[...] += _rows8(dlg)
        dlb_ref[...] += _rows8(dlb)

    return pl.pallas_call(
        body, name="merge_bwd", grid=(T // tm,),
        in_specs=[_row_spec(tm, D), _row_spec(tm, ATTN_W), _row_spec(tm, C), _row_spec(tm, D, gcol),
                  _row_spec(tm, D, gcol + 1), _row_spec(tm, C), _weight_spec((ATTN_W, D)), _weight_spec((C, D)),
                  _full_spec((1, D)), _weight_spec((D, D)), _full_spec((1, C)), _full_spec((1, C))],
        out_specs=[_row_spec(tm, ATTN_W), _row_spec(tm, C), _row_spec(tm, D), _row_spec(tm, D),
                   _row_spec(tm, D), _row_spec(tm, D), _full_spec((SUBLANES, D)), _full_spec((SUBLANES, C)),
                   _full_spec((SUBLANES, C))],
        out_shape=[jax.ShapeDtypeStruct((T, ATTN_W), BF16), jax.ShapeDtypeStruct((T, C), F32),
                   jax.ShapeDtypeStruct((T, D), BF16), jax.ShapeDtypeStruct((T, D), BF16),
                   jax.ShapeDtypeStruct((T, D), BF16), jax.ShapeDtypeStruct((T, D), BF16),
                   jax.ShapeDtypeStruct((SUBLANES, D), F32), jax.ShapeDtypeStruct((SUBLANES, C), F32),
                   jax.ShapeDtypeStruct((SUBLANES, C), F32)],
        compiler_params=_params("arbitrary"),
    )(dx1, attn, u, rest, rest, yc, wa, wc, bc, wo, lg, lb)


def _swish_norm_bwd(du, yv, g, b):
    xc = yv - jnp.mean(yv, axis=-1, keepdims=True)
    rstd = lax.rsqrt(jnp.mean(xc * xc, axis=-1, keepdims=True) + EPS)
    xn = xc * rstd
    ln = xn * g + b
    sg = _sigmoid(ln)
    dln = du * sg * (1.0 + ln * (1.0 - sg))
    dxn = dln * g
    dyc = rstd * (dxn - jnp.mean(dxn, axis=-1, keepdims=True) - xn * jnp.mean(dxn * xn, axis=-1, keepdims=True))
    return dyc, dln * xn, dln


def _conv_taps_bwd(first, last, dy_ref, dyn_ref, u0_ref, u0p_ref, glu_ref, w_ref, dglu_ref, dw_ref, db_ref,
                   dbuf, ubuf):
    tm, C = dy_ref.shape
    R = _tile(tm, CONV_ROWS)
    dbuf[0, 0:tm, :] = dy_ref[...]
    dbuf[0, tm:, :] = jnp.where(last, 0.0, dyn_ref[...])
    ubuf[0, 0:HALO, :] = jnp.where(first, 0.0, u0p_ref[...])
    ubuf[0, HALO:, :] = u0_ref[...]
    _shifted_copies(dbuf)
    _shifted_copies(ubuf)
    off = HALO - (CONV_K - 1)
    for c in range(tm // R):
        rows = slice(c * R, (c + 1) * R)
        dy = dbuf[0, rows, :]
        acc = jnp.zeros((R, C), F32)
        for j in range(CONV_K):
            acc = acc + w_ref[j:j + 1, :] * _shifted_rows(dbuf, c * R + CONV_K - 1 - j, R)
            dw_ref[j * SUBLANES:(j + 1) * SUBLANES, :] += _rows8(dy * _shifted_rows(ubuf, c * R + off + j, R))
        db_ref[...] += _rows8(dy)
        a = glu_ref[rows, :C].astype(F32)
        sb = _sigmoid(glu_ref[rows, C:].astype(F32))
        dglu_ref[rows, :C] = (acc * sb).astype(BF16)
        dglu_ref[rows, C:] = (acc * a * sb * (1.0 - sb)).astype(BF16)


def _attn_bwd(qkv, dattn, sinks, comm=None):
    T = qkv.shape[0]
    tq = _tile(T, 512)
    nblk = tq // BLOCK
    scale = 1.0 / math.sqrt(HEAD_DIM)

    def body(sink_ref, cur_ref, prev_ref, do_ref, dq_ref, hi_ref, lo_ref, ds_ref, kv_buf):
        i = pl.program_id(0)

        @pl.when(i == 0)
        def _():
            ds_ref[...] = jnp.zeros_like(ds_ref)

        _fill_kv(kv_buf, cur_ref, prev_ref)
        upper, distf, keep = _fold_masks(i == 0)
        for j in range(nblk):
            rows = slice(j * BLOCK, (j + 1) * BLOCK)
            band = kv_buf[j * BLOCK:(j + 2) * BLOCK, :]
            q = cur_ref[rows, :ATTN_W]
            do = do_ref[rows, :]
            dqs, dks, dvs = [], [], []
            for kh in range(N_KV):
                k = band[:, kh * HEAD_DIM:(kh + 1) * HEAD_DIM]
                v = band[:, KV_W + kh * HEAD_DIM:KV_W + (kh + 1) * HEAD_DIM]
                qg = _group_rows(q, kh)
                dog = _group_rows(do, kh)
                scores = _dot_nt(qg, k)
                dps = _dot_nt(dog, v)
                ps, dss = [], []
                for g in range(GROUP):
                    h = kh * GROUP + g
                    head = slice(g * BLOCK, (g + 1) * BLOCK)
                    p, psink = _attn_probs(scores[head, :], h, sink_ref[h], upper, distf, keep if j == 0 else None)
                    pdp = p * _fold(dps[head, :], upper)
                    delta = jnp.sum(pdp, axis=-1, keepdims=True)
                    ds_ref[h * SUBLANES:(h + 1) * SUBLANES, :] += _rows8(psink * pdp)
                    dss.append(_unfold(pdp - p * delta, upper).astype(BF16))
                    ps.append(_unfold(p, upper).astype(BF16))
                dsg = jnp.concatenate(dss, axis=0)
                dqg = _dot(dsg, k)
                dqs += [dqg[g * BLOCK:(g + 1) * BLOCK, :] for g in range(GROUP)]
                dks.append(_dot_tn(dsg, qg) * scale)
                dvs.append(_dot_tn(jnp.concatenate(ps, axis=0), dog))
            dq_ref[rows, :] = jnp.concatenate(dqs, axis=1).astype(BF16)
            dkv = jnp.concatenate(dks + dvs, axis=1)
            lo_ref[rows, :] = dkv[:BLOCK, :]
            hi_ref[rows, :] = dkv[BLOCK:, :]

    return _pallas(
        body, name="attn_bwd", grid=(T // tq,),
        in_specs=[pl.BlockSpec(memory_space=pltpu.SMEM),
                  _row_spec(tq, QKV_W),
                  pl.BlockSpec((BLOCK, QKV_W), lambda i: (jnp.maximum(i * nblk - 1, 0), 0)),
                  _row_spec(tq, ATTN_W)],
        out_specs=[_row_spec(tq, ATTN_W), _row_spec(tq, 2 * KV_W), _row_spec(tq, 2 * KV_W),
                   _full_spec((N_Q * SUBLANES, BLOCK))],
        out_shape=[jax.ShapeDtypeStruct((T, ATTN_W), BF16), jax.ShapeDtypeStruct((T, 2 * KV_W), F32),
                   jax.ShapeDtypeStruct((T, 2 * KV_W), F32), jax.ShapeDtypeStruct((N_Q * SUBLANES, BLOCK), F32)],
        scratch_shapes=[pltpu.VMEM((tq + BLOCK, 2 * KV_W), BF16)],
        args=(sinks, qkv, qkv, dattn), sem=("arbitrary",), comm=comm)


def _inproj_bwd(dq, hi, lo, dyc, u0, rest, cw, dga, dgc, x, g, w, dx1, comm=None):
    T, D = x.shape
    C = CONV_C
    tm = _tile(T, 256)
    per = tm // BLOCK
    per_halo = tm // HALO
    nt = T // tm
    kv2 = 2 * KV_W
    glu0, gate0 = QKV_W, QKV_W + 2 * C

    def body(dq_ref, hi_ref, lo_ref, lon_ref, dy_ref, dyn_ref, u0_ref, u0p_ref, glu_ref, cw_ref, dga_ref, dgc_ref,
             x_ref, g_ref, w_ref, dx1_ref, dp_ref, dx_ref, dg_ref, dbias_ref, dcw_ref, dcb_ref, dbuf, ubuf):
        i = pl.program_id(0)

        @pl.when(i == 0)
        def _():
            for ref in (dg_ref, dbias_ref, dcw_ref, dcb_ref):
                ref[...] = jnp.zeros_like(ref)

        def part(cols):
            dp = dp_ref[:, cols]
            dbias_ref[:, cols] += _rows8(dp.astype(F32))
            return _dot_nt(dp, w_ref[:, cols])

        dp_ref[:, :ATTN_W] = dq_ref[...]
        lo_next = jnp.where(i < nt - 1, lon_ref[...], 0.0)
        lo_shift = jnp.concatenate([lo_ref[BLOCK:, :], lo_next], axis=0) if tm > BLOCK else lo_next
        dp_ref[:, ATTN_W:QKV_W] = (hi_ref[...] + lo_shift).astype(BF16)
        dp_ref[:, gate0:gate0 + D] = dga_ref[...]
        dp_ref[:, gate0 + D:] = dgc_ref[...]
        _conv_taps_bwd(i == 0, i == nt - 1, dy_ref, dyn_ref, u0_ref, u0p_ref, glu_ref, cw_ref,
                       dp_ref.at[:, glu0:gate0], dcw_ref, dcb_ref, dbuf, ubuf)
        dh = part(slice(0, QKV_W)) + part(slice(gate0, IN_W)) + part(slice(glu0, gate0))
        dx, dg = _rms_bwd(x_ref[...], g_ref[...], dh)
        dx_ref[...] = dx1_ref[...] + dx
        dg_ref[...] += _rows8(dg)

    return _pallas(
        body, name="inproj_bwd", grid=(nt,),
        in_specs=[_row_spec(tm, ATTN_W), _row_spec(tm, kv2), _row_spec(tm, kv2),
                  pl.BlockSpec((BLOCK, kv2), lambda i: (jnp.minimum((i + 1) * per, T // BLOCK - 1), 0)),
                  _row_spec(tm, C),
                  pl.BlockSpec((HALO, C), lambda i: (jnp.minimum((i + 1) * per_halo, T // HALO - 1), 0)),
                  _row_spec(tm, C),
                  pl.BlockSpec((HALO, C), lambda i: (jnp.maximum(i * per_halo - 1, 0), 0)),
                  _row_spec(tm, 2 * C), _full_spec((CONV_K, C)),
                  _row_spec(tm, D), _row_spec(tm, D), _row_spec(tm, D), _full_spec((1, D)),
                  _weight_spec((D, IN_W)), _row_spec(tm, D)],
        out_specs=[_row_spec(tm, IN_W), _row_spec(tm, D), _full_spec((SUBLANES, D)), _full_spec((SUBLANES, IN_W)),
                   _full_spec((CONV_K * SUBLANES, C)), _full_spec((SUBLANES, C))],
        out_shape=[jax.ShapeDtypeStruct((T, IN_W), BF16), jax.ShapeDtypeStruct((T, D), F32),
                   jax.ShapeDtypeStruct((SUBLANES, D), F32), jax.ShapeDtypeStruct((SUBLANES, IN_W), F32),
                   jax.ShapeDtypeStruct((CONV_K * SUBLANES, C), F32), jax.ShapeDtypeStruct((SUBLANES, C), F32)],
        scratch_shapes=[pltpu.VMEM((SUBLANES, tm + HALO, C), F32), pltpu.VMEM((SUBLANES, tm + HALO, C), F32)],
        args=(dq, hi, lo, lo, dyc, dyc, u0, u0, rest, cw, dga, dgc, x, g, w, dx1), sem=("arbitrary",), comm=comm)


def _adamw_math(g, w, m, v):
    c1 = 1.0 / (1.0 - ADAM_B1 ** ADAM_STEP)
    c2 = 1.0 / (1.0 - ADAM_B2 ** ADAM_STEP)
    mn = ADAM_B1 * m + (1.0 - ADAM_B1) * g
    vn = ADAM_B2 * v + (1.0 - ADAM_B2) * (g * g)
    return -ADAM_LR * ((mn * c1) / (jnp.sqrt(vn * c2) + ADAM_EPS) + ADAM_WD * w), mn, vn


def _adamw_sharded(parts, w, m, v, name):
    depth, a, b = w.shape
    tr = _tile(a, 256) if a % SUBLANES == 0 else a
    nr = a // tr

    def body(*refs):
        p_refs, (w_ref, m_ref, v_ref, g_ref, d_ref, mo_ref, vo_ref) = refs[:depth], refs[depth:]
        layer = pl.program_id(0)
        for l in range(depth):
            @pl.when(layer == l)
            def _(l=l):
                g = p_refs[l][0].astype(F32)
                for s in range(1, N_DEV):
                    g = g + p_refs[l][s].astype(F32)
                g_ref[...] = g
                d_ref[...], mo_ref[...], vo_ref[...] = _adamw_math(g, w_ref[...], m_ref[...], v_ref[...])

    def part_spec(l):
        return pl.BlockSpec((N_DEV, tr, b),
                            lambda k, i: (0, jnp.where(k == l, i, jnp.where(k < l, 0, nr - 1)), 0))

    spec = pl.BlockSpec((None, tr, b), lambda k, i: (k, i, 0))
    out = jax.ShapeDtypeStruct((depth, a, b), F32)
    return pl.pallas_call(
        body, name=name, grid=(depth, nr),
        in_specs=[part_spec(l) for l in range(depth)] + [spec] * 3,
        out_specs=[spec] * 4, out_shape=[out] * 4,
        compiler_params=_params("arbitrary", "arbitrary"),
    )(*parts, w, m, v)


def _adamw_small(parts, w, m, v):
    R, N = w.shape

    def body(p_ref, w_ref, m_ref, v_ref, g_ref, d_ref, mo_ref, vo_ref):
        g = p_ref[0]
        for s in range(1, N_DEV):
            g = g + p_ref[s]
        g_ref[...] = g
        d_ref[...], mo_ref[...], vo_ref[...] = _adamw_math(g, w_ref[...], m_ref[...], v_ref[...])

    out = jax.ShapeDtypeStruct((R, N), F32)
    return pl.pallas_call(
        body, name="adamw_small", grid=(1,),
        in_specs=[_full_spec((N_DEV, R, N))] + [_full_spec((R, N))] * 3,
        out_specs=[_full_spec((R, N))] * 4, out_shape=[out] * 4,
        compiler_params=_params("arbitrary"),
    )(parts, w, m, v)


_SHARDED = ("w_in", "conv_w", "w_attn_proj", "w_conv_proj", "w_out", "w_mlp1", "w_mlp2")
_ROW_SHARDED = ("w_out", "w_mlp2")
_FIRST = ("w_in", "conv_w")
_REST = tuple(n for n in _SHARDED if n not in _FIRST)
_SMALL = ("mix_norm_g", "b_in", "sinks", "conv_b", "conv_ln_g", "conv_ln_b", "b_conv_proj", "mlp_norm_g",
          "final_norm_g")
_ORDER = ("mix_norm_g", "w_in", "b_in", "sinks", "conv_w", "conv_b", "conv_ln_g", "conv_ln_b", "w_attn_proj",
          "w_conv_proj", "b_conv_proj", "w_out", "mlp_norm_g", "w_mlp1", "w_mlp2", "final_norm_g")
_PACK = 1024


def _full_weight(name, gathered):
    _, a, b = gathered.shape
    if name in _ROW_SHARDED:
        return gathered.reshape(N_DEV * a, b)
    return gathered.transpose(1, 0, 2).reshape(a, N_DEV * b)


def _pack(arrs):
    flat = []
    for a in arrs:
        a = a.reshape(-1)
        flat.append(jnp.pad(a, (0, -a.size % _PACK)))
    return jnp.concatenate(flat).reshape(-1, BLOCK)


def _unpack(packed, shapes):
    flat = packed.reshape(-1)
    out, off = [], 0
    for s in shapes:
        n = math.prod(s)
        out.append(flat[off:off + n].reshape(s))
        off += n + (-n % _PACK)
    return out


def _layer_fwd(x, lw, own_rest=None, comm=None):
    (h, qkv, rest), got = _inproj_fwd(x, lw["mix_norm_g"], lw["w_in"], lw["b_in"], own_rest)
    if got is not None:
        lw.update({n: _full_weight(n, a) for n, a in zip(_REST, got)})
    attn = _attn_fwd(qkv, lw["sinks"])
    u0, yc, u = _conv_fwd(rest, lw["conv_w"], lw["conv_b"], lw["conv_ln_g"], lw["conv_ln_b"])
    merged, x1 = _merge_fwd(attn, u, rest, x, lw["w_attn_proj"], lw["w_conv_proj"], lw["b_conv_proj"], lw["w_out"])
    (h2, z, x2), gathered = _mlp_fwd(x1, lw["mlp_norm_g"], lw["w_mlp1"], lw["w_mlp2"], comm)
    saved = dict(x=x, h=h, qkv=qkv, rest=rest, attn=attn, u0=u0, yc=yc, u=u, merged=merged, x1=x1, h2=h2, z=z)
    return x2, saved, gathered


_EARLY = ("w_mlp1", "w_mlp2")
_MIDDLE = ("w_out", "w_attn_proj", "w_conv_proj")
_LATE = ("w_in", "conv_w")


def _layer_bwd(dx2, lw, s, late_blocks=None):
    g, recv = {}, {}
    late = None if late_blocks is None else _Exchange(late_blocks)
    (dx1, dz, dg2), late_recv = _mlp_bwd(dx2, s["x1"], s["z"], lw["mlp_norm_g"], lw["w_mlp1"], lw["w_mlp2"], late)
    g["mlp_norm_g"] = jnp.sum(dg2, axis=0)
    early = [_tn_blocks(s["h2"], dz, "dw_mlp1", True), _tn_blocks(s["z"], dx2, "dw_mlp2", False, relu_sq=True)]
    dattn, dyc, dga, dgc, dbra, dbrc, dbc, dlg, dlb = _merge_bwd(
        dx1, s["attn"], s["u"], s["rest"], s["yc"], lw["w_attn_proj"], lw["w_conv_proj"], lw["b_conv_proj"],
        lw["w_out"], lw["conv_ln_g"], lw["conv_ln_b"])
    g["b_conv_proj"] = jnp.sum(dbc, axis=0)
    g["conv_ln_g"] = jnp.sum(dlg, axis=0)
    g["conv_ln_b"] = jnp.sum(dlb, axis=0)
    middle = [_tn_blocks(s["merged"], dx1, "dw_out", False), _tn_blocks(s["attn"], dbra, "dw_attn_proj", True),
              _tn_blocks(s["u"], dbrc, "dw_conv_proj", True)]
    (dq, hi, lo, dsk), middle_recv = _attn_bwd(s["qkv"], dattn, lw["sinks"], _Exchange(middle))
    recv.update(zip(_MIDDLE, middle_recv))
    g["sinks"] = -jnp.sum(dsk.reshape(N_Q, SUBLANES * BLOCK), axis=1)
    (dproj, dx, dg1, dbin, dcw, dcb), early_recv = _inproj_bwd(
        dq, hi, lo, dyc, s["u0"], s["rest"], lw["conv_w"], dga, dgc, s["x"], lw["mix_norm_g"], lw["w_in"], dx1,
        _Exchange(early))
    recv.update(zip(_EARLY, early_recv))
    dconv_w = jnp.sum(dcw.reshape(CONV_K, SUBLANES, CONV_C), axis=1)
    g["conv_b"] = jnp.sum(dcb, axis=0)
    g["mix_norm_g"] = jnp.sum(dg1, axis=0)
    g["b_in"] = jnp.sum(dbin, axis=0)
    own_late = [_tn_blocks(s["h"], dproj, "dw_in", True),
                dconv_w.reshape(CONV_K, N_DEV, CONV_C // N_DEV).transpose(1, 0, 2)]
    return dx, g, recv, late_recv, own_late


def kernel(x, mix_norm_g, w_in, b_in, sinks, conv_w, conv_b, conv_ln_g, conv_ln_b, w_attn_proj, w_conv_proj, b_conv_proj, w_out, mlp_norm_g, w_mlp1, w_mlp2, final_norm_g, loss_target, m_mix_norm_g, m_w_in, m_b_in, m_sinks, m_conv_w, m_conv_b, m_conv_ln_g, m_conv_ln_b, m_w_attn_proj, m_w_conv_proj, m_b_conv_proj, m_w_out, m_mlp_norm_g, m_w_mlp1, m_w_mlp2, m_final_norm_g, v_mix_norm_g, v_w_in, v_b_in, v_sinks, v_conv_w, v_conv_b, v_conv_ln_g, v_conv_ln_b, v_w_attn_proj, v_w_conv_proj, v_b_conv_proj, v_w_out, v_mlp_norm_g, v_w_mlp1, v_w_mlp2, v_final_norm_g):
    w = dict(mix_norm_g=mix_norm_g, w_in=w_in, b_in=b_in, sinks=sinks, conv_w=conv_w, conv_b=conv_b,
             conv_ln_g=conv_ln_g, conv_ln_b=conv_ln_b, w_attn_proj=w_attn_proj, w_conv_proj=w_conv_proj,
             b_conv_proj=b_conv_proj, w_out=w_out, mlp_norm_g=mlp_norm_g, w_mlp1=w_mlp1, w_mlp2=w_mlp2,
             final_norm_g=final_norm_g)
    m = dict(mix_norm_g=m_mix_norm_g, w_in=m_w_in, b_in=m_b_in, sinks=m_sinks, conv_w=m_conv_w, conv_b=m_conv_b,
             conv_ln_g=m_conv_ln_g, conv_ln_b=m_conv_ln_b, w_attn_proj=m_w_attn_proj, w_conv_proj=m_w_conv_proj,
             b_conv_proj=m_b_conv_proj, w_out=m_w_out, mlp_norm_g=m_mlp_norm_g, w_mlp1=m_w_mlp1, w_mlp2=m_w_mlp2,
             final_norm_g=m_final_norm_g)
    v = dict(mix_norm_g=v_mix_norm_g, w_in=v_w_in, b_in=v_b_in, sinks=v_sinks, conv_w=v_conv_w, conv_b=v_conv_b,
             conv_ln_g=v_conv_ln_g, conv_ln_b=v_conv_ln_b, w_attn_proj=v_w_attn_proj, w_conv_proj=v_w_conv_proj,
             b_conv_proj=v_b_conv_proj, w_out=v_w_out, mlp_norm_g=v_mlp_norm_g, w_mlp1=v_w_mlp1, w_mlp2=v_w_mlp2,
             final_norm_g=v_final_norm_g)
    T = x.shape[1]
    xs = x.reshape(T, D_MODEL)
    target = loss_target.reshape(T, D_MODEL)

    def gather_of(l, names):
        return _Gather([w[n][l] if n == "conv_w" else w[n][l].astype(BF16) for n in names])

    def layer_weights(l, names, gathered):
        lw = {n: _full_weight(n, a) for n, a in zip(names, gathered)}
        for n in _SMALL:
            if n != "final_norm_g":
                lw[n] = w[n][l] if n == "sinks" else w[n][l].reshape(1, -1)
        return lw

    acts = xs
    saved, weights = [], []
    for l in range(DEPTH):
        following = gather_of(l + 1, _SHARDED) if l + 1 < DEPTH else None
        if l == 0:
            lw = layer_weights(0, _FIRST, _run_comm(gather_of(0, _FIRST), "gather_first"))
            acts, s, gathered = _layer_fwd(acts, lw, gather_of(0, _REST), following)
        else:
            lw = layer_weights(l, _SHARDED, gathered)
            acts, s, gathered = _layer_fwd(acts, lw, None, following)
        weights.append(lw)
        saved.append(s)
    lterms, dx, dgf = _final_loss(acts, final_norm_g.reshape(1, -1), target)
    grads, received = [None] * DEPTH, [None] * DEPTH
    late = None
    for l in reversed(range(DEPTH)):
        dx, grads[l], received[l], late_recv, late = _layer_bwd(dx, weights[l], saved[l], late)
        if late_recv is not None:
            received[l + 1].update(zip(_LATE, late_recv))
    received[0].update(zip(_LATE, _run_comm(_Exchange(late), "scatter_late")))
    grad = {n: jnp.stack([grads[l][n] for l in range(DEPTH)]) for n in _SMALL if n != "final_norm_g"}
    grad["final_norm_g"] = jnp.sum(dgf, axis=0)

    small_shapes = [w[n].shape for n in _SMALL] + [(1,)]
    small = _pack([grad[n] for n in _SMALL] + [jnp.sum(lterms).reshape(1)])
    small_parts = _run_comm(_Gather([small]), "gather_small")[0]

    out_g, out_d, out_m, out_v = {}, {}, {}, {}
    for n in _SHARDED:
        out_g[n], out_d[n], out_m[n], out_v[n] = _adamw_sharded(
            [received[l][n] for l in range(DEPTH)], w[n], m[n], v[n], "adamw_" + n)
    zero = jnp.zeros((1,), F32)
    res = _adamw_small(small_parts, _pack([w[n] for n in _SMALL] + [zero]), _pack([m[n] for n in _SMALL] + [zero]),
                       _pack([v[n] for n in _SMALL] + [zero]))
    unpacked = [_unpack(r, small_shapes) for r in res]
    for i, n in enumerate(_SMALL):
        out_g[n], out_d[n], out_m[n], out_v[n] = (u[i] for u in unpacked)
    loss = unpacked[0][-1].reshape(())
    return (loss, dx.reshape(x.shape), *[out_g[n] for n in _ORDER], *[out_d[n] for n in _ORDER],
            *[out_m[n] for n in _ORDER], *[out_v[n] for n in _ORDER])
```

```python
import functools
import math

import jax
import jax.numpy as jnp
from jax import lax
from jax.experimental import pallas as pl
from jax.experimental.pallas import tpu as pltpu

D_MODEL = 1024
SEQ = 16384
DEPTH = 2
N_Q = 8
N_KV = 2
GROUP = N_Q // N_KV
HEAD_DIM = 64
ATTN_W = N_Q * HEAD_DIM
KV_W = N_KV * HEAD_DIM
BLOCK = 128
CONV_C = D_MODEL // 2
CONV_K = 31
D_FF = 4 * D_MODEL
QKV_W = ATTN_W + 2 * KV_W
IN_W = QKV_W + 2 * CONV_C + 2 * D_MODEL
EPS = 1e-6
NEG = -1e30
N_DEV = 8

ADAM_LR = 0.001
ADAM_B1 = 0.9
ADAM_B2 = 0.999
ADAM_EPS = 1e-08
ADAM_WD = 0.01
ADAM_STEP = 10

F32 = jnp.float32
BF16 = jnp.bfloat16
MESH = pl.DeviceIdType.MESH

SUBLANES = 8
HALO = 32
FF_CHUNK = 1024
CONV_ROWS = 16
VMEM_LIMIT = 52 * 1024 * 1024

_NT = (((1,), (1,)), ((), ()))
_TN = (((0,), (0,)), ((), ()))


def _params(*sem):
    return pltpu.CompilerParams(dimension_semantics=sem, vmem_limit_bytes=VMEM_LIMIT)


def _tile(n, pref):
    t = min(n, pref)
    assert n % t == 0, (n, t)
    return t


def _sigmoid(v):
    return 1.0 / (1.0 + jnp.exp(-v))


def _rows8(v):
    r, n = v.shape
    return jnp.sum(v.reshape(r // SUBLANES, SUBLANES, n), axis=0)


def _dot(a, b):
    return jnp.dot(a, b, preferred_element_type=F32)


def _dot_nt(a, b):
    return lax.dot_general(a, b, _NT, preferred_element_type=F32)


def _dot_tn(a, b):
    return lax.dot_general(a, b, _TN, preferred_element_type=F32)


def _rms_bwd(xv, g, dh):
    r = lax.rsqrt(jnp.mean(xv * xv, axis=-1, keepdims=True) + EPS)
    xhat = xv * r
    dxhat = dh * g
    dx = r * (dxhat - xhat * jnp.mean(dxhat * xhat, axis=-1, keepdims=True))
    return dx, dh * xhat


def _row_spec(tm, n, col=0):
    return pl.BlockSpec((tm, n), lambda i: (i, col))


def _full_spec(shape):
    return pl.BlockSpec(shape, lambda *_: (0,) * len(shape))


def _weight_spec(shape):
    return pl.BlockSpec(shape, lambda *_: (0,) * len(shape), pipeline_mode=pl.Buffered(1))


def _mesh_pos():
    return lax.axis_index("x"), lax.axis_index("y"), lax.axis_index("c")


def _dev_index(dev):
    return 4 * dev[0] + 2 * dev[1] + dev[2]


class _Exchange:
    middle_at = None

    def __init__(self, arrs):
        self.arrays = list(arrs)

    def out_shape(self):
        return [jax.ShapeDtypeStruct(a.shape, a.dtype) for a in self.arrays]

    def scratch(self):
        n = len(self.arrays)
        return [pltpu.SemaphoreType.DMA((7 * n,)), pltpu.SemaphoreType.DMA((7 * n,)), pltpu.SemaphoreType.DMA((n,))]

    def _copies(self, ins, outs, sems):
        send_sems, recv_sems, local_sems = sems
        x, y, c = _mesh_pos()
        me = _dev_index((x, y, c))
        mine, sends, arrivals = [], [], []
        for p in range(len(self.arrays)):
            mine.append(pltpu.make_async_copy(ins[p].at[me], outs[p].at[me], local_sems.at[p]))
            for k in range(1, N_DEV):
                peer = (1 - x if k & 4 else x, 1 - y if k & 2 else y, 1 - c if k & 1 else c)
                pid = _dev_index(peer)
                pair = dict(send_sem=send_sems.at[7 * p + k - 1], recv_sem=recv_sems.at[7 * p + k - 1],
                            device_id=peer, device_id_type=MESH)
                sends.append(pltpu.make_async_remote_copy(src_ref=ins[p].at[pid], dst_ref=outs[p].at[me], **pair))
                arrivals.append(pltpu.make_async_remote_copy(src_ref=ins[p].at[pid], dst_ref=outs[p].at[pid], **pair))
        return mine, sends, arrivals

    def start(self, ins, outs, sems):
        mine, sends, _ = self._copies(ins, outs, sems)
        for cp in mine + sends:
            cp.start()

    def finish(self, ins, outs, sems):
        mine, sends, arrivals = self._copies(ins, outs, sems)
        for cp in arrivals:
            cp.wait_recv()
        for cp in sends:
            cp.wait_send()
        for cp in mine:
            cp.wait()


class _Gather:
    middle_at = 0.75

    def __init__(self, arrs):
        self.arrays = list(arrs)

    def out_shape(self):
        return [jax.ShapeDtypeStruct((N_DEV,) + a.shape, a.dtype) for a in self.arrays]

    def scratch(self):
        n = len(self.arrays)
        return [pltpu.SemaphoreType.DMA((7 * n,)), pltpu.SemaphoreType.DMA((7 * n,)), pltpu.SemaphoreType.DMA((n,))]

    def _copies(self, ins, outs, sems):
        send_sems, recv_sems, local_sems = sems
        x, y, c = _mesh_pos()
        me, sibling = (x, y, c), (x, y, 1 - c)
        chips = [(1 - x, y), (x, 1 - y), (1 - x, 1 - y)]
        n = len(self.arrays)

        def copy(p, k, dev, to, src=None):
            block = outs[p].at[_dev_index(dev)]
            return pltpu.make_async_remote_copy(
                src_ref=block if src is None else src, dst_ref=block,
                send_sem=send_sems.at[7 * p + k], recv_sem=recv_sems.at[7 * p + k],
                device_id=to, device_id_type=MESH)

        cp = dict(mine=[pltpu.make_async_copy(ins[p], outs[p].at[_dev_index(me)], local_sems.at[p])
                        for p in range(n)])
        cp["first"] = [copy(p, 0, me, sibling, src=ins[p]) for p in range(n)]
        cp["first"] += [copy(p, 1 + j, me, (*chip, c), src=ins[p]) for p in range(n) for j, chip in enumerate(chips)]
        cp["over_ici"] = [copy(p, 1 + j, (*chip, c), me) for j, chip in enumerate(chips) for p in range(n)]
        cp["passed"] = [copy(p, 4 + j, (*chip, c), sibling) for j, chip in enumerate(chips) for p in range(n)]
        cp["from_sibling"] = [copy(p, 0, sibling, me) for p in range(n)]
        cp["from_sibling"] += [copy(p, 4 + j, (*chip, 1 - c), me) for j, chip in enumerate(chips) for p in range(n)]
        return cp

    def start(self, ins, outs, sems):
        cp = self._copies(ins, outs, sems)
        for d in cp["mine"] + cp["first"]:
            d.start()

    def middle(self, ins, outs, sems):
        cp = self._copies(ins, outs, sems)
        for arrived, onward in zip(cp["over_ici"], cp["passed"]):
            arrived.wait_recv()
            onward.start()

    def finish(self, ins, outs, sems):
        cp = self._copies(ins, outs, sems)
        for d in cp["from_sibling"]:
            d.wait_recv()
        for d in cp["first"] + cp["passed"]:
            d.wait_send()
        for d in cp["mine"]:
            d.wait()


def _run_comm(comm, name):
    n = len(comm.arrays)

    def body(*refs):
        ins, outs, sems = refs[:n], refs[n:2 * n], refs[2 * n:]
        comm.start(ins, outs, sems)
        if comm.middle_at is not None:
            comm.middle(ins, outs, sems)
        comm.finish(ins, outs, sems)

    any_spec = pl.BlockSpec(memory_space=pl.ANY)
    return pl.pallas_call(
        body, name=name, in_specs=[any_spec] * n, out_specs=[any_spec] * n, out_shape=comm.out_shape(),
        scratch_shapes=comm.scratch(),
    )(*comm.arrays)


def _pallas(body, *, name, grid, in_specs, out_specs, out_shape, args, sem, scratch_shapes=(), comm=None):
    if comm is None:
        outs = pl.pallas_call(
            body, name=name, grid=grid, in_specs=in_specs, out_specs=out_specs, out_shape=out_shape,
            scratch_shapes=list(scratch_shapes), compiler_params=_params(*sem),
        )(*args)
        return outs, None
    n_in, n_out, n_scr, n_c = len(in_specs), len(out_specs), len(scratch_shapes), len(comm.arrays)
    steps = grid[0]
    middle = None if comm.middle_at is None else min(steps - 1, int(steps * comm.middle_at))

    def carried(*refs):
        ins, refs = refs[:n_in], refs[n_in:]
        cins, refs = refs[:n_c], refs[n_c:]
        outs, refs = refs[:n_out], refs[n_out:]
        couts, refs = refs[:n_c], refs[n_c:]
        scr, csems = refs[:n_scr], refs[n_scr:]
        step = pl.program_id(0)

        @pl.when(step == 0)
        def _():
            comm.start(cins, couts, csems)

        body(*ins, *outs, *scr)

        if middle is not None:
            @pl.when(step == middle)
            def _():
                comm.middle(cins, couts, csems)

        @pl.when(step == steps - 1)
        def _():
            comm.finish(cins, couts, csems)

    any_spec = pl.BlockSpec(memory_space=pl.ANY)
    res = pl.pallas_call(
        carried, name=name, grid=grid,
        in_specs=list(in_specs) + [any_spec] * n_c, out_specs=list(out_specs) + [any_spec] * n_c,
        out_shape=list(out_shape) + comm.out_shape(),
        scratch_shapes=list(scratch_shapes) + comm.scratch(),
        compiler_params=_params(*(("arbitrary",) + tuple(sem[1:]))),
    )(*args, *comm.arrays)
    return res[:n_out], res[n_out:]


def _inproj_fwd(x, g, w, b, comm=None):
    T, D = x.shape
    rest_w = IN_W - QKV_W
    tm = _tile(T, 512)

    def body(x_ref, g_ref, w_ref, b_ref, h_ref, qkv_ref, rest_ref):
        xv = x_ref[...]
        r = lax.rsqrt(jnp.mean(xv * xv, axis=-1, keepdims=True) + EPS)
        h = (xv * r * g_ref[...]).astype(BF16)
        h_ref[...] = h
        qkv_ref[...] = (_dot(h, w_ref[:, :QKV_W]) + b_ref[:, :QKV_W]).astype(BF16)
        rest_ref[...] = (_dot(h, w_ref[:, QKV_W:]) + b_ref[:, QKV_W:]).astype(BF16)

    return _pallas(
        body, name="inproj_fwd", grid=(T // tm,),
        in_specs=[_row_spec(tm, D), _full_spec((1, D)), _weight_spec((D, IN_W)), _full_spec((1, IN_W))],
        out_specs=[_row_spec(tm, D), _row_spec(tm, QKV_W), _row_spec(tm, rest_w)],
        out_shape=[jax.ShapeDtypeStruct((T, D), BF16), jax.ShapeDtypeStruct((T, QKV_W), BF16),
                   jax.ShapeDtypeStruct((T, rest_w), BF16)],
        args=(x, g, w, b), sem=("parallel",), comm=comm)


def _fold_masks(first):
    shape = (BLOCK, GROUP * BLOCK)
    key = lax.broadcasted_iota(jnp.int32, shape, 0)
    qry = lax.broadcasted_iota(jnp.int32, shape, 1) & (BLOCK - 1)
    upper = key > qry
    dist = jnp.where(upper, qry + BLOCK - key, qry - key)
    keep = key <= qry + jnp.where(first, 0, BLOCK)
    return upper, dist.astype(F32), keep


def _fold(band, upper):
    return jnp.where(upper, band[:BLOCK, :], band[BLOCK:, :])


def _unfold(folded, upper):
    return jnp.concatenate([jnp.where(upper, folded, 0.0), jnp.where(upper, 0.0, folded)], axis=0)


def _head_row(values):
    return jnp.concatenate([jnp.full((1, BLOCK), v, F32) for v in values], axis=1)


def _head_consts(sink_ref, distf):
    bias, sink = [], []
    for kh in range(N_KV):
        heads = range(kh * GROUP, (kh + 1) * GROUP)
        bias.append(_head_row([2.0 ** (-8.0 * (h + 1) / N_Q) for h in heads]) * distf)
        sink.append(_head_row([sink_ref[h] for h in heads]))
    return bias, sink


def _heads_out(t):
    stacked = jnp.concatenate([t[:, g * BLOCK:(g + 1) * BLOCK] for g in range(GROUP)], axis=0)
    return stacked.T


def _fill_kv(kv_buf, cur_ref, prev_ref):
    scale = 1.0 / math.sqrt(HEAD_DIM)
    assert math.frexp(scale)[0] == 0.5
    for r0, ref in ((0, prev_ref), (BLOCK, cur_ref)):
        rows = ref.shape[0]
        kv_buf[r0:r0 + rows, :KV_W] = ref[:, ATTN_W:ATTN_W + KV_W] * scale
        kv_buf[r0:r0 + rows, KV_W:] = ref[:, ATTN_W + KV_W:]


def _group_rows(x, kh):
    return jnp.concatenate([x[:, h * HEAD_DIM:(h + 1) * HEAD_DIM] for h in range(kh * GROUP, (kh + 1) * GROUP)],
                           axis=0)


def _attn_probs(scores, bias, sink, upper, keep):
    s = _fold(scores, upper) - bias
    if keep is not None:
        s = jnp.where(keep, s, NEG)
    m = jnp.maximum(jnp.max(s, axis=0, keepdims=True), sink)
    p = jnp.exp(s - m)
    e = jnp.exp(sink - m)
    inv = 1.0 / (jnp.sum(p, axis=0, keepdims=True) + e)
    return p * inv, e * inv


def _attn_fwd(qkv, sinks):
    T = qkv.shape[0]
    tq = _tile(T, 512)
    nblk = tq // BLOCK

    def body(sink_ref, cur_ref, prev_ref, o_ref, kv_buf):
        _fill_kv(kv_buf, cur_ref, prev_ref)
        upper, distf, keep = _fold_masks(pl.program_id(0) == 0)
        bias, sink = _head_consts(sink_ref, distf)
        for j in range(nblk):
            band = kv_buf[j * BLOCK:(j + 2) * BLOCK, :]
            q = cur_ref[j * BLOCK:(j + 1) * BLOCK, :ATTN_W]
            outs = []
            for kh in range(N_KV):
                k = band[:, kh * HEAD_DIM:(kh + 1) * HEAD_DIM]
                v = band[:, KV_W + kh * HEAD_DIM:KV_W + (kh + 1) * HEAD_DIM]
                p, _ = _attn_probs(_dot_nt(k, _group_rows(q, kh)), bias[kh], sink[kh], upper,
                                   keep if j == 0 else None)
                outs.append(_heads_out(_dot_tn(v, _unfold(p, upper).astype(BF16))))
            o_ref[j * BLOCK:(j + 1) * BLOCK, :] = jnp.concatenate(outs, axis=1).astype(BF16)

    return pl.pallas_call(
        body, name="attn_fwd", grid=(T // tq,),
        in_specs=[pl.BlockSpec(memory_space=pltpu.SMEM),
                  _row_spec(tq, QKV_W),
                  pl.BlockSpec((BLOCK, QKV_W), lambda i: (jnp.maximum(i * nblk - 1, 0), 0))],
        out_specs=_row_spec(tq, ATTN_W),
        out_shape=jax.ShapeDtypeStruct((T, ATTN_W), BF16),
        scratch_shapes=[pltpu.VMEM((tq + BLOCK, 2 * KV_W), BF16)],
        compiler_params=_params("parallel"),
    )(sinks, qkv, qkv)


def _shifted_copies(buf):
    n = buf.shape[1] - SUBLANES
    for s in range(1, SUBLANES):
        buf[s, 0:n, :] = buf[0, s:s + n, :]


def _shifted_rows(buf, start, rows):
    s = start % SUBLANES
    return buf[s, start - s:start - s + rows, :]


def _conv_fwd(rest, cw, cb, lg, lb):
    T = rest.shape[0]
    C = CONV_C
    tm = _tile(T, 256)
    R = _tile(tm, CONV_ROWS)
    per = tm // HALO

    def body(cur_ref, prev_ref, w_ref, cb_ref, g_ref, b_ref, u0_ref, yc_ref, u_ref, ubuf):
        i = pl.program_id(0)
        up = prev_ref[:, :C].astype(F32) * _sigmoid(prev_ref[:, C:].astype(F32))
        ubuf[0, 0:HALO, :] = jnp.where(i > 0, up, 0.0)
        u0 = cur_ref[:, :C].astype(F32) * _sigmoid(cur_ref[:, C:].astype(F32))
        ubuf[0, HALO:, :] = u0
        u0_ref[...] = u0
        _shifted_copies(ubuf)
        off = HALO - (CONV_K - 1)
        for c in range(tm // R):
            acc = jnp.broadcast_to(cb_ref[...], (R, C))
            for j in range(CONV_K):
                acc = acc + w_ref[j:j + 1, :] * _shifted_rows(ubuf, c * R + off + j, R)
            yc_ref[c * R:(c + 1) * R, :] = acc
            xc = acc - jnp.mean(acc, axis=-1, keepdims=True)
            ln = xc * lax.rsqrt(jnp.mean(xc * xc, axis=-1, keepdims=True) + EPS) * g_ref[...] + b_ref[...]
            u_ref[c * R:(c + 1) * R, :] = (ln * _sigmoid(ln)).astype(BF16)

    return pl.pallas_call(
        body, name="conv_fwd", grid=(T // tm,),
        in_specs=[_row_spec(tm, 2 * C),
                  pl.BlockSpec((HALO, 2 * C), lambda i: (jnp.maximum(i * per - 1, 0), 0)),
                  _full_spec((CONV_K, C)), _full_spec((1, C)), _full_spec((1, C)), _full_spec((1, C))],
        out_specs=[_row_spec(tm, C), _row_spec(tm, C), _row_spec(tm, C)],
        out_shape=[jax.ShapeDtypeStruct((T, C), F32), jax.ShapeDtypeStruct((T, C), F32),
                   jax.ShapeDtypeStruct((T, C), BF16)],
        scratch_shapes=[pltpu.VMEM((SUBLANES, tm + HALO, C), F32)],
        compiler_params=_params("parallel"),
    )(rest, rest, cw, cb, lg, lb)


def _merge_fwd(attn, u, rest, x, wa, wc, bc, wo):
    T, D = x.shape
    tm = _tile(T, 512)
    gcol = 2 * CONV_C // D

    def body(attn_ref, u_ref, ga_ref, gc_ref, x_ref, wa_ref, wc_ref, bc_ref, wo_ref, m_ref, x1_ref):
        bra = _dot(attn_ref[...], wa_ref[...])
        brc = _dot(u_ref[...], wc_ref[...]) + bc_ref[...]
        mb = (_sigmoid(ga_ref[...].astype(F32)) * bra + _sigmoid(gc_ref[...].astype(F32)) * brc).astype(BF16)
        m_ref[...] = mb
        x1_ref[...] = x_ref[...] + _dot(mb, wo_ref[...])

    return pl.pallas_call(
        body, name="merge_fwd", grid=(T // tm,),
        in_specs=[_row_spec(tm, ATTN_W), _row_spec(tm, CONV_C), _row_spec(tm, D, gcol), _row_spec(tm, D, gcol + 1),
                  _row_spec(tm, D), _weight_spec((ATTN_W, D)), _weight_spec((CONV_C, D)), _full_spec((1, D)),
                  _weight_spec((D, D))],
        out_specs=[_row_spec(tm, D), _row_spec(tm, D)],
        out_shape=[jax.ShapeDtypeStruct((T, D), BF16), jax.ShapeDtypeStruct((T, D), F32)],
        compiler_params=_params("parallel"),
    )(attn, u, rest, rest, x, wa, wc, bc, wo)


def _mlp_fwd(x1, g, w1, w2, comm=None):
    T, D = x1.shape
    tm = _tile(T, 512)
    fc = _tile(D_FF, FF_CHUNK)

    def body(x_ref, g_ref, w1_ref, w2_ref, h_ref, z_ref, o_ref):
        xv = x_ref[...]
        r = lax.rsqrt(jnp.mean(xv * xv, axis=-1, keepdims=True) + EPS)
        h = (xv * r * g_ref[...]).astype(BF16)
        h_ref[...] = h
        acc = xv
        for c in range(D_FF // fc):
            cols = slice(c * fc, (c + 1) * fc)
            z = _dot(h, w1_ref[:, cols])
            z_ref[:, cols] = z.astype(BF16)
            acc = acc + _dot(jnp.square(jnp.maximum(z, 0.0)).astype(BF16), w2_ref[cols, :])
        o_ref[...] = acc

    return _pallas(
        body, name="mlp_fwd", grid=(T // tm,),
        in_specs=[_row_spec(tm, D), _full_spec((1, D)), _weight_spec((D, D_FF)), _weight_spec((D_FF, D))],
        out_specs=[_row_spec(tm, D), _row_spec(tm, D_FF), _row_spec(tm, D)],
        out_shape=[jax.ShapeDtypeStruct((T, D), BF16), jax.ShapeDtypeStruct((T, D_FF), BF16),
                   jax.ShapeDtypeStruct((T, D), F32)],
        args=(x1, g, w1, w2), sem=("parallel",), comm=comm)


def _final_loss(x, g, target):
    T, D = x.shape
    tm = _tile(T, 512)

    def body(x_ref, g_ref, t_ref, l_ref, dx_ref, dg_ref):
        @pl.when(pl.program_id(0) == 0)
        def _():
            l_ref[...] = jnp.zeros_like(l_ref)
            dg_ref[...] = jnp.zeros_like(dg_ref)

        xv = x_ref[...]
        r = lax.rsqrt(jnp.mean(xv * xv, axis=-1, keepdims=True) + EPS)
        e = xv * r * g_ref[...] - t_ref[...]
        l_ref[...] += _rows8(e * e) * (0.5 / D)
        dx, dg = _rms_bwd(xv, g_ref[...], e * (1.0 / D))
        dx_ref[...] = dx
        dg_ref[...] += _rows8(dg)

    return pl.pallas_call(
        body, name="final_loss", grid=(T // tm,),
        in_specs=[_row_spec(tm, D), _full_spec((1, D)), _row_spec(tm, D)],
        out_specs=[_full_spec((SUBLANES, D)), _row_spec(tm, D), _full_spec((SUBLANES, D))],
        out_shape=[jax.ShapeDtypeStruct((SUBLANES, D), F32), jax.ShapeDtypeStruct((T, D), F32),
                   jax.ShapeDtypeStruct((SUBLANES, D), F32)],
        compiler_params=_params("arbitrary"),
    )(x, g, target)


def _mlp_bwd(dx2, x1, z, g, w1, w2, comm=None):
    T, D = x1.shape
    tm = _tile(T, 512)
    fc = _tile(D_FF, FF_CHUNK)

    def body(dx2_ref, x_ref, z_ref, g_ref, w1_ref, w2_ref, dx1_ref, dz_ref, dg_ref):
        @pl.when(pl.program_id(0) == 0)
        def _():
            dg_ref[...] = jnp.zeros_like(dg_ref)

        dxo = dx2_ref[...]
        dxb = dxo.astype(BF16)
        dh = jnp.zeros((tm, D), F32)
        for c in range(D_FF // fc):
            cols = slice(c * fc, (c + 1) * fc)
            da = _dot_nt(dxb, w2_ref[cols, :])
            dz = (da * (2.0 * jnp.maximum(z_ref[:, cols].astype(F32), 0.0))).astype(BF16)
            dz_ref[:, cols] = dz
            dh = dh + _dot_nt(dz, w1_ref[:, cols])
        dx, dg = _rms_bwd(x_ref[...], g_ref[...], dh)
        dx1_ref[...] = dxo + dx
        dg_ref[...] += _rows8(dg)

    return _pallas(
        body, name="mlp_bwd", grid=(T // tm,),
        in_specs=[_row_spec(tm, D), _row_spec(tm, D), _row_spec(tm, D_FF), _full_spec((1, D)),
                  _weight_spec((D, D_FF)), _weight_spec((D_FF, D))],
        out_specs=[_row_spec(tm, D), _row_spec(tm, D_FF), _full_spec((SUBLANES, D))],
        out_shape=[jax.ShapeDtypeStruct((T, D), F32), jax.ShapeDtypeStruct((T, D_FF), BF16),
                   jax.ShapeDtypeStruct((SUBLANES, D), F32)],
        args=(dx2, x1, z, g, w1, w2), sem=("arbitrary",), comm=comm)


def _tn_blocks(a, b, name, col_sharded, relu_sq=False):
    T, M = a.shape
    N = b.shape[1]
    tk = _tile(T, 1024)
    tm = _tile(M, 512 if col_sharded else 1024)
    nb = N // N_DEV
    last = T // tk - 1

    def body(a_ref, b_ref, o_ref, acc_ref):
        k = pl.program_id(1)

        @pl.when(k == 0)
        def _():
            acc_ref[...] = jnp.zeros_like(acc_ref)

        av = a_ref[...]
        if relu_sq:
            av = jnp.square(jnp.maximum(av, 0.0))
        acc_ref[...] += _dot_tn(av.astype(BF16), b_ref[...].astype(BF16))

        @pl.when(k == last)
        def _():
            if col_sharded:
                for d in range(N_DEV):
                    o_ref[d] = acc_ref[:, d * nb:(d + 1) * nb].astype(BF16)
            else:
                o_ref[...] = acc_ref[...].astype(BF16)

    if col_sharded:
        out_spec = pl.BlockSpec((N_DEV, tm, nb), lambda i, k: (0, i, 0))
        out_shape = jax.ShapeDtypeStruct((N_DEV, M, nb), BF16)
    else:
        out_spec = pl.BlockSpec((tm, N), lambda i, k: (i, 0))
        out_shape = jax.ShapeDtypeStruct((M, N), BF16)
    out = pl.pallas_call(
        body, name=name, grid=(M // tm, T // tk),
        in_specs=[pl.BlockSpec((tk, tm), lambda i, k: (k, i)), pl.BlockSpec((tk, N), lambda i, k: (k, 0))],
        out_specs=out_spec, out_shape=out_shape,
        scratch_shapes=[pltpu.VMEM((tm, N), F32)],
        compiler_params=_params("parallel", "arbitrary"),
    )(a, b)
    return out if col_sharded else out.reshape(N_DEV, M // N_DEV, N)


def _merge_bwd(dx1, attn, u, rest, yc, wa, wc, bc, wo, lg, lb, comm=None):
    T, D = dx1.shape
    C = CONV_C
    tm = _tile(T, 512)
    gcol = 2 * CONV_C // D

    def body(dx_ref, attn_ref, u_ref, ga_ref, gc_ref, yc_ref, wa_ref, wc_ref, bc_ref, wo_ref, lg_ref, lb_ref,
             dattn_ref, dyc_ref, dga_ref, dgc_ref, dbra_ref, dbrc_ref, dbc_ref, dlg_ref, dlb_ref):
        @pl.when(pl.program_id(0) == 0)
        def _():
            dbc_ref[...] = jnp.zeros_like(dbc_ref)
            dlg_ref[...] = jnp.zeros_like(dlg_ref)
            dlb_ref[...] = jnp.zeros_like(dlb_ref)

        dm = _dot_nt(dx_ref[...].astype(BF16), wo_ref[...])
        bra = _dot(attn_ref[...], wa_ref[...])
        brc = _dot(u_ref[...], wc_ref[...]) + bc_ref[...]
        sa = _sigmoid(ga_ref[...].astype(F32))
        sc = _sigmoid(gc_ref[...].astype(F32))
        dbra = dm * sa
        dbrc = dm * sc
        dga_ref[...] = (dm * bra * sa * (1.0 - sa)).astype(BF16)
        dgc_ref[...] = (dm * brc * sc * (1.0 - sc)).astype(BF16)
        dbra_b = dbra.astype(BF16)
        dbrc_b = dbrc.astype(BF16)
        dbra_ref[...] = dbra_b
        dbrc_ref[...] = dbrc_b
        dbc_ref[...] += _rows8(dbrc)
        dattn_ref[...] = _dot_nt(dbra_b, wa_ref[...]).astype(BF16)
        dyc, dlg, dlb = _swish_norm_bwd(_dot_nt(dbrc_b, wc_ref[...]), yc_ref[...], lg_ref[...], lb_ref[...])
        dyc_ref[...] = dyc
        dlg_ref[...] += _rows8(dlg)
        dlb_ref[...] += _rows8(dlb)

    return _pallas(
        body, name="merge_bwd", grid=(T // tm,),
        in_specs=[_row_spec(tm, D), _row_spec(tm, ATTN_W), _row_spec(tm, C), _row_spec(tm, D, gcol),
                  _row_spec(tm, D, gcol + 1), _row_spec(tm, C), _weight_spec((ATTN_W, D)), _weight_spec((C, D)),
                  _full_spec((1, D)), _weight_spec((D, D)), _full_spec((1, C)), _full_spec((1, C))],
        out_specs=[_row_spec(tm, ATTN_W), _row_spec(tm, C), _row_spec(tm, D), _row_spec(tm, D),
                   _row_spec(tm, D), _row_spec(tm, D), _full_spec((SUBLANES, D)), _full_spec((SUBLANES, C)),
                   _full_spec((SUBLANES, C))],
        out_shape=[jax.ShapeDtypeStruct((T, ATTN_W), BF16), jax.ShapeDtypeStruct((T, C), F32),
                   jax.ShapeDtypeStruct((T, D), BF16), jax.ShapeDtypeStruct((T, D), BF16),
                   jax.ShapeDtypeStruct((T, D), BF16), jax.ShapeDtypeStruct((T, D), BF16),
                   jax.ShapeDtypeStruct((SUBLANES, D), F32), jax.ShapeDtypeStruct((SUBLANES, C), F32),
                   jax.ShapeDtypeStruct((SUBLANES, C), F32)],
        args=(dx1, attn, u, rest, rest, yc, wa, wc, bc, wo, lg, lb), sem=("arbitrary",), comm=comm)


def _swish_norm_bwd(du, yv, g, b):
    xc = yv - jnp.mean(yv, axis=-1, keepdims=True)
    rstd = lax.rsqrt(jnp.mean(xc * xc, axis=-1, keepdims=True) + EPS)
    xn = xc * rstd
    ln = xn * g + b
    sg = _sigmoid(ln)
    dln = du * sg * (1.0 + ln * (1.0 - sg))
    dxn = dln * g
    dyc = rstd * (dxn - jnp.mean(dxn, axis=-1, keepdims=True) - xn * jnp.mean(dxn * xn, axis=-1, keepdims=True))
    return dyc, dln * xn, dln


def _conv_taps_bwd(first, last, dy_ref, dyn_ref, u0_ref, u0p_ref, glu_ref, w_ref, dglu_ref, dw_ref, db_ref,
                   dbuf, ubuf):
    tm, C = dy_ref.shape
    R = _tile(tm, CONV_ROWS)
    dbuf[0, 0:tm, :] = dy_ref[...]
    dbuf[0, tm:, :] = jnp.where(last, 0.0, dyn_ref[...])
    ubuf[0, 0:HALO, :] = jnp.where(first, 0.0, u0p_ref[...])
    ubuf[0, HALO:, :] = u0_ref[...]
    _shifted_copies(dbuf)
    _shifted_copies(ubuf)
    off = HALO - (CONV_K - 1)
    for c in range(tm // R):
        rows = slice(c * R, (c + 1) * R)
        dy = dbuf[0, rows, :]
        acc = jnp.zeros((R, C), F32)
        for j in range(CONV_K):
            acc = acc + w_ref[j:j + 1, :] * _shifted_rows(dbuf, c * R + CONV_K - 1 - j, R)
            dw_ref[j * SUBLANES:(j + 1) * SUBLANES, :] += _rows8(dy * _shifted_rows(ubuf, c * R + off + j, R))
        db_ref[...] += _rows8(dy)
        a = glu_ref[rows, :C].astype(F32)
        sb = _sigmoid(glu_ref[rows, C:].astype(F32))
        dglu_ref[rows, :C] = (acc * sb).astype(BF16)
        dglu_ref[rows, C:] = (acc * a * sb * (1.0 - sb)).astype(BF16)


def _attn_bwd(qkv, dattn, sinks, comm=None):
    T = qkv.shape[0]
    tq = _tile(T, 512)
    nblk = tq // BLOCK
    scale = 1.0 / math.sqrt(HEAD_DIM)

    def body(sink_ref, cur_ref, prev_ref, do_ref, dq_ref, hi_ref, lo_ref, ds_ref, kv_buf):
        i = pl.program_id(0)

        @pl.when(i == 0)
        def _():
            ds_ref[...] = jnp.zeros_like(ds_ref)

        _fill_kv(kv_buf, cur_ref, prev_ref)
        upper, distf, keep = _fold_masks(i == 0)
        bias, sink = _head_consts(sink_ref, distf)
        for j in range(nblk):
            rows = slice(j * BLOCK, (j + 1) * BLOCK)
            band = kv_buf[j * BLOCK:(j + 2) * BLOCK, :]
            q = cur_ref[rows, :ATTN_W]
            do = do_ref[rows, :]
            dqs, dks, dvs = [], [], []
            for kh in range(N_KV):
                k = band[:, kh * HEAD_DIM:(kh + 1) * HEAD_DIM]
                v = band[:, KV_W + kh * HEAD_DIM:KV_W + (kh + 1) * HEAD_DIM]
                qg = _group_rows(q, kh)
                dog = _group_rows(do, kh)
                p, psink = _attn_probs(_dot_nt(k, qg), bias[kh], sink[kh], upper, keep if j == 0 else None)
                pdp = p * _fold(_dot_nt(v, dog), upper)
                delta = jnp.sum(pdp, axis=0, keepdims=True)
                lanes = slice(kh * GROUP * BLOCK, (kh + 1) * GROUP * BLOCK)
                ds_ref[0:1, lanes] += psink * delta
                dsb = _unfold(pdp - p * delta, upper).astype(BF16)
                dqs.append(_heads_out(_dot_tn(k, dsb)))
                dks.append(_dot(dsb, qg) * scale)
                dvs.append(_dot(_unfold(p, upper).astype(BF16), dog))
            dq_ref[rows, :] = jnp.concatenate(dqs, axis=1).astype(BF16)
            dkv = jnp.concatenate(dks + dvs, axis=1)
            lo_ref[rows, :] = dkv[:BLOCK, :]
            hi_ref[rows, :] = dkv[BLOCK:, :]

    return _pallas(
        body, name="attn_bwd", grid=(T // tq,),
        in_specs=[pl.BlockSpec(memory_space=pltpu.SMEM),
                  _row_spec(tq, QKV_W),
                  pl.BlockSpec((BLOCK, QKV_W), lambda i: (jnp.maximum(i * nblk - 1, 0), 0)),
                  _row_spec(tq, ATTN_W)],
        out_specs=[_row_spec(tq, ATTN_W), _row_spec(tq, 2 * KV_W), _row_spec(tq, 2 * KV_W),
                   _full_spec((SUBLANES, N_Q * BLOCK))],
        out_shape=[jax.ShapeDtypeStruct((T, ATTN_W), BF16), jax.ShapeDtypeStruct((T, 2 * KV_W), F32),
                   jax.ShapeDtypeStruct((T, 2 * KV_W), F32), jax.ShapeDtypeStruct((SUBLANES, N_Q * BLOCK), F32)],
        scratch_shapes=[pltpu.VMEM((tq + BLOCK, 2 * KV_W), BF16)],
        args=(sinks, qkv, qkv, dattn), sem=("arbitrary",), comm=comm)


def _inproj_bwd(dq, hi, lo, dyc, u0, rest, cw, dga, dgc, x, g, w, dx1, comm=None):
    T, D = x.shape
    C = CONV_C
    tm = _tile(T, 256)
    per = tm // BLOCK
    per_halo = tm // HALO
    nt = T // tm
    kv2 = 2 * KV_W
    glu0, gate0 = QKV_W, QKV_W + 2 * C

    def body(dq_ref, hi_ref, lo_ref, lon_ref, dy_ref, dyn_ref, u0_ref, u0p_ref, glu_ref, cw_ref, dga_ref, dgc_ref,
             x_ref, g_ref, w_ref, dx1_ref, dp_ref, dx_ref, dg_ref, dbias_ref, dcw_ref, dcb_ref, dbuf, ubuf):
        i = pl.program_id(0)

        @pl.when(i == 0)
        def _():
            for ref in (dg_ref, dbias_ref, dcw_ref, dcb_ref):
                ref[...] = jnp.zeros_like(ref)

        def part(cols):
            dp = dp_ref[:, cols]
            dbias_ref[:, cols] += _rows8(dp.astype(F32))
            return _dot_nt(dp, w_ref[:, cols])

        dp_ref[:, :ATTN_W] = dq_ref[...]
        lo_next = jnp.where(i < nt - 1, lon_ref[...], 0.0)
        lo_shift = jnp.concatenate([lo_ref[BLOCK:, :], lo_next], axis=0) if tm > BLOCK else lo_next
        dp_ref[:, ATTN_W:QKV_W] = (hi_ref[...] + lo_shift).astype(BF16)
        dp_ref[:, gate0:gate0 + D] = dga_ref[...]
        dp_ref[:, gate0 + D:] = dgc_ref[...]
        _conv_taps_bwd(i == 0, i == nt - 1, dy_ref, dyn_ref, u0_ref, u0p_ref, glu_ref, cw_ref,
                       dp_ref.at[:, glu0:gate0], dcw_ref, dcb_ref, dbuf, ubuf)
        dh = part(slice(0, QKV_W)) + part(slice(gate0, IN_W)) + part(slice(glu0, gate0))
        dx, dg = _rms_bwd(x_ref[...], g_ref[...], dh)
        dx_ref[...] = dx1_ref[...] + dx
        dg_ref[...] += _rows8(dg)

    return _pallas(
        body, name="inproj_bwd", grid=(nt,),
        in_specs=[_row_spec(tm, ATTN_W), _row_spec(tm, kv2), _row_spec(tm, kv2),
                  pl.BlockSpec((BLOCK, kv2), lambda i: (jnp.minimum((i + 1) * per, T // BLOCK - 1), 0)),
                  _row_spec(tm, C),
                  pl.BlockSpec((HALO, C), lambda i: (jnp.minimum((i + 1) * per_halo, T // HALO - 1), 0)),
                  _row_spec(tm, C),
                  pl.BlockSpec((HALO, C), lambda i: (jnp.maximum(i * per_halo - 1, 0), 0)),
                  _row_spec(tm, 2 * C), _full_spec((CONV_K, C)),
                  _row_spec(tm, D), _row_spec(tm, D), _row_spec(tm, D), _full_spec((1, D)),
                  _weight_spec((D, IN_W)), _row_spec(tm, D)],
        out_specs=[_row_spec(tm, IN_W), _row_spec(tm, D), _full_spec((SUBLANES, D)), _full_spec((SUBLANES, IN_W)),
                   _full_spec((CONV_K * SUBLANES, C)), _full_spec((SUBLANES, C))],
        out_shape=[jax.ShapeDtypeStruct((T, IN_W), BF16), jax.ShapeDtypeStruct((T, D), F32),
                   jax.ShapeDtypeStruct((SUBLANES, D), F32), jax.ShapeDtypeStruct((SUBLANES, IN_W), F32),
                   jax.ShapeDtypeStruct((CONV_K * SUBLANES, C), F32), jax.ShapeDtypeStruct((SUBLANES, C), F32)],
        scratch_shapes=[pltpu.VMEM((SUBLANES, tm + HALO, C), F32), pltpu.VMEM((SUBLANES, tm + HALO, C), F32)],
        args=(dq, hi, lo, lo, dyc, dyc, u0, u0, rest, cw, dga, dgc, x, g, w, dx1), sem=("arbitrary",), comm=comm)


def _adamw_math(g, w, m, v):
    c1 = 1.0 / (1.0 - ADAM_B1 ** ADAM_STEP)
    c2 = 1.0 / (1.0 - ADAM_B2 ** ADAM_STEP)
    mn = ADAM_B1 * m + (1.0 - ADAM_B1) * g
    vn = ADAM_B2 * v + (1.0 - ADAM_B2) * (g * g)
    return -ADAM_LR * ((mn * c1) / (jnp.sqrt(vn * c2) + ADAM_EPS) + ADAM_WD * w), mn, vn


def _adamw_sharded(parts, w, m, v, name):
    depth, a, b = w.shape
    tr = _tile(a, 256) if a % SUBLANES == 0 else a
    nr = a // tr

    def body(*refs):
        p_refs, (w_ref, m_ref, v_ref, g_ref, d_ref, mo_ref, vo_ref) = refs[:depth], refs[depth:]
        layer = pl.program_id(0)
        for l in range(depth):
            @pl.when(layer == l)
            def _(l=l):
                g = p_refs[l][0].astype(F32)
                for s in range(1, N_DEV):
                    g = g + p_refs[l][s].astype(F32)
                g_ref[...] = g
                d_ref[...], mo_ref[...], vo_ref[...] = _adamw_math(g, w_ref[...], m_ref[...], v_ref[...])

    def part_spec(l):
        return pl.BlockSpec((N_DEV, tr, b),
                            lambda k, i: (0, jnp.where(k == l, i, jnp.where(k < l, 0, nr - 1)), 0))

    spec = pl.BlockSpec((None, tr, b), lambda k, i: (k, i, 0))
    out = jax.ShapeDtypeStruct((depth, a, b), F32)
    return pl.pallas_call(
        body, name=name, grid=(depth, nr),
        in_specs=[part_spec(l) for l in range(depth)] + [spec] * 3,
        out_specs=[spec] * 4, out_shape=[out] * 4,
        compiler_params=_params("arbitrary", "arbitrary"),
    )(*parts, w, m, v)


def _adamw_small(parts, w, m, v):
    R, N = w.shape

    def body(p_ref, w_ref, m_ref, v_ref, g_ref, d_ref, mo_ref, vo_ref):
        g = p_ref[0]
        for s in range(1, N_DEV):
            g = g + p_ref[s]
        g_ref[...] = g
        d_ref[...], mo_ref[...], vo_ref[...] = _adamw_math(g, w_ref[...], m_ref[...], v_ref[...])

    out = jax.ShapeDtypeStruct((R, N), F32)
    return pl.pallas_call(
        body, name="adamw_small", grid=(1,),
        in_specs=[_full_spec((N_DEV, R, N))] + [_full_spec((R, N))] * 3,
        out_specs=[_full_spec((R, N))] * 4, out_shape=[out] * 4,
        compiler_params=_params("arbitrary"),
    )(parts, w, m, v)


_SHARDED = ("w_in", "conv_w", "w_attn_proj", "w_conv_proj", "w_out", "w_mlp1", "w_mlp2")
_ROW_SHARDED = ("w_out", "w_mlp2")
_FIRST = ("w_in", "conv_w")
_REST = tuple(n for n in _SHARDED if n not in _FIRST)
_SMALL = ("mix_norm_g", "b_in", "sinks", "conv_b", "conv_ln_g", "conv_ln_b", "b_conv_proj", "mlp_norm_g",
          "final_norm_g")
_ORDER = ("mix_norm_g", "w_in", "b_in", "sinks", "conv_w", "conv_b", "conv_ln_g", "conv_ln_b", "w_attn_proj",
          "w_conv_proj", "b_conv_proj", "w_out", "mlp_norm_g", "w_mlp1", "w_mlp2", "final_norm_g")
_PACK = 1024


def _full_weight(name, gathered):
    _, a, b = gathered.shape
    if name in _ROW_SHARDED:
        return gathered.reshape(N_DEV * a, b)
    return gathered.transpose(1, 0, 2).reshape(a, N_DEV * b)


def _pack(arrs):
    flat = []
    for a in arrs:
        a = a.reshape(-1)
        flat.append(jnp.pad(a, (0, -a.size % _PACK)))
    return jnp.concatenate(flat).reshape(-1, BLOCK)


def _unpack(packed, shapes):
    flat = packed.reshape(-1)
    out, off = [], 0
    for s in shapes:
        n = math.prod(s)
        out.append(flat[off:off + n].reshape(s))
        off += n + (-n % _PACK)
    return out


def _layer_fwd(x, lw, own_rest=None, comm=None):
    (h, qkv, rest), got = _inproj_fwd(x, lw["mix_norm_g"], lw["w_in"], lw["b_in"], own_rest)
    if got is not None:
        lw.update({n: _full_weight(n, a) for n, a in zip(_REST, got)})
    attn = _attn_fwd(qkv, lw["sinks"])
    u0, yc, u = _conv_fwd(rest, lw["conv_w"], lw["conv_b"], lw["conv_ln_g"], lw["conv_ln_b"])
    merged, x1 = _merge_fwd(attn, u, rest, x, lw["w_attn_proj"], lw["w_conv_proj"], lw["b_conv_proj"], lw["w_out"])
    (h2, z, x2), gathered = _mlp_fwd(x1, lw["mlp_norm_g"], lw["w_mlp1"], lw["w_mlp2"], comm)
    saved = dict(x=x, h=h, qkv=qkv, rest=rest, attn=attn, u0=u0, yc=yc, u=u, merged=merged, x1=x1, h2=h2, z=z)
    return x2, saved, gathered


_EARLY = ("w_mlp1", "w_mlp2")
_MIDDLE = ("w_out", "w_attn_proj", "w_conv_proj")
_LATE = ("w_in", "conv_w")


def _layer_bwd(dx2, lw, s, late_blocks, dx_is_result):
    g, recv = {}, {}
    late = None if late_blocks is None else _Exchange(late_blocks)
    (dx1, dz, dg2), late_recv = _mlp_bwd(dx2, s["x1"], s["z"], lw["mlp_norm_g"], lw["w_mlp1"], lw["w_mlp2"], late)
    g["mlp_norm_g"] = jnp.sum(dg2, axis=0)
    early = _Exchange([_tn_blocks(s["h2"], dz, "dw_mlp1", True),
                       _tn_blocks(s["z"], dx2, "dw_mlp2", False, relu_sq=True)])
    (dattn, dyc, dga, dgc, dbra, dbrc, dbc, dlg, dlb), early_recv = _merge_bwd(
        dx1, s["attn"], s["u"], s["rest"], s["yc"], lw["w_attn_proj"], lw["w_conv_proj"], lw["b_conv_proj"],
        lw["w_out"], lw["conv_ln_g"], lw["conv_ln_b"], early if dx_is_result else None)
    g["b_conv_proj"] = jnp.sum(dbc, axis=0)
    g["conv_ln_g"] = jnp.sum(dlg, axis=0)
    g["conv_ln_b"] = jnp.sum(dlb, axis=0)
    middle = [_tn_blocks(s["merged"], dx1, "dw_out", False), _tn_blocks(s["attn"], dbra, "dw_attn_proj", True),
              _tn_blocks(s["u"], dbrc, "dw_conv_proj", True)]
    (dq, hi, lo, dsk), middle_recv = _attn_bwd(s["qkv"], dattn, lw["sinks"], _Exchange(middle))
    recv.update(zip(_MIDDLE, middle_recv))
    g["sinks"] = -jnp.sum(dsk[0].reshape(N_Q, BLOCK), axis=1)
    (dproj, dx, dg1, dbin, dcw, dcb), behind_inproj = _inproj_bwd(
        dq, hi, lo, dyc, s["u0"], s["rest"], lw["conv_w"], dga, dgc, s["x"], lw["mix_norm_g"], lw["w_in"], dx1,
        None if dx_is_result else early)
    recv.update(zip(_EARLY, early_recv if dx_is_result else behind_inproj))
    dconv_w = jnp.sum(dcw.reshape(CONV_K, SUBLANES, CONV_C), axis=1)
    g["conv_b"] = jnp.sum(dcb, axis=0)
    g["mix_norm_g"] = jnp.sum(dg1, axis=0)
    g["b_in"] = jnp.sum(dbin, axis=0)
    own_late = [_tn_blocks(s["h"], dproj, "dw_in", True),
                dconv_w.reshape(CONV_K, N_DEV, CONV_C // N_DEV).transpose(1, 0, 2)]
    return dx, g, recv, late_recv, own_late


def kernel(x, mix_norm_g, w_in, b_in, sinks, conv_w, conv_b, conv_ln_g, conv_ln_b, w_attn_proj, w_conv_proj, b_conv_proj, w_out, mlp_norm_g, w_mlp1, w_mlp2, final_norm_g, loss_target, m_mix_norm_g, m_w_in, m_b_in, m_sinks, m_conv_w, m_conv_b, m_conv_ln_g, m_conv_ln_b, m_w_attn_proj, m_w_conv_proj, m_b_conv_proj, m_w_out, m_mlp_norm_g, m_w_mlp1, m_w_mlp2, m_final_norm_g, v_mix_norm_g, v_w_in, v_b_in, v_sinks, v_conv_w, v_conv_b, v_conv_ln_g, v_conv_ln_b, v_w_attn_proj, v_w_conv_proj, v_b_conv_proj, v_w_out, v_mlp_norm_g, v_w_mlp1, v_w_mlp2, v_final_norm_g):
    w = dict(mix_norm_g=mix_norm_g, w_in=w_in, b_in=b_in, sinks=sinks, conv_w=conv_w, conv_b=conv_b,
             conv_ln_g=conv_ln_g, conv_ln_b=conv_ln_b, w_attn_proj=w_attn_proj, w_conv_proj=w_conv_proj,
             b_conv_proj=b_conv_proj, w_out=w_out, mlp_norm_g=mlp_norm_g, w_mlp1=w_mlp1, w_mlp2=w_mlp2,
             final_norm_g=final_norm_g)
    m = dict(mix_norm_g=m_mix_norm_g, w_in=m_w_in, b_in=m_b_in, sinks=m_sinks, conv_w=m_conv_w, conv_b=m_conv_b,
             conv_ln_g=m_conv_ln_g, conv_ln_b=m_conv_ln_b, w_attn_proj=m_w_attn_proj, w_conv_proj=m_w_conv_proj,
             b_conv_proj=m_b_conv_proj, w_out=m_w_out, mlp_norm_g=m_mlp_norm_g, w_mlp1=m_w_mlp1, w_mlp2=m_w_mlp2,
             final_norm_g=m_final_norm_g)
    v = dict(mix_norm_g=v_mix_norm_g, w_in=v_w_in, b_in=v_b_in, sinks=v_sinks, conv_w=v_conv_w, conv_b=v_conv_b,
             conv_ln_g=v_conv_ln_g, conv_ln_b=v_conv_ln_b, w_attn_proj=v_w_attn_proj, w_conv_proj=v_w_conv_proj,
             b_conv_proj=v_b_conv_proj, w_out=v_w_out, mlp_norm_g=v_mlp_norm_g, w_mlp1=v_w_mlp1, w_mlp2=v_w_mlp2,
             final_norm_g=v_final_norm_g)
    T = x.shape[1]
    xs = x.reshape(T, D_MODEL)
    target = loss_target.reshape(T, D_MODEL)

    def gather_of(l, names):
        return _Gather([w[n][l] if n == "conv_w" else w[n][l].astype(BF16) for n in names])

    def layer_weights(l, names, gathered):
        lw = {n: _full_weight(n, a) for n, a in zip(names, gathered)}
        for n in _SMALL:
            if n != "final_norm_g":
                lw[n] = w[n][l] if n == "sinks" else w[n][l].reshape(1, -1)
        return lw

    acts = xs
    saved, weights = [], []
    for l in range(DEPTH):
        following = gather_of(l + 1, _SHARDED) if l + 1 < DEPTH else None
        if l == 0:
            lw = layer_weights(0, _FIRST, _run_comm(gather_of(0, _FIRST), "gather_first"))
            acts, s, gathered = _layer_fwd(acts, lw, gather_of(0, _REST), following)
        else:
            lw = layer_weights(l, _SHARDED, gathered)
            acts, s, gathered = _layer_fwd(acts, lw, None, following)
        weights.append(lw)
        saved.append(s)
    lterms, dx, dgf = _final_loss(acts, final_norm_g.reshape(1, -1), target)
    grads, received = [None] * DEPTH, [None] * DEPTH
    late = None
    for l in reversed(range(DEPTH)):
        dx, grads[l], received[l], late_recv, late = _layer_bwd(dx, weights[l], saved[l], late, l == 0)
        if late_recv is not None:
            received[l + 1].update(zip(_LATE, late_recv))
    received[0].update(zip(_LATE, _run_comm(_Exchange(late), "scatter_late")))
    grad = {n: jnp.stack([grads[l][n] for l in range(DEPTH)]) for n in _SMALL if n != "final_norm_g"}
    grad["final_norm_g"] = jnp.sum(dgf, axis=0)

    small_shapes = [w[n].shape for n in _SMALL] + [(1,)]
    small = _pack([grad[n] for n in _SMALL] + [jnp.sum(lterms).reshape(1)])
    small_parts = _run_comm(_Gather([small]), "gather_small")[0]

    out_g, out_d, out_m, out_v = {}, {}, {}, {}
    for n in _SHARDED:
        out_g[n], out_d[n], out_m[n], out_v[n] = _adamw_sharded(
            [received[l][n] for l in range(DEPTH)], w[n], m[n], v[n], "adamw_" + n)
    zero = jnp.zeros((1,), F32)
    res = _adamw_small(small_parts, _pack([w[n] for n in _SMALL] + [zero]), _pack([m[n] for n in _SMALL] + [zero]),
                       _pack([v[n] for n in _SMALL] + [zero]))
    unpacked = [_unpack(r, small_shapes) for r in res]
    for i, n in enumerate(_SMALL):
        out_g[n], out_d[n], out_m[n], out_v[n] = (u[i] for u in unpacked)
    loss = unpacked[0][-1].reshape(())
    return (loss, dx.reshape(x.shape), *[out_g[n] for n in _ORDER], *[out_d[n] for n in _ORDER],
            *[out_m[n] for n in _ORDER], *[out_v[n] for n in _ORDER])
```

```python
import functools
import math

import jax
import jax.numpy as jnp
from jax import lax
from jax.experimental import pallas as pl
from jax.experimental.pallas import tpu as pltpu

D_MODEL = 1024
SEQ = 16384
DEPTH = 2
N_Q = 8
N_KV = 2
GROUP = N_Q // N_KV
HEAD_DIM = 64
ATTN_W = N_Q * HEAD_DIM
KV_W = N_KV * HEAD_DIM
BLOCK = 128
CONV_C = D_MODEL // 2
CONV_K = 31
D_FF = 4 * D_MODEL
QKV_W = ATTN_W + 2 * KV_W
IN_W = QKV_W + 2 * CONV_C + 2 * D_MODEL
EPS = 1e-6
NEG = -1e30
N_DEV = 8

ADAM_LR = 0.001
ADAM_B1 = 0.9
ADAM_B2 = 0.999
ADAM_EPS = 1e-08
ADAM_WD = 0.01
ADAM_STEP = 10

F32 = jnp.float32
BF16 = jnp.bfloat16
MESH = pl.DeviceIdType.MESH

SUBLANES = 8
HALO = 32
FF_CHUNK = 1024
CONV_ROWS = 16
VMEM_LIMIT = 52 * 1024 * 1024

_NT = (((1,), (1,)), ((), ()))
_TN = (((0,), (0,)), ((), ()))


def _params(*sem):
    return pltpu.CompilerParams(dimension_semantics=sem, vmem_limit_bytes=VMEM_LIMIT)


def _tile(n, pref):
    t = min(n, pref)
    assert n % t == 0, (n, t)
    return t


def _sigmoid(v):
    return 1.0 / (1.0 + jnp.exp(-v))


def _rows8(v):
    r, n = v.shape
    return jnp.sum(v.reshape(r // SUBLANES, SUBLANES, n), axis=0)


def _dot(a, b):
    return jnp.dot(a, b, preferred_element_type=F32)


def _dot_nt(a, b):
    return lax.dot_general(a, b, _NT, preferred_element_type=F32)


def _dot_tn(a, b):
    return lax.dot_general(a, b, _TN, preferred_element_type=F32)


def _rms_bwd(xv, g, dh):
    r = lax.rsqrt(jnp.mean(xv * xv, axis=-1, keepdims=True) + EPS)
    xhat = xv * r
    dxhat = dh * g
    dx = r * (dxhat - xhat * jnp.mean(dxhat * xhat, axis=-1, keepdims=True))
    return dx, dh * xhat


def _row_spec(tm, n, col=0):
    return pl.BlockSpec((tm, n), lambda i: (i, col))


def _full_spec(shape):
    return pl.BlockSpec(shape, lambda *_: (0,) * len(shape))


def _weight_spec(shape):
    return pl.BlockSpec(shape, lambda *_: (0,) * len(shape), pipeline_mode=pl.Buffered(1))


def _mesh_pos():
    return lax.axis_index("x"), lax.axis_index("y"), lax.axis_index("c")


def _dev_index(dev):
    return 4 * dev[0] + 2 * dev[1] + dev[2]


class _Exchange:
    middle_at = None

    def __init__(self, arrs):
        self.arrays = list(arrs)

    def out_shape(self):
        return [jax.ShapeDtypeStruct(a.shape, a.dtype) for a in self.arrays]

    def scratch(self):
        n = len(self.arrays)
        return [pltpu.SemaphoreType.DMA((7 * n,)), pltpu.SemaphoreType.DMA((7 * n,)), pltpu.SemaphoreType.DMA((n,))]

    def _copies(self, ins, outs, sems):
        send_sems, recv_sems, local_sems = sems
        x, y, c = _mesh_pos()
        me = _dev_index((x, y, c))
        mine, sends, arrivals = [], [], []
        for p in range(len(self.arrays)):
            mine.append(pltpu.make_async_copy(ins[p].at[me], outs[p].at[me], local_sems.at[p]))
            for k in range(1, N_DEV):
                peer = (1 - x if k & 4 else x, 1 - y if k & 2 else y, 1 - c if k & 1 else c)
                pid = _dev_index(peer)
                pair = dict(send_sem=send_sems.at[7 * p + k - 1], recv_sem=recv_sems.at[7 * p + k - 1],
                            device_id=peer, device_id_type=MESH)
                sends.append(pltpu.make_async_remote_copy(src_ref=ins[p].at[pid], dst_ref=outs[p].at[me], **pair))
                arrivals.append(pltpu.make_async_remote_copy(src_ref=ins[p].at[pid], dst_ref=outs[p].at[pid], **pair))
        return mine, sends, arrivals

    def start(self, ins, outs, sems):
        mine, sends, _ = self._copies(ins, outs, sems)
        for cp in mine + sends:
            cp.start()

    def finish(self, ins, outs, sems):
        mine, sends, arrivals = self._copies(ins, outs, sems)
        for cp in arrivals:
            cp.wait_recv()
        for cp in sends:
            cp.wait_send()
        for cp in mine:
            cp.wait()


class _Gather:
    middle_at = 0.75

    def __init__(self, arrs):
        self.arrays = list(arrs)

    def out_shape(self):
        return [jax.ShapeDtypeStruct((N_DEV,) + a.shape, a.dtype) for a in self.arrays]

    def scratch(self):
        n = len(self.arrays)
        return [pltpu.SemaphoreType.DMA((7 * n,)), pltpu.SemaphoreType.DMA((7 * n,)), pltpu.SemaphoreType.DMA((n,))]

    def _copies(self, ins, outs, sems):
        send_sems, recv_sems, local_sems = sems
        x, y, c = _mesh_pos()
        me, sibling = (x, y, c), (x, y, 1 - c)
        chips = [(1 - x, y), (x, 1 - y), (1 - x, 1 - y)]
        n = len(self.arrays)

        def copy(p, k, dev, to, src=None):
            block = outs[p].at[_dev_index(dev)]
            return pltpu.make_async_remote_copy(
                src_ref=block if src is None else src, dst_ref=block,
                send_sem=send_sems.at[7 * p + k], recv_sem=recv_sems.at[7 * p + k],
                device_id=to, device_id_type=MESH)

        cp = dict(mine=[pltpu.make_async_copy(ins[p], outs[p].at[_dev_index(me)], local_sems.at[p])
                        for p in range(n)])
        cp["first"] = [copy(p, 0, me, sibling, src=ins[p]) for p in range(n)]
        cp["first"] += [copy(p, 1 + j, me, (*chip, c), src=ins[p]) for p in range(n) for j, chip in enumerate(chips)]
        cp["over_ici"] = [copy(p, 1 + j, (*chip, c), me) for j, chip in enumerate(chips) for p in range(n)]
        cp["passed"] = [copy(p, 4 + j, (*chip, c), sibling) for j, chip in enumerate(chips) for p in range(n)]
        cp["from_sibling"] = [copy(p, 0, sibling, me) for p in range(n)]
        cp["from_sibling"] += [copy(p, 4 + j, (*chip, 1 - c), me) for j, chip in enumerate(chips) for p in range(n)]
        return cp

    def start(self, ins, outs, sems):
        cp = self._copies(ins, outs, sems)
        for d in cp["mine"] + cp["first"]:
            d.start()

    def middle(self, ins, outs, sems):
        cp = self._copies(ins, outs, sems)
        for arrived, onward in zip(cp["over_ici"], cp["passed"]):
            arrived.wait_recv()
            onward.start()

    def finish(self, ins, outs, sems):
        cp = self._copies(ins, outs, sems)
        for d in cp["from_sibling"]:
            d.wait_recv()
        for d in cp["first"] + cp["passed"]:
            d.wait_send()
        for d in cp["mine"]:
            d.wait()


def _run_comm(comm, name):
    n = len(comm.arrays)

    def body(*refs):
        ins, outs, sems = refs[:n], refs[n:2 * n], refs[2 * n:]
        comm.start(ins, outs, sems)
        if comm.middle_at is not None:
            comm.middle(ins, outs, sems)
        comm.finish(ins, outs, sems)

    any_spec = pl.BlockSpec(memory_space=pl.ANY)
    return pl.pallas_call(
        body, name=name, in_specs=[any_spec] * n, out_specs=[any_spec] * n, out_shape=comm.out_shape(),
        scratch_shapes=comm.scratch(),
    )(*comm.arrays)


def _pallas(body, *, name, grid, in_specs, out_specs, out_shape, args, sem, scratch_shapes=(), comm=None):
    if comm is None:
        outs = pl.pallas_call(
            body, name=name, grid=grid, in_specs=in_specs, out_specs=out_specs, out_shape=out_shape,
            scratch_shapes=list(scratch_shapes), compiler_params=_params(*sem),
        )(*args)
        return outs, None
    n_in, n_out, n_scr, n_c = len(in_specs), len(out_specs), len(scratch_shapes), len(comm.arrays)
    steps = grid[0]
    middle = None if comm.middle_at is None else min(steps - 1, int(steps * comm.middle_at))

    def carried(*refs):
        ins, refs = refs[:n_in], refs[n_in:]
        cins, refs = refs[:n_c], refs[n_c:]
        outs, refs = refs[:n_out], refs[n_out:]
        couts, refs = refs[:n_c], refs[n_c:]
        scr, csems = refs[:n_scr], refs[n_scr:]
        step = pl.program_id(0)

        @pl.when(step == 0)
        def _():
            comm.start(cins, couts, csems)

        body(*ins, *outs, *scr)

        if middle is not None:
            @pl.when(step == middle)
            def _():
                comm.middle(cins, couts, csems)

        @pl.when(step == steps - 1)
        def _():
            comm.finish(cins, couts, csems)

    any_spec = pl.BlockSpec(memory_space=pl.ANY)
    res = pl.pallas_call(
        carried, name=name, grid=grid,
        in_specs=list(in_specs) + [any_spec] * n_c, out_specs=list(out_specs) + [any_spec] * n_c,
        out_shape=list(out_shape) + comm.out_shape(),
        scratch_shapes=list(scratch_shapes) + comm.scratch(),
        compiler_params=_params(*(("arbitrary",) + tuple(sem[1:]))),
    )(*args, *comm.arrays)
    return res[:n_out], res[n_out:]


def _inproj_fwd(x, g, w, b, cw, cb, lg, lb, comm=None):
    T, D = x.shape
    C = CONV_C
    rest_w = IN_W - QKV_W
    tm = _tile(T, 512)
    tp = _tile(tm, 256)

    def body(x_ref, g_ref, w_ref, b_ref, cw_ref, cb_ref, lg_ref, lb_ref, h_ref, qkv_ref, rest_ref, u0_ref, yc_ref,
             u_ref, ubuf, carry):
        @pl.when(pl.program_id(0) == 0)
        def _():
            carry[...] = jnp.zeros_like(carry)

        for part in range(tm // tp):
            rows = slice(part * tp, (part + 1) * tp)
            xv = x_ref[rows, :]
            r = lax.rsqrt(jnp.mean(xv * xv, axis=-1, keepdims=True) + EPS)
            h = (xv * r * g_ref[...]).astype(BF16)
            h_ref[rows, :] = h
            qkv_ref[rows, :] = (_dot(h, w_ref[:, :QKV_W]) + b_ref[:, :QKV_W]).astype(BF16)
            rest = (_dot(h, w_ref[:, QKV_W:]) + b_ref[:, QKV_W:]).astype(BF16)
            rest_ref[rows, :] = rest
            u0 = rest[:, :C].astype(F32) * _sigmoid(rest[:, C:2 * C].astype(F32))
            _conv_rows_fwd(u0, carry, ubuf, cw_ref, cb_ref, lg_ref, lb_ref, u0_ref.at[rows, :], yc_ref.at[rows, :],
                           u_ref.at[rows, :])

    return _pallas(
        body, name="inproj_fwd", grid=(T // tm,),
        in_specs=[_row_spec(tm, D), _full_spec((1, D)), _weight_spec((D, IN_W)), _full_spec((1, IN_W)),
                  _full_spec((CONV_K, C)), _full_spec((1, C)), _full_spec((1, C)), _full_spec((1, C))],
        out_specs=[_row_spec(tm, D), _row_spec(tm, QKV_W), _row_spec(tm, rest_w), _row_spec(tm, C),
                   _row_spec(tm, C), _row_spec(tm, C)],
        out_shape=[jax.ShapeDtypeStruct((T, D), BF16), jax.ShapeDtypeStruct((T, QKV_W), BF16),
                   jax.ShapeDtypeStruct((T, rest_w), BF16), jax.ShapeDtypeStruct((T, C), F32),
                   jax.ShapeDtypeStruct((T, C), F32), jax.ShapeDtypeStruct((T, C), BF16)],
        scratch_shapes=[pltpu.VMEM((SUBLANES, tp + HALO, C), F32), pltpu.VMEM((HALO, C), F32)],
        args=(x, g, w, b, cw, cb, lg, lb), sem=("arbitrary",), comm=comm)


def _fold_masks(first):
    shape = (BLOCK, GROUP * BLOCK)
    key = lax.broadcasted_iota(jnp.int32, shape, 0)
    qry = lax.broadcasted_iota(jnp.int32, shape, 1) & (BLOCK - 1)
    upper = key > qry
    dist = jnp.where(upper, qry + BLOCK - key, qry - key)
    keep = key <= qry + jnp.where(first, 0, BLOCK)
    return upper, dist.astype(F32), keep


def _fold(band, upper):
    return jnp.where(upper, band[:BLOCK, :], band[BLOCK:, :])


def _unfold(folded, upper):
    return jnp.concatenate([jnp.where(upper, folded, 0.0), jnp.where(upper, 0.0, folded)], axis=0)


def _head_row(values):
    return jnp.concatenate([jnp.full((1, BLOCK), v, F32) for v in values], axis=1)


def _head_consts(sink_ref, distf):
    bias, sink = [], []
    for kh in range(N_KV):
        heads = range(kh * GROUP, (kh + 1) * GROUP)
        bias.append(_head_row([2.0 ** (-8.0 * (h + 1) / N_Q) for h in heads]) * distf)
        sink.append(_head_row([sink_ref[h] for h in heads]))
    return bias, sink


def _heads_out(t):
    stacked = jnp.concatenate([t[:, g * BLOCK:(g + 1) * BLOCK] for g in range(GROUP)], axis=0)
    return stacked.T


def _fill_kv(kv_buf, cur_ref, prev_ref):
    scale = 1.0 / math.sqrt(HEAD_DIM)
    assert math.frexp(scale)[0] == 0.5
    for r0, ref in ((0, prev_ref), (BLOCK, cur_ref)):
        rows = ref.shape[0]
        kv_buf[r0:r0 + rows, :KV_W] = ref[:, ATTN_W:ATTN_W + KV_W] * scale
        kv_buf[r0:r0 + rows, KV_W:] = ref[:, ATTN_W + KV_W:]


def _group_rows(x, kh):
    return jnp.concatenate([x[:, h * HEAD_DIM:(h + 1) * HEAD_DIM] for h in range(kh * GROUP, (kh + 1) * GROUP)],
                           axis=0)


def _attn_probs(scores, bias, sink, upper, keep):
    s = _fold(scores, upper) - bias
    if keep is not None:
        s = jnp.where(keep, s, NEG)
    m = jnp.maximum(jnp.max(s, axis=0, keepdims=True), sink)
    p = jnp.exp(s - m)
    e = jnp.exp(sink - m)
    inv = 1.0 / (jnp.sum(p, axis=0, keepdims=True) + e)
    return p * inv, e * inv


def _attn_fwd(qkv, sinks):
    T = qkv.shape[0]
    tq = _tile(T, 512)
    nblk = tq // BLOCK

    def body(sink_ref, cur_ref, prev_ref, o_ref, kv_buf):
        _fill_kv(kv_buf, cur_ref, prev_ref)
        upper, distf, keep = _fold_masks(pl.program_id(0) == 0)
        bias, sink = _head_consts(sink_ref, distf)
        for j in range(nblk):
            band = kv_buf[j * BLOCK:(j + 2) * BLOCK, :]
            q = cur_ref[j * BLOCK:(j + 1) * BLOCK, :ATTN_W]
            outs = []
            for kh in range(N_KV):
                k = band[:, kh * HEAD_DIM:(kh + 1) * HEAD_DIM]
                v = band[:, KV_W + kh * HEAD_DIM:KV_W + (kh + 1) * HEAD_DIM]
                p, _ = _attn_probs(_dot_nt(k, _group_rows(q, kh)), bias[kh], sink[kh], upper,
                                   keep if j == 0 else None)
                outs.append(_heads_out(_dot_tn(v, _unfold(p, upper).astype(BF16))))
            o_ref[j * BLOCK:(j + 1) * BLOCK, :] = jnp.concatenate(outs, axis=1).astype(BF16)

    return pl.pallas_call(
        body, name="attn_fwd", grid=(T // tq,),
        in_specs=[pl.BlockSpec(memory_space=pltpu.SMEM),
                  _row_spec(tq, QKV_W),
                  pl.BlockSpec((BLOCK, QKV_W), lambda i: (jnp.maximum(i * nblk - 1, 0), 0))],
        out_specs=_row_spec(tq, ATTN_W),
        out_shape=jax.ShapeDtypeStruct((T, ATTN_W), BF16),
        scratch_shapes=[pltpu.VMEM((tq + BLOCK, 2 * KV_W), BF16)],
        compiler_params=_params("parallel"),
    )(sinks, qkv, qkv)


def _shifted_copies(buf):
    n = buf.shape[1] - SUBLANES
    for s in range(1, SUBLANES):
        buf[s, 0:n, :] = buf[0, s:s + n, :]


def _shifted_rows(buf, start, rows):
    s = start % SUBLANES
    return buf[s, start - s:start - s + rows, :]


def _conv_rows_fwd(u0, carry, ubuf, w_ref, cb_ref, g_ref, b_ref, u0_ref, yc_ref, u_ref):
    n, C = u0.shape
    R = _tile(n, CONV_ROWS)
    ubuf[0, 0:HALO, :] = carry[...]
    ubuf[0, HALO:, :] = u0
    carry[...] = u0[n - HALO:, :]
    u0_ref[...] = u0
    _shifted_copies(ubuf)
    off = HALO - (CONV_K - 1)
    for c in range(n // R):
        acc = jnp.broadcast_to(cb_ref[...], (R, C))
        for j in range(CONV_K):
            acc = acc + w_ref[j:j + 1, :] * _shifted_rows(ubuf, c * R + off + j, R)
        yc_ref[c * R:(c + 1) * R, :] = acc
        xc = acc - jnp.mean(acc, axis=-1, keepdims=True)
        ln = xc * lax.rsqrt(jnp.mean(xc * xc, axis=-1, keepdims=True) + EPS) * g_ref[...] + b_ref[...]
        u_ref[c * R:(c + 1) * R, :] = (ln * _sigmoid(ln)).astype(BF16)


def _merge_fwd(attn, u, rest, x, wa, wc, bc, wo):
    T, D = x.shape
    tm = _tile(T, 512)
    gcol = 2 * CONV_C // D

    def body(attn_ref, u_ref, ga_ref, gc_ref, x_ref, wa_ref, wc_ref, bc_ref, wo_ref, m_ref, x1_ref):
        bra = _dot(attn_ref[...], wa_ref[...])
        brc = _dot(u_ref[...], wc_ref[...]) + bc_ref[...]
        mb = (_sigmoid(ga_ref[...].astype(F32)) * bra + _sigmoid(gc_ref[...].astype(F32)) * brc).astype(BF16)
        m_ref[...] = mb
        x1_ref[...] = x_ref[...] + _dot(mb, wo_ref[...])

    return pl.pallas_call(
        body, name="merge_fwd", grid=(T // tm,),
        in_specs=[_row_spec(tm, ATTN_W), _row_spec(tm, CONV_C), _row_spec(tm, D, gcol), _row_spec(tm, D, gcol + 1),
                  _row_spec(tm, D), _weight_spec((ATTN_W, D)), _weight_spec((CONV_C, D)), _full_spec((1, D)),
                  _weight_spec((D, D))],
        out_specs=[_row_spec(tm, D), _row_spec(tm, D)],
        out_shape=[jax.ShapeDtypeStruct((T, D), BF16), jax.ShapeDtypeStruct((T, D), F32)],
        compiler_params=_params("parallel"),
    )(attn, u, rest, rest, x, wa, wc, bc, wo)


def _mlp_fwd(x1, g, w1, w2, comm=None):
    T, D = x1.shape
    tm = _tile(T, 512)
    fc = _tile(D_FF, FF_CHUNK)

    def body(x_ref, g_ref, w1_ref, w2_ref, h_ref, z_ref, o_ref):
        xv = x_ref[...]
        r = lax.rsqrt(jnp.mean(xv * xv, axis=-1, keepdims=True) + EPS)
        h = (xv * r * g_ref[...]).astype(BF16)
        h_ref[...] = h
        acc = xv
        for c in range(D_FF // fc):
            cols = slice(c * fc, (c + 1) * fc)
            z = _dot(h, w1_ref[:, cols])
            z_ref[:, cols] = z.astype(BF16)
            acc = acc + _dot(jnp.square(jnp.maximum(z, 0.0)).astype(BF16), w2_ref[cols, :])
        o_ref[...] = acc

    return _pallas(
        body, name="mlp_fwd", grid=(T // tm,),
        in_specs=[_row_spec(tm, D), _full_spec((1, D)), _weight_spec((D, D_FF)), _weight_spec((D_FF, D))],
        out_specs=[_row_spec(tm, D), _row_spec(tm, D_FF), _row_spec(tm, D)],
        out_shape=[jax.ShapeDtypeStruct((T, D), BF16), jax.ShapeDtypeStruct((T, D_FF), BF16),
                   jax.ShapeDtypeStruct((T, D), F32)],
        args=(x1, g, w1, w2), sem=("parallel",), comm=comm)


def _final_loss(x, g, target):
    T, D = x.shape
    tm = _tile(T, 512)

    def body(x_ref, g_ref, t_ref, l_ref, dx_ref, dg_ref):
        @pl.when(pl.program_id(0) == 0)
        def _():
            l_ref[...] = jnp.zeros_like(l_ref)
            dg_ref[...] = jnp.zeros_like(dg_ref)

        xv = x_ref[...]
        r = lax.rsqrt(jnp.mean(xv * xv, axis=-1, keepdims=True) + EPS)
        e = xv * r * g_ref[...] - t_ref[...]
        l_ref[...] += _rows8(e * e) * (0.5 / D)
        dx, dg = _rms_bwd(xv, g_ref[...], e * (1.0 / D))
        dx_ref[...] = dx
        dg_ref[...] += _rows8(dg)

    return pl.pallas_call(
        body, name="final_loss", grid=(T // tm,),
        in_specs=[_row_spec(tm, D), _full_spec((1, D)), _row_spec(tm, D)],
        out_specs=[_full_spec((SUBLANES, D)), _row_spec(tm, D), _full_spec((SUBLANES, D))],
        out_shape=[jax.ShapeDtypeStruct((SUBLANES, D), F32), jax.ShapeDtypeStruct((T, D), F32),
                   jax.ShapeDtypeStruct((SUBLANES, D), F32)],
        compiler_params=_params("arbitrary"),
    )(x, g, target)


def _mlp_bwd(dx2, x1, z, g, w1, w2, comm=None):
    T, D = x1.shape
    tm = _tile(T, 512)
    fc = _tile(D_FF, FF_CHUNK)

    def body(dx2_ref, x_ref, z_ref, g_ref, w1_ref, w2_ref, dx1_ref, dz_ref, dg_ref):
        @pl.when(pl.program_id(0) == 0)
        def _():
            dg_ref[...] = jnp.zeros_like(dg_ref)

        dxo = dx2_ref[...]
        dxb = dxo.astype(BF16)
        dh = jnp.zeros((tm, D), F32)
        for c in range(D_FF // fc):
            cols = slice(c * fc, (c + 1) * fc)
            da = _dot_nt(dxb, w2_ref[cols, :])
            dz = (da * (2.0 * jnp.maximum(z_ref[:, cols].astype(F32), 0.0))).astype(BF16)
            dz_ref[:, cols] = dz
            dh = dh + _dot_nt(dz, w1_ref[:, cols])
        dx, dg = _rms_bwd(x_ref[...], g_ref[...], dh)
        dx1_ref[...] = dxo + dx
        dg_ref[...] += _rows8(dg)

    return _pallas(
        body, name="mlp_bwd", grid=(T // tm,),
        in_specs=[_row_spec(tm, D), _row_spec(tm, D), _row_spec(tm, D_FF), _full_spec((1, D)),
                  _weight_spec((D, D_FF)), _weight_spec((D_FF, D))],
        out_specs=[_row_spec(tm, D), _row_spec(tm, D_FF), _full_spec((SUBLANES, D))],
        out_shape=[jax.ShapeDtypeStruct((T, D), F32), jax.ShapeDtypeStruct((T, D_FF), BF16),
                   jax.ShapeDtypeStruct((SUBLANES, D), F32)],
        args=(dx2, x1, z, g, w1, w2), sem=("arbitrary",), comm=comm)


def _tn_blocks(a, b, name, col_sharded, relu_sq=False):
    T, M = a.shape
    N = b.shape[1]
    tk = _tile(T, 1024)
    tm = _tile(M, 512 if col_sharded else 1024)
    nb = N // N_DEV
    last = T // tk - 1

    def body(a_ref, b_ref, o_ref, acc_ref):
        k = pl.program_id(1)

        @pl.when(k == 0)
        def _():
            acc_ref[...] = jnp.zeros_like(acc_ref)

        av = a_ref[...]
        if relu_sq:
            av = jnp.square(jnp.maximum(av, 0.0))
        acc_ref[...] += _dot_tn(av.astype(BF16), b_ref[...].astype(BF16))

        @pl.when(k == last)
        def _():
            if col_sharded:
                for d in range(N_DEV):
                    o_ref[d] = acc_ref[:, d * nb:(d + 1) * nb].astype(BF16)
            else:
                o_ref[...] = acc_ref[...].astype(BF16)

    if col_sharded:
        out_spec = pl.BlockSpec((N_DEV, tm, nb), lambda i, k: (0, i, 0))
        out_shape = jax.ShapeDtypeStruct((N_DEV, M, nb), BF16)
    else:
        out_spec = pl.BlockSpec((tm, N), lambda i, k: (i, 0))
        out_shape = jax.ShapeDtypeStruct((M, N), BF16)
    out = pl.pallas_call(
        body, name=name, grid=(M // tm, T // tk),
        in_specs=[pl.BlockSpec((tk, tm), lambda i, k: (k, i)), pl.BlockSpec((tk, N), lambda i, k: (k, 0))],
        out_specs=out_spec, out_shape=out_shape,
        scratch_shapes=[pltpu.VMEM((tm, N), F32)],
        compiler_params=_params("parallel", "arbitrary"),
    )(a, b)
    return out if col_sharded else out.reshape(N_DEV, M // N_DEV, N)


def _merge_bwd(dx1, attn, u, rest, yc, wa, wc, bc, wo, lg, lb, comm=None):
    T, D = dx1.shape
    C = CONV_C
    tm = _tile(T, 512)
    gcol = 2 * CONV_C // D

    def body(dx_ref, attn_ref, u_ref, ga_ref, gc_ref, yc_ref, wa_ref, wc_ref, bc_ref, wo_ref, lg_ref, lb_ref,
             dattn_ref, dyc_ref, dga_ref, dgc_ref, dbra_ref, dbrc_ref, dbc_ref, dlg_ref, dlb_ref):
        @pl.when(pl.program_id(0) == 0)
        def _():
            dbc_ref[...] = jnp.zeros_like(dbc_ref)
            dlg_ref[...] = jnp.zeros_like(dlg_ref)
            dlb_ref[...] = jnp.zeros_like(dlb_ref)

        dm = _dot_nt(dx_ref[...].astype(BF16), wo_ref[...])
        bra = _dot(attn_ref[...], wa_ref[...])
        brc = _dot(u_ref[...], wc_ref[...]) + bc_ref[...]
        sa = _sigmoid(ga_ref[...].astype(F32))
        sc = _sigmoid(gc_ref[...].astype(F32))
        dbra = dm * sa
        dbrc = dm * sc
        dga_ref[...] = (dm * bra * sa * (1.0 - sa)).astype(BF16)
        dgc_ref[...] = (dm * brc * sc * (1.0 - sc)).astype(BF16)
        dbra_b = dbra.astype(BF16)
        dbrc_b = dbrc.astype(BF16)
        dbra_ref[...] = dbra_b
        dbrc_ref[...] = dbrc_b
        dbc_ref[...] += _rows8(dbrc)
        dattn_ref[...] = _dot_nt(dbra_b, wa_ref[...]).astype(BF16)
        dyc, dlg, dlb = _swish_norm_bwd(_dot_nt(dbrc_b, wc_ref[...]), yc_ref[...], lg_ref[...], lb_ref[...])
        dyc_ref[...] = dyc
        dlg_ref[...] += _rows8(dlg)
        dlb_ref[...] += _rows8(dlb)

    return _pallas(
        body, name="merge_bwd", grid=(T // tm,),
        in_specs=[_row_spec(tm, D), _row_spec(tm, ATTN_W), _row_spec(tm, C), _row_spec(tm, D, gcol),
                  _row_spec(tm, D, gcol + 1), _row_spec(tm, C), _weight_spec((ATTN_W, D)), _weight_spec((C, D)),
                  _full_spec((1, D)), _weight_spec((D, D)), _full_spec((1, C)), _full_spec((1, C))],
        out_specs=[_row_spec(tm, ATTN_W), _row_spec(tm, C), _row_spec(tm, D), _row_spec(tm, D),
                   _row_spec(tm, D), _row_spec(tm, D), _full_spec((SUBLANES, D)), _full_spec((SUBLANES, C)),
                   _full_spec((SUBLANES, C))],
        out_shape=[jax.ShapeDtypeStruct((T, ATTN_W), BF16), jax.ShapeDtypeStruct((T, C), F32),
                   jax.ShapeDtypeStruct((T, D), BF16), jax.ShapeDtypeStruct((T, D), BF16),
                   jax.ShapeDtypeStruct((T, D), BF16), jax.ShapeDtypeStruct((T, D), BF16),
                   jax.ShapeDtypeStruct((SUBLANES, D), F32), jax.ShapeDtypeStruct((SUBLANES, C), F32),
                   jax.ShapeDtypeStruct((SUBLANES, C), F32)],
        args=(dx1, attn, u, rest, rest, yc, wa, wc, bc, wo, lg, lb), sem=("arbitrary",), comm=comm)


def _swish_norm_bwd(du, yv, g, b):
    xc = yv - jnp.mean(yv, axis=-1, keepdims=True)
    rstd = lax.rsqrt(jnp.mean(xc * xc, axis=-1, keepdims=True) + EPS)
    xn = xc * rstd
    ln = xn * g + b
    sg = _sigmoid(ln)
    dln = du * sg * (1.0 + ln * (1.0 - sg))
    dxn = dln * g
    dyc = rstd * (dxn - jnp.mean(dxn, axis=-1, keepdims=True) - xn * jnp.mean(dxn * xn, axis=-1, keepdims=True))
    return dyc, dln * xn, dln


def _conv_taps_bwd(first, last, dy_ref, dyn_ref, u0_ref, u0p_ref, glu_ref, w_ref, dglu_ref, dw_ref, db_ref,
                   dbuf, ubuf):
    tm, C = dy_ref.shape
    R = _tile(tm, CONV_ROWS)
    dbuf[0, 0:tm, :] = dy_ref[...]
    dbuf[0, tm:, :] = jnp.where(last, 0.0, dyn_ref[...])
    ubuf[0, 0:HALO, :] = jnp.where(first, 0.0, u0p_ref[...])
    ubuf[0, HALO:, :] = u0_ref[...]
    _shifted_copies(dbuf)
    _shifted_copies(ubuf)
    off = HALO - (CONV_K - 1)
    for c in range(tm // R):
        rows = slice(c * R, (c + 1) * R)
        dy = dbuf[0, rows, :]
        acc = jnp.zeros((R, C), F32)
        for j in range(CONV_K):
            acc = acc + w_ref[j:j + 1, :] * _shifted_rows(dbuf, c * R + CONV_K - 1 - j, R)
            dw_ref[j * SUBLANES:(j + 1) * SUBLANES, :] += _rows8(dy * _shifted_rows(ubuf, c * R + off + j, R))
        db_ref[...] += _rows8(dy)
        a = glu_ref[rows, :C].astype(F32)
        sb = _sigmoid(glu_ref[rows, C:].astype(F32))
        dglu_ref[rows, :C] = (acc * sb).astype(BF16)
        dglu_ref[rows, C:] = (acc * a * sb * (1.0 - sb)).astype(BF16)


def _attn_bwd(qkv, dattn, sinks, comm=None):
    T = qkv.shape[0]
    tq = _tile(T, 512)
    nblk = tq // BLOCK
    scale = 1.0 / math.sqrt(HEAD_DIM)

    def body(sink_ref, cur_ref, prev_ref, do_ref, dq_ref, hi_ref, lo_ref, ds_ref, kv_buf):
        i = pl.program_id(0)

        @pl.when(i == 0)
        def _():
            ds_ref[...] = jnp.zeros_like(ds_ref)

        _fill_kv(kv_buf, cur_ref, prev_ref)
        upper, distf, keep = _fold_masks(i == 0)
        bias, sink = _head_consts(sink_ref, distf)
        for j in range(nblk):
            rows = slice(j * BLOCK, (j + 1) * BLOCK)
            band = kv_buf[j * BLOCK:(j + 2) * BLOCK, :]
            q = cur_ref[rows, :ATTN_W]
            do = do_ref[rows, :]
            dqs, dks, dvs = [], [], []
            for kh in range(N_KV):
                k = band[:, kh * HEAD_DIM:(kh + 1) * HEAD_DIM]
                v = band[:, KV_W + kh * HEAD_DIM:KV_W + (kh + 1) * HEAD_DIM]
                qg = _group_rows(q, kh)
                dog = _group_rows(do, kh)
                p, psink = _attn_probs(_dot_nt(k, qg), bias[kh], sink[kh], upper, keep if j == 0 else None)
                pdp = p * _fold(_dot_nt(v, dog), upper)
                delta = jnp.sum(pdp, axis=0, keepdims=True)
                lanes = slice(kh * GROUP * BLOCK, (kh + 1) * GROUP * BLOCK)
                ds_ref[0:1, lanes] += psink * delta
                dsb = _unfold(pdp - p * delta, upper).astype(BF16)
                dqs.append(_heads_out(_dot_tn(k, dsb)))
                dks.append(_dot(dsb, qg) * scale)
                dvs.append(_dot(_unfold(p, upper).astype(BF16), dog))
            dq_ref[rows, :] = jnp.concatenate(dqs, axis=1).astype(BF16)
            dkv = jnp.concatenate(dks + dvs, axis=1)
            lo_ref[rows, :] = dkv[:BLOCK, :]
            hi_ref[rows, :] = dkv[BLOCK:, :]

    return _pallas(
        body, name="attn_bwd", grid=(T // tq,),
        in_specs=[pl.BlockSpec(memory_space=pltpu.SMEM),
                  _row_spec(tq, QKV_W),
                  pl.BlockSpec((BLOCK, QKV_W), lambda i: (jnp.maximum(i * nblk - 1, 0), 0)),
                  _row_spec(tq, ATTN_W)],
        out_specs=[_row_spec(tq, ATTN_W), _row_spec(tq, 2 * KV_W), _row_spec(tq, 2 * KV_W),
                   _full_spec((SUBLANES, N_Q * BLOCK))],
        out_shape=[jax.ShapeDtypeStruct((T, ATTN_W), BF16), jax.ShapeDtypeStruct((T, 2 * KV_W), F32),
                   jax.ShapeDtypeStruct((T, 2 * KV_W), F32), jax.ShapeDtypeStruct((SUBLANES, N_Q * BLOCK), F32)],
        scratch_shapes=[pltpu.VMEM((tq + BLOCK, 2 * KV_W), BF16)],
        args=(sinks, qkv, qkv, dattn), sem=("arbitrary",), comm=comm)


def _inproj_bwd(dq, hi, lo, dyc, u0, rest, cw, dga, dgc, x, g, w, dx1, comm=None):
    T, D = x.shape
    C = CONV_C
    tm = _tile(T, 256)
    per = tm // BLOCK
    per_halo = tm // HALO
    nt = T // tm
    kv2 = 2 * KV_W
    glu0, gate0 = QKV_W, QKV_W + 2 * C

    def body(dq_ref, hi_ref, lo_ref, lon_ref, dy_ref, dyn_ref, u0_ref, u0p_ref, glu_ref, cw_ref, dga_ref, dgc_ref,
             x_ref, g_ref, w_ref, dx1_ref, dp_ref, dx_ref, dg_ref, dbias_ref, dcw_ref, dcb_ref, dbuf, ubuf):
        i = pl.program_id(0)

        @pl.when(i == 0)
        def _():
            for ref in (dg_ref, dbias_ref, dcw_ref, dcb_ref):
                ref[...] = jnp.zeros_like(ref)

        def part(cols):
            dp = dp_ref[:, cols]
            dbias_ref[:, cols] += _rows8(dp.astype(F32))
            return _dot_nt(dp, w_ref[:, cols])

        dp_ref[:, :ATTN_W] = dq_ref[...]
        lo_next = jnp.where(i < nt - 1, lon_ref[...], 0.0)
        lo_shift = jnp.concatenate([lo_ref[BLOCK:, :], lo_next], axis=0) if tm > BLOCK else lo_next
        dp_ref[:, ATTN_W:QKV_W] = (hi_ref[...] + lo_shift).astype(BF16)
        dp_ref[:, gate0:gate0 + D] = dga_ref[...]
        dp_ref[:, gate0 + D:] = dgc_ref[...]
        _conv_taps_bwd(i == 0, i == nt - 1, dy_ref, dyn_ref, u0_ref, u0p_ref, glu_ref, cw_ref,
                       dp_ref.at[:, glu0:gate0], dcw_ref, dcb_ref, dbuf, ubuf)
        dh = part(slice(0, QKV_W)) + part(slice(gate0, IN_W)) + part(slice(glu0, gate0))
        dx, dg = _rms_bwd(x_ref[...], g_ref[...], dh)
        dx_ref[...] = dx1_ref[...] + dx
        dg_ref[...] += _rows8(dg)

    return _pallas(
        body, name="inproj_bwd", grid=(nt,),
        in_specs=[_row_spec(tm, ATTN_W), _row_spec(tm, kv2), _row_spec(tm, kv2),
                  pl.BlockSpec((BLOCK, kv2), lambda i: (jnp.minimum((i + 1) * per, T // BLOCK - 1), 0)),
                  _row_spec(tm, C),
                  pl.BlockSpec((HALO, C), lambda i: (jnp.minimum((i + 1) * per_halo, T // HALO - 1), 0)),
                  _row_spec(tm, C),
                  pl.BlockSpec((HALO, C), lambda i: (jnp.maximum(i * per_halo - 1, 0), 0)),
                  _row_spec(tm, 2 * C), _full_spec((CONV_K, C)),
                  _row_spec(tm, D), _row_spec(tm, D), _row_spec(tm, D), _full_spec((1, D)),
                  _weight_spec((D, IN_W)), _row_spec(tm, D)],
        out_specs=[_row_spec(tm, IN_W), _row_spec(tm, D), _full_spec((SUBLANES, D)), _full_spec((SUBLANES, IN_W)),
                   _full_spec((CONV_K * SUBLANES, C)), _full_spec((SUBLANES, C))],
        out_shape=[jax.ShapeDtypeStruct((T, IN_W), BF16), jax.ShapeDtypeStruct((T, D), F32),
                   jax.ShapeDtypeStruct((SUBLANES, D), F32), jax.ShapeDtypeStruct((SUBLANES, IN_W), F32),
                   jax.ShapeDtypeStruct((CONV_K * SUBLANES, C), F32), jax.ShapeDtypeStruct((SUBLANES, C), F32)],
        scratch_shapes=[pltpu.VMEM((SUBLANES, tm + HALO, C), F32), pltpu.VMEM((SUBLANES, tm + HALO, C), F32)],
        args=(dq, hi, lo, lo, dyc, dyc, u0, u0, rest, cw, dga, dgc, x, g, w, dx1), sem=("arbitrary",), comm=comm)


def _adamw_math(g, w, m, v):
    c1 = 1.0 / (1.0 - ADAM_B1 ** ADAM_STEP)
    c2 = 1.0 / (1.0 - ADAM_B2 ** ADAM_STEP)
    mn = ADAM_B1 * m + (1.0 - ADAM_B1) * g
    vn = ADAM_B2 * v + (1.0 - ADAM_B2) * (g * g)
    return -ADAM_LR * ((mn * c1) / (jnp.sqrt(vn * c2) + ADAM_EPS) + ADAM_WD * w), mn, vn


def _adamw_sharded(parts, w, m, v, name):
    depth, a, b = w.shape
    tr = _tile(a, 256) if a % SUBLANES == 0 else a
    nr = a // tr

    def body(*refs):
        p_refs, (w_ref, m_ref, v_ref, g_ref, d_ref, mo_ref, vo_ref) = refs[:depth], refs[depth:]
        layer = pl.program_id(0)
        for l in range(depth):
            @pl.when(layer == l)
            def _(l=l):
                g = p_refs[l][0].astype(F32)
                for s in range(1, N_DEV):
                    g = g + p_refs[l][s].astype(F32)
                g_ref[...] = g
                d_ref[...], mo_ref[...], vo_ref[...] = _adamw_math(g, w_ref[...], m_ref[...], v_ref[...])

    def part_spec(l):
        return pl.BlockSpec((N_DEV, tr, b),
                            lambda k, i: (0, jnp.where(k == l, i, jnp.where(k < l, 0, nr - 1)), 0))

    spec = pl.BlockSpec((None, tr, b), lambda k, i: (k, i, 0))
    out = jax.ShapeDtypeStruct((depth, a, b), F32)
    return pl.pallas_call(
        body, name=name, grid=(depth, nr),
        in_specs=[part_spec(l) for l in range(depth)] + [spec] * 3,
        out_specs=[spec] * 4, out_shape=[out] * 4,
        compiler_params=_params("arbitrary", "arbitrary"),
    )(*parts, w, m, v)


def _adamw_small(parts, w, m, v):
    R, N = w.shape

    def body(p_ref, w_ref, m_ref, v_ref, g_ref, d_ref, mo_ref, vo_ref):
        g = p_ref[0]
        for s in range(1, N_DEV):
            g = g + p_ref[s]
        g_ref[...] = g
        d_ref[...], mo_ref[...], vo_ref[...] = _adamw_math(g, w_ref[...], m_ref[...], v_ref[...])

    out = jax.ShapeDtypeStruct((R, N), F32)
    return pl.pallas_call(
        body, name="adamw_small", grid=(1,),
        in_specs=[_full_spec((N_DEV, R, N))] + [_full_spec((R, N))] * 3,
        out_specs=[_full_spec((R, N))] * 4, out_shape=[out] * 4,
        compiler_params=_params("arbitrary"),
    )(parts, w, m, v)


_SHARDED = ("w_in", "conv_w", "w_attn_proj", "w_conv_proj", "w_out", "w_mlp1", "w_mlp2")
_ROW_SHARDED = ("w_out", "w_mlp2")
_FIRST = ("w_in", "conv_w")
_REST = tuple(n for n in _SHARDED if n not in _FIRST)
_SMALL = ("mix_norm_g", "b_in", "sinks", "conv_b", "conv_ln_g", "conv_ln_b", "b_conv_proj", "mlp_norm_g",
          "final_norm_g")
_ORDER = ("mix_norm_g", "w_in", "b_in", "sinks", "conv_w", "conv_b", "conv_ln_g", "conv_ln_b", "w_attn_proj",
          "w_conv_proj", "b_conv_proj", "w_out", "mlp_norm_g", "w_mlp1", "w_mlp2", "final_norm_g")
_PACK = 1024


def _full_weights(names, gathered):
    cols = [i for i, n in enumerate(names) if n not in _ROW_SHARDED]

    def body(*refs):
        for src, dst in zip(refs[:len(cols)], refs[len(cols):]):
            b = src.shape[2]
            for d in range(N_DEV):
                dst[:, d * b:(d + 1) * b] = src[d]

    vmem = pl.BlockSpec(memory_space=pltpu.VMEM)
    placed = pl.pallas_call(
        body, name="place_" + names[cols[0]], in_specs=[vmem] * len(cols), out_specs=[vmem] * len(cols),
        out_shape=[jax.ShapeDtypeStruct((gathered[i].shape[1], N_DEV * gathered[i].shape[2]), gathered[i].dtype)
                   for i in cols],
        compiler_params=pltpu.CompilerParams(vmem_limit_bytes=VMEM_LIMIT),
    )(*[gathered[i] for i in cols])
    full = {names[i]: a for i, a in zip(cols, placed)}
    for n, a in zip(names, gathered):
        if n in _ROW_SHARDED:
            full[n] = a.reshape(N_DEV * a.shape[1], a.shape[2])
    return full


def _pack(arrs):
    flat = []
    for a in arrs:
        a = a.reshape(-1)
        flat.append(jnp.pad(a, (0, -a.size % _PACK)))
    return jnp.concatenate(flat).reshape(-1, BLOCK)


def _unpack(packed, shapes):
    flat = packed.reshape(-1)
    out, off = [], 0
    for s in shapes:
        n = math.prod(s)
        out.append(flat[off:off + n].reshape(s))
        off += n + (-n % _PACK)
    return out


def _layer_fwd(x, lw, own_rest=None, comm=None):
    (h, qkv, rest, u0, yc, u), got = _inproj_fwd(x, lw["mix_norm_g"], lw["w_in"], lw["b_in"], lw["conv_w"],
                                                 lw["conv_b"], lw["conv_ln_g"], lw["conv_ln_b"], own_rest)
    if got is not None:
        lw.update(_full_weights(_REST, got))
    attn = _attn_fwd(qkv, lw["sinks"])
    merged, x1 = _merge_fwd(attn, u, rest, x, lw["w_attn_proj"], lw["w_conv_proj"], lw["b_conv_proj"], lw["w_out"])
    (h2, z, x2), gathered = _mlp_fwd(x1, lw["mlp_norm_g"], lw["w_mlp1"], lw["w_mlp2"], comm)
    saved = dict(x=x, h=h, qkv=qkv, rest=rest, attn=attn, u0=u0, yc=yc, u=u, merged=merged, x1=x1, h2=h2, z=z)
    return x2, saved, gathered


_EARLY = ("w_mlp1", "w_mlp2")
_MIDDLE = ("w_out", "w_attn_proj", "w_conv_proj")
_LATE = ("w_in", "conv_w")


def _layer_bwd(dx2, lw, s, late_blocks, dx_is_result):
    g, recv = {}, {}
    late = None if late_blocks is None else _Exchange(late_blocks)
    (dx1, dz, dg2), late_recv = _mlp_bwd(dx2, s["x1"], s["z"], lw["mlp_norm_g"], lw["w_mlp1"], lw["w_mlp2"], late)
    g["mlp_norm_g"] = jnp.sum(dg2, axis=0)
    early = _Exchange([_tn_blocks(s["h2"], dz, "dw_mlp1", True),
                       _tn_blocks(s["z"], dx2, "dw_mlp2", False, relu_sq=True)])
    (dattn, dyc, dga, dgc, dbra, dbrc, dbc, dlg, dlb), early_recv = _merge_bwd(
        dx1, s["attn"], s["u"], s["rest"], s["yc"], lw["w_attn_proj"], lw["w_conv_proj"], lw["b_conv_proj"],
        lw["w_out"], lw["conv_ln_g"], lw["conv_ln_b"], early if dx_is_result else None)
    g["b_conv_proj"] = jnp.sum(dbc, axis=0)
    g["conv_ln_g"] = jnp.sum(dlg, axis=0)
    g["conv_ln_b"] = jnp.sum(dlb, axis=0)
    middle = [_tn_blocks(s["merged"], dx1, "dw_out", False), _tn_blocks(s["attn"], dbra, "dw_attn_proj", True),
              _tn_blocks(s["u"], dbrc, "dw_conv_proj", True)]
    (dq, hi, lo, dsk), middle_recv = _attn_bwd(s["qkv"], dattn, lw["sinks"], _Exchange(middle))
    recv.update(zip(_MIDDLE, middle_recv))
    g["sinks"] = -jnp.sum(dsk[0].reshape(N_Q, BLOCK), axis=1)
    (dproj, dx, dg1, dbin, dcw, dcb), behind_inproj = _inproj_bwd(
        dq, hi, lo, dyc, s["u0"], s["rest"], lw["conv_w"], dga, dgc, s["x"], lw["mix_norm_g"], lw["w_in"], dx1,
        None if dx_is_result else early)
    recv.update(zip(_EARLY, early_recv if dx_is_result else behind_inproj))
    dconv_w = jnp.sum(dcw.reshape(CONV_K, SUBLANES, CONV_C), axis=1)
    g["conv_b"] = jnp.sum(dcb, axis=0)
    g["mix_norm_g"] = jnp.sum(dg1, axis=0)
    g["b_in"] = jnp.sum(dbin, axis=0)
    own_late = [_tn_blocks(s["h"], dproj, "dw_in", True),
                dconv_w.reshape(CONV_K, N_DEV, CONV_C // N_DEV).transpose(1, 0, 2)]
    return dx, g, recv, late_recv, own_late


def kernel(x, mix_norm_g, w_in, b_in, sinks, conv_w, conv_b, conv_ln_g, conv_ln_b, w_attn_proj, w_conv_proj, b_conv_proj, w_out, mlp_norm_g, w_mlp1, w_mlp2, final_norm_g, loss_target, m_mix_norm_g, m_w_in, m_b_in, m_sinks, m_conv_w, m_conv_b, m_conv_ln_g, m_conv_ln_b, m_w_attn_proj, m_w_conv_proj, m_b_conv_proj, m_w_out, m_mlp_norm_g, m_w_mlp1, m_w_mlp2, m_final_norm_g, v_mix_norm_g, v_w_in, v_b_in, v_sinks, v_conv_w, v_conv_b, v_conv_ln_g, v_conv_ln_b, v_w_attn_proj, v_w_conv_proj, v_b_conv_proj, v_w_out, v_mlp_norm_g, v_w_mlp1, v_w_mlp2, v_final_norm_g):
    w = dict(mix_norm_g=mix_norm_g, w_in=w_in, b_in=b_in, sinks=sinks, conv_w=conv_w, conv_b=conv_b,
             conv_ln_g=conv_ln_g, conv_ln_b=conv_ln_b, w_attn_proj=w_attn_proj, w_conv_proj=w_conv_proj,
             b_conv_proj=b_conv_proj, w_out=w_out, mlp_norm_g=mlp_norm_g, w_mlp1=w_mlp1, w_mlp2=w_mlp2,
             final_norm_g=final_norm_g)
    m = dict(mix_norm_g=m_mix_norm_g, w_in=m_w_in, b_in=m_b_in, sinks=m_sinks, conv_w=m_conv_w, conv_b=m_conv_b,
             conv_ln_g=m_conv_ln_g, conv_ln_b=m_conv_ln_b, w_attn_proj=m_w_attn_proj, w_conv_proj=m_w_conv_proj,
             b_conv_proj=m_b_conv_proj, w_out=m_w_out, mlp_norm_g=m_mlp_norm_g, w_mlp1=m_w_mlp1, w_mlp2=m_w_mlp2,
             final_norm_g=m_final_norm_g)
    v = dict(mix_norm_g=v_mix_norm_g, w_in=v_w_in, b_in=v_b_in, sinks=v_sinks, conv_w=v_conv_w, conv_b=v_conv_b,
             conv_ln_g=v_conv_ln_g, conv_ln_b=v_conv_ln_b, w_attn_proj=v_w_attn_proj, w_conv_proj=v_w_conv_proj,
             b_conv_proj=v_b_conv_proj, w_out=v_w_out, mlp_norm_g=v_mlp_norm_g, w_mlp1=v_w_mlp1, w_mlp2=v_w_mlp2,
             final_norm_g=v_final_norm_g)
    T = x.shape[1]
    xs = x.reshape(T, D_MODEL)
    target = loss_target.reshape(T, D_MODEL)

    def gather_of(l, names):
        return _Gather([w[n][l] if n == "conv_w" else w[n][l].astype(BF16) for n in names])

    def layer_weights(l, names, gathered):
        lw = _full_weights(names, gathered)
        for n in _SMALL:
            if n != "final_norm_g":
                lw[n] = w[n][l] if n == "sinks" else w[n][l].reshape(1, -1)
        return lw

    acts = xs
    saved, weights = [], []
    for l in range(DEPTH):
        following = gather_of(l + 1, _SHARDED) if l + 1 < DEPTH else None
        if l == 0:
            lw = layer_weights(0, _FIRST, _run_comm(gather_of(0, _FIRST), "gather_first"))
            acts, s, gathered = _layer_fwd(acts, lw, gather_of(0, _REST), following)
        else:
            lw = layer_weights(l, _SHARDED, gathered)
            acts, s, gathered = _layer_fwd(acts, lw, None, following)
        weights.append(lw)
        saved.append(s)
    lterms, dx, dgf = _final_loss(acts, final_norm_g.reshape(1, -1), target)
    grads, received = [None] * DEPTH, [None] * DEPTH
    late = None
    for l in reversed(range(DEPTH)):
        dx, grads[l], received[l], late_recv, late = _layer_bwd(dx, weights[l], saved[l], late, l == 0)
        if late_recv is not None:
            received[l + 1].update(zip(_LATE, late_recv))
    received[0].update(zip(_LATE, _run_comm(_Exchange(late), "scatter_late")))
    grad = {n: jnp.stack([grads[l][n] for l in range(DEPTH)]) for n in _SMALL if n != "final_norm_g"}
    grad["final_norm_g"] = jnp.sum(dgf, axis=0)

    small_shapes = [w[n].shape for n in _SMALL] + [(1,)]
    small = _pack([grad[n] for n in _SMALL] + [jnp.sum(lterms).reshape(1)])
    small_parts = _run_comm(_Gather([small]), "gather_small")[0]

    out_g, out_d, out_m, out_v = {}, {}, {}, {}
    for n in _SHARDED:
        out_g[n], out_d[n], out_m[n], out_v[n] = _adamw_sharded(
            [received[l][n] for l in range(DEPTH)], w[n], m[n], v[n], "adamw_" + n)
    zero = jnp.zeros((1,), F32)
    res = _adamw_small(small_parts, _pack([w[n] for n in _SMALL] + [zero]), _pack([m[n] for n in _SMALL] + [zero]),
                       _pack([v[n] for n in _SMALL] + [zero]))
    unpacked = [_unpack(r, small_shapes) for r in res]
    for i, n in enumerate(_SMALL):
        out_g[n], out_d[n], out_m[n], out_v[n] = (u[i] for u in unpacked)
    loss = unpacked[0][-1].reshape(())
    return (loss, dx.reshape(x.shape), *[out_g[n] for n in _ORDER], *[out_d[n] for n in _ORDER],
            *[out_m[n] for n in _ORDER], *[out_v[n] for n in _ORDER])
```

```python
import functools
import math

import jax
import jax.numpy as jnp
from jax import lax
from jax.experimental import pallas as pl
from jax.experimental.pallas import tpu as pltpu

D_MODEL = 1024
SEQ = 16384
DEPTH = 2
N_Q = 8
N_KV = 2
GROUP = N_Q // N_KV
HEAD_DIM = 64
ATTN_W = N_Q * HEAD_DIM
KV_W = N_KV * HEAD_DIM
BLOCK = 128
CONV_C = D_MODEL // 2
CONV_K = 31
D_FF = 4 * D_MODEL
QKV_W = ATTN_W + 2 * KV_W
IN_W = QKV_W + 2 * CONV_C + 2 * D_MODEL
EPS = 1e-6
NEG = -1e30
N_DEV = 8

ADAM_LR = 0.001
ADAM_B1 = 0.9
ADAM_B2 = 0.999
ADAM_EPS = 1e-08
ADAM_WD = 0.01
ADAM_STEP = 10

F32 = jnp.float32
BF16 = jnp.bfloat16
MESH = pl.DeviceIdType.MESH

SUBLANES = 8
HALO = 32
FF_CHUNK = 1024
CONV_ROWS = 16
VMEM_LIMIT = 52 * 1024 * 1024

_NT = (((1,), (1,)), ((), ()))
_TN = (((0,), (0,)), ((), ()))


def _params(*sem):
    return pltpu.CompilerParams(dimension_semantics=sem, vmem_limit_bytes=VMEM_LIMIT)


def _tile(n, pref):
    t = min(n, pref)
    assert n % t == 0, (n, t)
    return t


def _sigmoid(v):
    return 1.0 / (1.0 + jnp.exp(-v))


def _rows8(v):
    r, n = v.shape
    return jnp.sum(v.reshape(r // SUBLANES, SUBLANES, n), axis=0)


def _dot(a, b):
    return jnp.dot(a, b, preferred_element_type=F32)


def _dot_nt(a, b):
    return lax.dot_general(a, b, _NT, preferred_element_type=F32)


def _dot_tn(a, b):
    return lax.dot_general(a, b, _TN, preferred_element_type=F32)


def _rms_bwd(xv, g, dh):
    r = lax.rsqrt(jnp.mean(xv * xv, axis=-1, keepdims=True) + EPS)
    xhat = xv * r
    dxhat = dh * g
    dx = r * (dxhat - xhat * jnp.mean(dxhat * xhat, axis=-1, keepdims=True))
    return dx, dh * xhat


def _row_spec(tm, n, col=0):
    return pl.BlockSpec((tm, n), lambda i: (i, col))


def _full_spec(shape):
    return pl.BlockSpec(shape, lambda *_: (0,) * len(shape))


def _weight_spec(shape):
    return pl.BlockSpec(shape, lambda *_: (0,) * len(shape), pipeline_mode=pl.Buffered(1))


def _mesh_pos():
    return lax.axis_index("x"), lax.axis_index("y"), lax.axis_index("c")


def _dev_index(dev):
    return 4 * dev[0] + 2 * dev[1] + dev[2]


class _Exchange:
    middle_at = None

    def __init__(self, arrs):
        self.arrays = list(arrs)

    def out_shape(self):
        return [jax.ShapeDtypeStruct(a.shape, a.dtype) for a in self.arrays]

    def scratch(self):
        n = len(self.arrays)
        return [pltpu.SemaphoreType.DMA((7 * n,)), pltpu.SemaphoreType.DMA((7 * n,)), pltpu.SemaphoreType.DMA((n,))]

    def _copies(self, ins, outs, sems):
        send_sems, recv_sems, local_sems = sems
        x, y, c = _mesh_pos()
        me = _dev_index((x, y, c))
        mine, sends, arrivals = [], [], []
        for p in range(len(self.arrays)):
            mine.append(pltpu.make_async_copy(ins[p].at[me], outs[p].at[me], local_sems.at[p]))
            for k in range(1, N_DEV):
                peer = (1 - x if k & 4 else x, 1 - y if k & 2 else y, 1 - c if k & 1 else c)
                pid = _dev_index(peer)
                pair = dict(send_sem=send_sems.at[7 * p + k - 1], recv_sem=recv_sems.at[7 * p + k - 1],
                            device_id=peer, device_id_type=MESH)
                sends.append(pltpu.make_async_remote_copy(src_ref=ins[p].at[pid], dst_ref=outs[p].at[me], **pair))
                arrivals.append(pltpu.make_async_remote_copy(src_ref=ins[p].at[pid], dst_ref=outs[p].at[pid], **pair))
        return mine, sends, arrivals

    def start(self, ins, outs, sems):
        mine, sends, _ = self._copies(ins, outs, sems)
        for cp in mine + sends:
            cp.start()

    def finish(self, ins, outs, sems):
        mine, sends, arrivals = self._copies(ins, outs, sems)
        for cp in arrivals:
            cp.wait_recv()
        for cp in sends:
            cp.wait_send()
        for cp in mine:
            cp.wait()


class _Gather:
    middle_at = 0.75

    def __init__(self, arrs):
        self.arrays = list(arrs)

    def out_shape(self):
        return [jax.ShapeDtypeStruct((N_DEV,) + a.shape, a.dtype) for a in self.arrays]

    def scratch(self):
        n = len(self.arrays)
        return [pltpu.SemaphoreType.DMA((7 * n,)), pltpu.SemaphoreType.DMA((7 * n,)), pltpu.SemaphoreType.DMA((n,))]

    def _copies(self, ins, outs, sems):
        send_sems, recv_sems, local_sems = sems
        x, y, c = _mesh_pos()
        me, sibling = (x, y, c), (x, y, 1 - c)
        chips = [(1 - x, y), (x, 1 - y), (1 - x, 1 - y)]
        n = len(self.arrays)

        def copy(p, k, dev, to, src=None):
            block = outs[p].at[_dev_index(dev)]
            return pltpu.make_async_remote_copy(
                src_ref=block if src is None else src, dst_ref=block,
                send_sem=send_sems.at[7 * p + k], recv_sem=recv_sems.at[7 * p + k],
                device_id=to, device_id_type=MESH)

        cp = dict(mine=[pltpu.make_async_copy(ins[p], outs[p].at[_dev_index(me)], local_sems.at[p])
                        for p in range(n)])
        cp["first"] = [copy(p, 0, me, sibling, src=ins[p]) for p in range(n)]
        cp["first"] += [copy(p, 1 + j, me, (*chip, c), src=ins[p]) for p in range(n) for j, chip in enumerate(chips)]
        cp["over_ici"] = [copy(p, 1 + j, (*chip, c), me) for j, chip in enumerate(chips) for p in range(n)]
        cp["passed"] = [copy(p, 4 + j, (*chip, c), sibling) for j, chip in enumerate(chips) for p in range(n)]
        cp["from_sibling"] = [copy(p, 0, sibling, me) for p in range(n)]
        cp["from_sibling"] += [copy(p, 4 + j, (*chip, 1 - c), me) for j, chip in enumerate(chips) for p in range(n)]
        return cp

    def start(self, ins, outs, sems):
        cp = self._copies(ins, outs, sems)
        for d in cp["mine"] + cp["first"]:
            d.start()

    def middle(self, ins, outs, sems):
        cp = self._copies(ins, outs, sems)
        for arrived, onward in zip(cp["over_ici"], cp["passed"]):
            arrived.wait_recv()
            onward.start()

    def finish(self, ins, outs, sems):
        cp = self._copies(ins, outs, sems)
        for d in cp["from_sibling"]:
            d.wait_recv()
        for d in cp["first"] + cp["passed"]:
            d.wait_send()
        for d in cp["mine"]:
            d.wait()


def _run_comm(comm, name):
    n = len(comm.arrays)

    def body(*refs):
        ins, outs, sems = refs[:n], refs[n:2 * n], refs[2 * n:]
        comm.start(ins, outs, sems)
        if comm.middle_at is not None:
            comm.middle(ins, outs, sems)
        comm.finish(ins, outs, sems)

    any_spec = pl.BlockSpec(memory_space=pl.ANY)
    return pl.pallas_call(
        body, name=name, in_specs=[any_spec] * n, out_specs=[any_spec] * n, out_shape=comm.out_shape(),
        scratch_shapes=comm.scratch(),
    )(*comm.arrays)


def _pallas(body, *, name, grid, in_specs, out_specs, out_shape, args, sem, scratch_shapes=(), comm=None):
    if comm is None:
        outs = pl.pallas_call(
            body, name=name, grid=grid, in_specs=in_specs, out_specs=out_specs, out_shape=out_shape,
            scratch_shapes=list(scratch_shapes), compiler_params=_params(*sem),
        )(*args)
        return outs, None
    n_in, n_out, n_scr, n_c = len(in_specs), len(out_specs), len(scratch_shapes), len(comm.arrays)
    steps = grid[0]
    middle = None if comm.middle_at is None else min(steps - 1, int(steps * comm.middle_at))

    def carried(*refs):
        ins, refs = refs[:n_in], refs[n_in:]
        cins, refs = refs[:n_c], refs[n_c:]
        outs, refs = refs[:n_out], refs[n_out:]
        couts, refs = refs[:n_c], refs[n_c:]
        scr, csems = refs[:n_scr], refs[n_scr:]
        step = pl.program_id(0)

        @pl.when(step == 0)
        def _():
            comm.start(cins, couts, csems)

        body(*ins, *outs, *scr)

        if middle is not None:
            @pl.when(step == middle)
            def _():
                comm.middle(cins, couts, csems)

        @pl.when(step == steps - 1)
        def _():
            comm.finish(cins, couts, csems)

    any_spec = pl.BlockSpec(memory_space=pl.ANY)
    res = pl.pallas_call(
        carried, name=name, grid=grid,
        in_specs=list(in_specs) + [any_spec] * n_c, out_specs=list(out_specs) + [any_spec] * n_c,
        out_shape=list(out_shape) + comm.out_shape(),
        scratch_shapes=list(scratch_shapes) + comm.scratch(),
        compiler_params=_params(*(("arbitrary",) + tuple(sem[1:]))),
    )(*args, *comm.arrays)
    return res[:n_out], res[n_out:]


def _inproj_fwd(x, g, w, b, cw, cb, lg, lb, comm=None):
    T, D = x.shape
    C = CONV_C
    rest_w = IN_W - QKV_W
    tm = _tile(T, 512)
    tp = _tile(tm, 256)

    def body(x_ref, g_ref, w_ref, b_ref, cw_ref, cb_ref, lg_ref, lb_ref, h_ref, qkv_ref, rest_ref, u0_ref, yc_ref,
             u_ref, ubuf, carry):
        @pl.when(pl.program_id(0) == 0)
        def _():
            carry[...] = jnp.zeros_like(carry)

        for part in range(tm // tp):
            rows = slice(part * tp, (part + 1) * tp)
            xv = x_ref[rows, :]
            r = lax.rsqrt(jnp.mean(xv * xv, axis=-1, keepdims=True) + EPS)
            h = (xv * r * g_ref[...]).astype(BF16)
            h_ref[rows, :] = h
            qkv_ref[rows, :] = (_dot(h, w_ref[:, :QKV_W]) + b_ref[:, :QKV_W]).astype(BF16)
            rest = (_dot(h, w_ref[:, QKV_W:]) + b_ref[:, QKV_W:]).astype(BF16)
            rest_ref[rows, :] = rest
            u0 = rest[:, :C].astype(F32) * _sigmoid(rest[:, C:2 * C].astype(F32))
            _conv_rows_fwd(u0, carry, ubuf, cw_ref, cb_ref, lg_ref, lb_ref, u0_ref.at[rows, :], yc_ref.at[rows, :],
                           u_ref.at[rows, :])

    return _pallas(
        body, name="inproj_fwd", grid=(T // tm,),
        in_specs=[_row_spec(tm, D), _full_spec((1, D)), _weight_spec((D, IN_W)), _full_spec((1, IN_W)),
                  _full_spec((CONV_K, C)), _full_spec((1, C)), _full_spec((1, C)), _full_spec((1, C))],
        out_specs=[_row_spec(tm, D), _row_spec(tm, QKV_W), _row_spec(tm, rest_w), _row_spec(tm, C),
                   _row_spec(tm, C), _row_spec(tm, C)],
        out_shape=[jax.ShapeDtypeStruct((T, D), BF16), jax.ShapeDtypeStruct((T, QKV_W), BF16),
                   jax.ShapeDtypeStruct((T, rest_w), BF16), jax.ShapeDtypeStruct((T, C), F32),
                   jax.ShapeDtypeStruct((T, C), F32), jax.ShapeDtypeStruct((T, C), BF16)],
        scratch_shapes=[pltpu.VMEM((SUBLANES, tp + HALO, C), F32), pltpu.VMEM((HALO, C), F32)],
        args=(x, g, w, b, cw, cb, lg, lb), sem=("arbitrary",), comm=comm)


def _fold_masks(first):
    shape = (BLOCK, GROUP * BLOCK)
    key = lax.broadcasted_iota(jnp.int32, shape, 0)
    qry = lax.broadcasted_iota(jnp.int32, shape, 1) & (BLOCK - 1)
    upper = key > qry
    dist = jnp.where(upper, qry + BLOCK - key, qry - key)
    keep = key <= qry + jnp.where(first, 0, BLOCK)
    return upper, dist.astype(F32), keep


def _fold(band, upper):
    return jnp.where(upper, band[:BLOCK, :], band[BLOCK:, :])


def _unfold(folded, upper):
    return jnp.concatenate([jnp.where(upper, folded, 0.0), jnp.where(upper, 0.0, folded)], axis=0)


def _head_row(values):
    return jnp.concatenate([jnp.full((1, BLOCK), v, F32) for v in values], axis=1)


def _head_consts(sink_ref, distf):
    bias, sink = [], []
    for kh in range(N_KV):
        heads = range(kh * GROUP, (kh + 1) * GROUP)
        bias.append(_head_row([2.0 ** (-8.0 * (h + 1) / N_Q) for h in heads]) * distf)
        sink.append(_head_row([sink_ref[h] for h in heads]))
    return bias, sink


def _heads_out(t):
    stacked = jnp.concatenate([t[:, g * BLOCK:(g + 1) * BLOCK] for g in range(GROUP)], axis=0)
    return stacked.T


def _fill_kv(kv_buf, cur_ref, prev_ref):
    scale = 1.0 / math.sqrt(HEAD_DIM)
    assert math.frexp(scale)[0] == 0.5
    for r0, ref in ((0, prev_ref), (BLOCK, cur_ref)):
        rows = ref.shape[0]
        kv_buf[r0:r0 + rows, :KV_W] = ref[:, ATTN_W:ATTN_W + KV_W] * scale
        kv_buf[r0:r0 + rows, KV_W:] = ref[:, ATTN_W + KV_W:]


def _group_rows(x, kh):
    return jnp.concatenate([x[:, h * HEAD_DIM:(h + 1) * HEAD_DIM] for h in range(kh * GROUP, (kh + 1) * GROUP)],
                           axis=0)


def _attn_probs(scores, bias, sink, upper, keep):
    s = _fold(scores, upper) - bias
    if keep is not None:
        s = jnp.where(keep, s, NEG)
    m = jnp.maximum(jnp.max(s, axis=0, keepdims=True), sink)
    p = jnp.exp(s - m)
    e = jnp.exp(sink - m)
    inv = 1.0 / (jnp.sum(p, axis=0, keepdims=True) + e)
    return p * inv, e * inv


def _attn_fwd(qkv, sinks):
    T = qkv.shape[0]
    tq = _tile(T, 512)
    nblk = tq // BLOCK

    def body(sink_ref, cur_ref, prev_ref, o_ref, kv_buf):
        _fill_kv(kv_buf, cur_ref, prev_ref)
        upper, distf, keep = _fold_masks(pl.program_id(0) == 0)
        bias, sink = _head_consts(sink_ref, distf)
        for j in range(nblk):
            band = kv_buf[j * BLOCK:(j + 2) * BLOCK, :]
            q = cur_ref[j * BLOCK:(j + 1) * BLOCK, :ATTN_W]
            outs = []
            for kh in range(N_KV):
                k = band[:, kh * HEAD_DIM:(kh + 1) * HEAD_DIM]
                v = band[:, KV_W + kh * HEAD_DIM:KV_W + (kh + 1) * HEAD_DIM]
                p, _ = _attn_probs(_dot_nt(k, _group_rows(q, kh)), bias[kh], sink[kh], upper,
                                   keep if j == 0 else None)
                outs.append(_heads_out(_dot_tn(v, _unfold(p, upper).astype(BF16))))
            o_ref[j * BLOCK:(j + 1) * BLOCK, :] = jnp.concatenate(outs, axis=1).astype(BF16)

    return pl.pallas_call(
        body, name="attn_fwd", grid=(T // tq,),
        in_specs=[pl.BlockSpec(memory_space=pltpu.SMEM),
                  _row_spec(tq, QKV_W),
                  pl.BlockSpec((BLOCK, QKV_W), lambda i: (jnp.maximum(i * nblk - 1, 0), 0))],
        out_specs=_row_spec(tq, ATTN_W),
        out_shape=jax.ShapeDtypeStruct((T, ATTN_W), BF16),
        scratch_shapes=[pltpu.VMEM((tq + BLOCK, 2 * KV_W), BF16)],
        compiler_params=_params("parallel"),
    )(sinks, qkv, qkv)


def _shifted_copies(buf):
    n = buf.shape[1] - SUBLANES
    for s in range(1, SUBLANES):
        buf[s, 0:n, :] = buf[0, s:s + n, :]


def _shifted_rows(buf, start, rows):
    s = start % SUBLANES
    return buf[s, start - s:start - s + rows, :]


def _conv_rows_fwd(u0, carry, ubuf, w_ref, cb_ref, g_ref, b_ref, u0_ref, yc_ref, u_ref):
    n, C = u0.shape
    R = _tile(n, CONV_ROWS)
    ubuf[0, 0:HALO, :] = carry[...]
    ubuf[0, HALO:, :] = u0
    carry[...] = u0[n - HALO:, :]
    u0_ref[...] = u0
    _shifted_copies(ubuf)
    off = HALO - (CONV_K - 1)
    for c in range(n // R):
        acc = jnp.broadcast_to(cb_ref[...], (R, C))
        for j in range(CONV_K):
            acc = acc + w_ref[j:j + 1, :] * _shifted_rows(ubuf, c * R + off + j, R)
        yc_ref[c * R:(c + 1) * R, :] = acc
        xc = acc - jnp.mean(acc, axis=-1, keepdims=True)
        ln = xc * lax.rsqrt(jnp.mean(xc * xc, axis=-1, keepdims=True) + EPS) * g_ref[...] + b_ref[...]
        u_ref[c * R:(c + 1) * R, :] = (ln * _sigmoid(ln)).astype(BF16)


def _merge_fwd(attn, u, rest, x, wa, wc, bc, wo):
    T, D = x.shape
    tm = _tile(T, 512)
    gcol = 2 * CONV_C // D

    def body(attn_ref, u_ref, ga_ref, gc_ref, x_ref, wa_ref, wc_ref, bc_ref, wo_ref, m_ref, x1_ref):
        bra = _dot(attn_ref[...], wa_ref[...])
        brc = _dot(u_ref[...], wc_ref[...]) + bc_ref[...]
        mb = (_sigmoid(ga_ref[...].astype(F32)) * bra + _sigmoid(gc_ref[...].astype(F32)) * brc).astype(BF16)
        m_ref[...] = mb
        x1_ref[...] = x_ref[...] + _dot(mb, wo_ref[...])

    return pl.pallas_call(
        body, name="merge_fwd", grid=(T // tm,),
        in_specs=[_row_spec(tm, ATTN_W), _row_spec(tm, CONV_C), _row_spec(tm, D, gcol), _row_spec(tm, D, gcol + 1),
                  _row_spec(tm, D), _weight_spec((ATTN_W, D)), _weight_spec((CONV_C, D)), _full_spec((1, D)),
                  _weight_spec((D, D))],
        out_specs=[_row_spec(tm, D), _row_spec(tm, D)],
        out_shape=[jax.ShapeDtypeStruct((T, D), BF16), jax.ShapeDtypeStruct((T, D), F32)],
        compiler_params=_params("parallel"),
    )(attn, u, rest, rest, x, wa, wc, bc, wo)


def _mlp_tile(x_ref, g_ref, w1_ref, w2_ref, h_ref, z_ref):
    fc = _tile(D_FF, FF_CHUNK)
    xv = x_ref[...]
    r = lax.rsqrt(jnp.mean(xv * xv, axis=-1, keepdims=True) + EPS)
    h = (xv * r * g_ref[...]).astype(BF16)
    h_ref[...] = h
    acc = xv
    for c in range(D_FF // fc):
        cols = slice(c * fc, (c + 1) * fc)
        z = _dot(h, w1_ref[:, cols])
        z_ref[:, cols] = z.astype(BF16)
        acc = acc + _dot(jnp.square(jnp.maximum(z, 0.0)).astype(BF16), w2_ref[cols, :])
    return acc


def _mlp_fwd(x1, g, w1, w2, comm=None):
    T, D = x1.shape
    tm = _tile(T, 512)

    def body(x_ref, g_ref, w1_ref, w2_ref, h_ref, z_ref, o_ref):
        o_ref[...] = _mlp_tile(x_ref, g_ref, w1_ref, w2_ref, h_ref, z_ref)

    return _pallas(
        body, name="mlp_fwd", grid=(T // tm,),
        in_specs=[_row_spec(tm, D), _full_spec((1, D)), _weight_spec((D, D_FF)), _weight_spec((D_FF, D))],
        out_specs=[_row_spec(tm, D), _row_spec(tm, D_FF), _row_spec(tm, D)],
        out_shape=[jax.ShapeDtypeStruct((T, D), BF16), jax.ShapeDtypeStruct((T, D_FF), BF16),
                   jax.ShapeDtypeStruct((T, D), F32)],
        args=(x1, g, w1, w2), sem=("parallel",), comm=comm)


def _mlp_fwd_loss(x1, g, w1, w2, gf, target):
    T, D = x1.shape
    tm = _tile(T, 512)

    def body(x_ref, g_ref, w1_ref, w2_ref, gf_ref, t_ref, h_ref, z_ref, l_ref, dx_ref, dg_ref):
        @pl.when(pl.program_id(0) == 0)
        def _():
            l_ref[...] = jnp.zeros_like(l_ref)
            dg_ref[...] = jnp.zeros_like(dg_ref)

        x2 = _mlp_tile(x_ref, g_ref, w1_ref, w2_ref, h_ref, z_ref)
        r = lax.rsqrt(jnp.mean(x2 * x2, axis=-1, keepdims=True) + EPS)
        e = x2 * r * gf_ref[...] - t_ref[...]
        l_ref[...] += _rows8(e * e) * (0.5 / D)
        dx, dg = _rms_bwd(x2, gf_ref[...], e * (1.0 / D))
        dx_ref[...] = dx
        dg_ref[...] += _rows8(dg)

    return pl.pallas_call(
        body, name="mlp_fwd_loss", grid=(T // tm,),
        in_specs=[_row_spec(tm, D), _full_spec((1, D)), _weight_spec((D, D_FF)), _weight_spec((D_FF, D)),
                  _full_spec((1, D)), _row_spec(tm, D)],
        out_specs=[_row_spec(tm, D), _row_spec(tm, D_FF), _full_spec((SUBLANES, D)), _row_spec(tm, D),
                   _full_spec((SUBLANES, D))],
        out_shape=[jax.ShapeDtypeStruct((T, D), BF16), jax.ShapeDtypeStruct((T, D_FF), BF16),
                   jax.ShapeDtypeStruct((SUBLANES, D), F32), jax.ShapeDtypeStruct((T, D), F32),
                   jax.ShapeDtypeStruct((SUBLANES, D), F32)],
        compiler_params=_params("arbitrary"),
    )(x1, g, w1, w2, gf, target)


def _mlp_bwd(dx2, x1, z, g, w1, w2, comm=None):
    T, D = x1.shape
    tm = _tile(T, 512)
    fc = _tile(D_FF, FF_CHUNK)

    def body(dx2_ref, x_ref, z_ref, g_ref, w1_ref, w2_ref, dx1_ref, dz_ref, dg_ref):
        @pl.when(pl.program_id(0) == 0)
        def _():
            dg_ref[...] = jnp.zeros_like(dg_ref)

        dxo = dx2_ref[...]
        dxb = dxo.astype(BF16)
        dh = jnp.zeros((tm, D), F32)
        for c in range(D_FF // fc):
            cols = slice(c * fc, (c + 1) * fc)
            da = _dot_nt(dxb, w2_ref[cols, :])
            dz = (da * (2.0 * jnp.maximum(z_ref[:, cols].astype(F32), 0.0))).astype(BF16)
            dz_ref[:, cols] = dz
            dh = dh + _dot_nt(dz, w1_ref[:, cols])
        dx, dg = _rms_bwd(x_ref[...], g_ref[...], dh)
        dx1_ref[...] = dxo + dx
        dg_ref[...] += _rows8(dg)

    return _pallas(
        body, name="mlp_bwd", grid=(T // tm,),
        in_specs=[_row_spec(tm, D), _row_spec(tm, D), _row_spec(tm, D_FF), _full_spec((1, D)),
                  _weight_spec((D, D_FF)), _weight_spec((D_FF, D))],
        out_specs=[_row_spec(tm, D), _row_spec(tm, D_FF), _full_spec((SUBLANES, D))],
        out_shape=[jax.ShapeDtypeStruct((T, D), F32), jax.ShapeDtypeStruct((T, D_FF), BF16),
                   jax.ShapeDtypeStruct((SUBLANES, D), F32)],
        args=(dx2, x1, z, g, w1, w2), sem=("arbitrary",), comm=comm)


def _tn_blocks(a, b, name, col_sharded, relu_sq=False):
    T, M = a.shape
    N = b.shape[1]
    tk = _tile(T, 1024)
    tm = _tile(M, 512 if col_sharded else 1024)
    nb = N // N_DEV
    last = T // tk - 1

    def body(a_ref, b_ref, o_ref, acc_ref):
        k = pl.program_id(1)

        @pl.when(k == 0)
        def _():
            acc_ref[...] = jnp.zeros_like(acc_ref)

        av = a_ref[...]
        if relu_sq:
            av = jnp.square(jnp.maximum(av, 0.0))
        acc_ref[...] += _dot_tn(av.astype(BF16), b_ref[...].astype(BF16))

        @pl.when(k == last)
        def _():
            if col_sharded:
                for d in range(N_DEV):
                    o_ref[d] = acc_ref[:, d * nb:(d + 1) * nb].astype(BF16)
            else:
                o_ref[...] = acc_ref[...].astype(BF16)

    if col_sharded:
        out_spec = pl.BlockSpec((N_DEV, tm, nb), lambda i, k: (0, i, 0))
        out_shape = jax.ShapeDtypeStruct((N_DEV, M, nb), BF16)
    else:
        out_spec = pl.BlockSpec((tm, N), lambda i, k: (i, 0))
        out_shape = jax.ShapeDtypeStruct((M, N), BF16)
    out = pl.pallas_call(
        body, name=name, grid=(M // tm, T // tk),
        in_specs=[pl.BlockSpec((tk, tm), lambda i, k: (k, i)), pl.BlockSpec((tk, N), lambda i, k: (k, 0))],
        out_specs=out_spec, out_shape=out_shape,
        scratch_shapes=[pltpu.VMEM((tm, N), F32)],
        compiler_params=_params("parallel", "arbitrary"),
    )(a, b)
    return out if col_sharded else out.reshape(N_DEV, M // N_DEV, N)


def _merge_bwd(dx1, attn, u, rest, yc, wa, wc, bc, wo, lg, lb, comm=None):
    T, D = dx1.shape
    C = CONV_C
    tm = _tile(T, 512)
    gcol = 2 * CONV_C // D

    def body(dx_ref, attn_ref, u_ref, ga_ref, gc_ref, yc_ref, wa_ref, wc_ref, bc_ref, wo_ref, lg_ref, lb_ref,
             dattn_ref, dyc_ref, dga_ref, dgc_ref, dbra_ref, dbrc_ref, dbc_ref, dlg_ref, dlb_ref):
        @pl.when(pl.program_id(0) == 0)
        def _():
            dbc_ref[...] = jnp.zeros_like(dbc_ref)
            dlg_ref[...] = jnp.zeros_like(dlg_ref)
            dlb_ref[...] = jnp.zeros_like(dlb_ref)

        dm = _dot_nt(dx_ref[...].astype(BF16), wo_ref[...])
        bra = _dot(attn_ref[...], wa_ref[...])
        brc = _dot(u_ref[...], wc_ref[...]) + bc_ref[...]
        sa = _sigmoid(ga_ref[...].astype(F32))
        sc = _sigmoid(gc_ref[...].astype(F32))
        dbra = dm * sa
        dbrc = dm * sc
        dga_ref[...] = (dm * bra * sa * (1.0 - sa)).astype(BF16)
        dgc_ref[...] = (dm * brc * sc * (1.0 - sc)).astype(BF16)
        dbra_b = dbra.astype(BF16)
        dbrc_b = dbrc.astype(BF16)
        dbra_ref[...] = dbra_b
        dbrc_ref[...] = dbrc_b
        dbc_ref[...] += _rows8(dbrc)
        dattn_ref[...] = _dot_nt(dbra_b, wa_ref[...]).astype(BF16)
        dyc, dlg, dlb = _swish_norm_bwd(_dot_nt(dbrc_b, wc_ref[...]), yc_ref[...], lg_ref[...], lb_ref[...])
        dyc_ref[...] = dyc
        dlg_ref[...] += _rows8(dlg)
        dlb_ref[...] += _rows8(dlb)

    return _pallas(
        body, name="merge_bwd", grid=(T // tm,),
        in_specs=[_row_spec(tm, D), _row_spec(tm, ATTN_W), _row_spec(tm, C), _row_spec(tm, D, gcol),
                  _row_spec(tm, D, gcol + 1), _row_spec(tm, C), _weight_spec((ATTN_W, D)), _weight_spec((C, D)),
                  _full_spec((1, D)), _weight_spec((D, D)), _full_spec((1, C)), _full_spec((1, C))],
        out_specs=[_row_spec(tm, ATTN_W), _row_spec(tm, C), _row_spec(tm, D), _row_spec(tm, D),
                   _row_spec(tm, D), _row_spec(tm, D), _full_spec((SUBLANES, D)), _full_spec((SUBLANES, C)),
                   _full_spec((SUBLANES, C))],
        out_shape=[jax.ShapeDtypeStruct((T, ATTN_W), BF16), jax.ShapeDtypeStruct((T, C), F32),
                   jax.ShapeDtypeStruct((T, D), BF16), jax.ShapeDtypeStruct((T, D), BF16),
                   jax.ShapeDtypeStruct((T, D), BF16), jax.ShapeDtypeStruct((T, D), BF16),
                   jax.ShapeDtypeStruct((SUBLANES, D), F32), jax.ShapeDtypeStruct((SUBLANES, C), F32),
                   jax.ShapeDtypeStruct((SUBLANES, C), F32)],
        args=(dx1, attn, u, rest, rest, yc, wa, wc, bc, wo, lg, lb), sem=("arbitrary",), comm=comm)


def _swish_norm_bwd(du, yv, g, b):
    xc = yv - jnp.mean(yv, axis=-1, keepdims=True)
    rstd = lax.rsqrt(jnp.mean(xc * xc, axis=-1, keepdims=True) + EPS)
    xn = xc * rstd
    ln = xn * g + b
    sg = _sigmoid(ln)
    dln = du * sg * (1.0 + ln * (1.0 - sg))
    dxn = dln * g
    dyc = rstd * (dxn - jnp.mean(dxn, axis=-1, keepdims=True) - xn * jnp.mean(dxn * xn, axis=-1, keepdims=True))
    return dyc, dln * xn, dln


def _conv_taps_bwd(first, last, dy_ref, dyn_ref, u0_ref, u0p_ref, glu_ref, w_ref, dglu_ref, dw_ref, db_ref,
                   dbuf, ubuf):
    tm, C = dy_ref.shape
    R = _tile(tm, CONV_ROWS)
    dbuf[0, 0:tm, :] = dy_ref[...]
    dbuf[0, tm:, :] = jnp.where(last, 0.0, dyn_ref[...])
    ubuf[0, 0:HALO, :] = jnp.where(first, 0.0, u0p_ref[...])
    ubuf[0, HALO:, :] = u0_ref[...]
    _shifted_copies(dbuf)
    _shifted_copies(ubuf)
    off = HALO - (CONV_K - 1)
    for c in range(tm // R):
        rows = slice(c * R, (c + 1) * R)
        dy = dbuf[0, rows, :]
        acc = jnp.zeros((R, C), F32)
        for j in range(CONV_K):
            acc = acc + w_ref[j:j + 1, :] * _shifted_rows(dbuf, c * R + CONV_K - 1 - j, R)
            dw_ref[j * SUBLANES:(j + 1) * SUBLANES, :] += _rows8(dy * _shifted_rows(ubuf, c * R + off + j, R))
        db_ref[...] += _rows8(dy)
        a = glu_ref[rows, :C].astype(F32)
        sb = _sigmoid(glu_ref[rows, C:].astype(F32))
        dglu_ref[rows, :C] = (acc * sb).astype(BF16)
        dglu_ref[rows, C:] = (acc * a * sb * (1.0 - sb)).astype(BF16)


def _attn_bwd(qkv, dattn, sinks, comm=None):
    T = qkv.shape[0]
    tq = _tile(T, 512)
    nblk = tq // BLOCK
    scale = 1.0 / math.sqrt(HEAD_DIM)

    def body(sink_ref, cur_ref, prev_ref, do_ref, dq_ref, hi_ref, lo_ref, ds_ref, kv_buf):
        i = pl.program_id(0)

        @pl.when(i == 0)
        def _():
            ds_ref[...] = jnp.zeros_like(ds_ref)

        _fill_kv(kv_buf, cur_ref, prev_ref)
        upper, distf, keep = _fold_masks(i == 0)
        bias, sink = _head_consts(sink_ref, distf)
        for j in range(nblk):
            rows = slice(j * BLOCK, (j + 1) * BLOCK)
            band = kv_buf[j * BLOCK:(j + 2) * BLOCK, :]
            q = cur_ref[rows, :ATTN_W]
            do = do_ref[rows, :]
            dqs, dks, dvs = [], [], []
            for kh in range(N_KV):
                k = band[:, kh * HEAD_DIM:(kh + 1) * HEAD_DIM]
                v = band[:, KV_W + kh * HEAD_DIM:KV_W + (kh + 1) * HEAD_DIM]
                qg = _group_rows(q, kh)
                dog = _group_rows(do, kh)
                p, psink = _attn_probs(_dot_nt(k, qg), bias[kh], sink[kh], upper, keep if j == 0 else None)
                pdp = p * _fold(_dot_nt(v, dog), upper)
                delta = jnp.sum(pdp, axis=0, keepdims=True)
                lanes = slice(kh * GROUP * BLOCK, (kh + 1) * GROUP * BLOCK)
                ds_ref[0:1, lanes] += psink * delta
                dsb = _unfold(pdp - p * delta, upper).astype(BF16)
                dqs.append(_heads_out(_dot_tn(k, dsb)))
                dks.append(_dot(dsb, qg) * scale)
                dvs.append(_dot(_unfold(p, upper).astype(BF16), dog))
            dq_ref[rows, :] = jnp.concatenate(dqs, axis=1).astype(BF16)
            dkv = jnp.concatenate(dks + dvs, axis=1)
            lo_ref[rows, :] = dkv[:BLOCK, :]
            hi_ref[rows, :] = dkv[BLOCK:, :]

    return _pallas(
        body, name="attn_bwd", grid=(T // tq,),
        in_specs=[pl.BlockSpec(memory_space=pltpu.SMEM),
                  _row_spec(tq, QKV_W),
                  pl.BlockSpec((BLOCK, QKV_W), lambda i: (jnp.maximum(i * nblk - 1, 0), 0)),
                  _row_spec(tq, ATTN_W)],
        out_specs=[_row_spec(tq, ATTN_W), _row_spec(tq, 2 * KV_W), _row_spec(tq, 2 * KV_W),
                   _full_spec((SUBLANES, N_Q * BLOCK))],
        out_shape=[jax.ShapeDtypeStruct((T, ATTN_W), BF16), jax.ShapeDtypeStruct((T, 2 * KV_W), F32),
                   jax.ShapeDtypeStruct((T, 2 * KV_W), F32), jax.ShapeDtypeStruct((SUBLANES, N_Q * BLOCK), F32)],
        scratch_shapes=[pltpu.VMEM((tq + BLOCK, 2 * KV_W), BF16)],
        args=(sinks, qkv, qkv, dattn), sem=("arbitrary",), comm=comm)


def _inproj_bwd(dq, hi, lo, dyc, u0, rest, cw, dga, dgc, x, g, w, dx1, comm=None):
    T, D = x.shape
    C = CONV_C
    tm = _tile(T, 256)
    per = tm // BLOCK
    per_halo = tm // HALO
    nt = T // tm
    kv2 = 2 * KV_W
    glu0, gate0 = QKV_W, QKV_W + 2 * C

    def body(dq_ref, hi_ref, lo_ref, lon_ref, dy_ref, dyn_ref, u0_ref, u0p_ref, glu_ref, cw_ref, dga_ref, dgc_ref,
             x_ref, g_ref, w_ref, dx1_ref, dp_ref, dx_ref, dg_ref, dbias_ref, dcw_ref, dcb_ref, dbuf, ubuf):
        i = pl.program_id(0)

        @pl.when(i == 0)
        def _():
            for ref in (dg_ref, dbias_ref, dcw_ref, dcb_ref):
                ref[...] = jnp.zeros_like(ref)

        def part(cols):
            dp = dp_ref[:, cols]
            dbias_ref[:, cols] += _rows8(dp.astype(F32))
            return _dot_nt(dp, w_ref[:, cols])

        dp_ref[:, :ATTN_W] = dq_ref[...]
        lo_next = jnp.where(i < nt - 1, lon_ref[...], 0.0)
        lo_shift = jnp.concatenate([lo_ref[BLOCK:, :], lo_next], axis=0) if tm > BLOCK else lo_next
        dp_ref[:, ATTN_W:QKV_W] = (hi_ref[...] + lo_shift).astype(BF16)
        dp_ref[:, gate0:gate0 + D] = dga_ref[...]
        dp_ref[:, gate0 + D:] = dgc_ref[...]
        _conv_taps_bwd(i == 0, i == nt - 1, dy_ref, dyn_ref, u0_ref, u0p_ref, glu_ref, cw_ref,
                       dp_ref.at[:, glu0:gate0], dcw_ref, dcb_ref, dbuf, ubuf)
        dh = part(slice(0, QKV_W)) + part(slice(gate0, IN_W)) + part(slice(glu0, gate0))
        dx, dg = _rms_bwd(x_ref[...], g_ref[...], dh)
        dx_ref[...] = dx1_ref[...] + dx
        dg_ref[...] += _rows8(dg)

    return _pallas(
        body, name="inproj_bwd", grid=(nt,),
        in_specs=[_row_spec(tm, ATTN_W), _row_spec(tm, kv2), _row_spec(tm, kv2),
                  pl.BlockSpec((BLOCK, kv2), lambda i: (jnp.minimum((i + 1) * per, T // BLOCK - 1), 0)),
                  _row_spec(tm, C),
                  pl.BlockSpec((HALO, C), lambda i: (jnp.minimum((i + 1) * per_halo, T // HALO - 1), 0)),
                  _row_spec(tm, C),
                  pl.BlockSpec((HALO, C), lambda i: (jnp.maximum(i * per_halo - 1, 0), 0)),
                  _row_spec(tm, 2 * C), _full_spec((CONV_K, C)),
                  _row_spec(tm, D), _row_spec(tm, D), _row_spec(tm, D), _full_spec((1, D)),
                  _weight_spec((D, IN_W)), _row_spec(tm, D)],
        out_specs=[_row_spec(tm, IN_W), _row_spec(tm, D), _full_spec((SUBLANES, D)), _full_spec((SUBLANES, IN_W)),
                   _full_spec((CONV_K * SUBLANES, C)), _full_spec((SUBLANES, C))],
        out_shape=[jax.ShapeDtypeStruct((T, IN_W), BF16), jax.ShapeDtypeStruct((T, D), F32),
                   jax.ShapeDtypeStruct((SUBLANES, D), F32), jax.ShapeDtypeStruct((SUBLANES, IN_W), F32),
                   jax.ShapeDtypeStruct((CONV_K * SUBLANES, C), F32), jax.ShapeDtypeStruct((SUBLANES, C), F32)],
        scratch_shapes=[pltpu.VMEM((SUBLANES, tm + HALO, C), F32), pltpu.VMEM((SUBLANES, tm + HALO, C), F32)],
        args=(dq, hi, lo, lo, dyc, dyc, u0, u0, rest, cw, dga, dgc, x, g, w, dx1), sem=("arbitrary",), comm=comm)


def _adamw_math(g, w, m, v):
    c1 = 1.0 / (1.0 - ADAM_B1 ** ADAM_STEP)
    c2 = 1.0 / (1.0 - ADAM_B2 ** ADAM_STEP)
    mn = ADAM_B1 * m + (1.0 - ADAM_B1) * g
    vn = ADAM_B2 * v + (1.0 - ADAM_B2) * (g * g)
    return -ADAM_LR * ((mn * c1) / (jnp.sqrt(vn * c2) + ADAM_EPS) + ADAM_WD * w), mn, vn


def _adamw_sharded(parts, w, m, v, name):
    depth, a, b = w.shape
    tr = _tile(a, 256) if a % SUBLANES == 0 else a
    nr = a // tr

    def body(*refs):
        p_refs, (w_ref, m_ref, v_ref, g_ref, d_ref, mo_ref, vo_ref) = refs[:depth], refs[depth:]
        layer = pl.program_id(0)
        for l in range(depth):
            @pl.when(layer == l)
            def _(l=l):
                g = p_refs[l][0].astype(F32)
                for s in range(1, N_DEV):
                    g = g + p_refs[l][s].astype(F32)
                g_ref[...] = g
                d_ref[...], mo_ref[...], vo_ref[...] = _adamw_math(g, w_ref[...], m_ref[...], v_ref[...])

    def part_spec(l):
        return pl.BlockSpec((N_DEV, tr, b),
                            lambda k, i: (0, jnp.where(k == l, i, jnp.where(k < l, 0, nr - 1)), 0))

    spec = pl.BlockSpec((None, tr, b), lambda k, i: (k, i, 0))
    out = jax.ShapeDtypeStruct((depth, a, b), F32)
    return pl.pallas_call(
        body, name=name, grid=(depth, nr),
        in_specs=[part_spec(l) for l in range(depth)] + [spec] * 3,
        out_specs=[spec] * 4, out_shape=[out] * 4,
        compiler_params=_params("arbitrary", "arbitrary"),
    )(*parts, w, m, v)


def _adamw_small(parts, w, m, v):
    R, N = w.shape

    def body(p_ref, w_ref, m_ref, v_ref, g_ref, d_ref, mo_ref, vo_ref):
        g = p_ref[0]
        for s in range(1, N_DEV):
            g = g + p_ref[s]
        g_ref[...] = g
        d_ref[...], mo_ref[...], vo_ref[...] = _adamw_math(g, w_ref[...], m_ref[...], v_ref[...])

    out = jax.ShapeDtypeStruct((R, N), F32)
    return pl.pallas_call(
        body, name="adamw_small", grid=(1,),
        in_specs=[_full_spec((N_DEV, R, N))] + [_full_spec((R, N))] * 3,
        out_specs=[_full_spec((R, N))] * 4, out_shape=[out] * 4,
        compiler_params=_params("arbitrary"),
    )(parts, w, m, v)


_SHARDED = ("w_in", "conv_w", "w_attn_proj", "w_conv_proj", "w_out", "w_mlp1", "w_mlp2")
_ROW_SHARDED = ("w_out", "w_mlp2")
_FIRST = ("w_in", "conv_w")
_REST = tuple(n for n in _SHARDED if n not in _FIRST)
_SMALL = ("mix_norm_g", "b_in", "sinks", "conv_b", "conv_ln_g", "conv_ln_b", "b_conv_proj", "mlp_norm_g",
          "final_norm_g")
_ORDER = ("mix_norm_g", "w_in", "b_in", "sinks", "conv_w", "conv_b", "conv_ln_g", "conv_ln_b", "w_attn_proj",
          "w_conv_proj", "b_conv_proj", "w_out", "mlp_norm_g", "w_mlp1", "w_mlp2", "final_norm_g")
_PACK = 1024


def _full_weights(names, gathered):
    cols = [i for i, n in enumerate(names) if n not in _ROW_SHARDED]

    def body(*refs):
        for src, dst in zip(refs[:len(cols)], refs[len(cols):]):
            b = src.shape[2]
            for d in range(N_DEV):
                dst[:, d * b:(d + 1) * b] = src[d]

    vmem = pl.BlockSpec(memory_space=pltpu.VMEM)
    placed = pl.pallas_call(
        body, name="place_" + names[cols[0]], in_specs=[vmem] * len(cols), out_specs=[vmem] * len(cols),
        out_shape=[jax.ShapeDtypeStruct((gathered[i].shape[1], N_DEV * gathered[i].shape[2]), gathered[i].dtype)
                   for i in cols],
        compiler_params=pltpu.CompilerParams(vmem_limit_bytes=VMEM_LIMIT),
    )(*[gathered[i] for i in cols])
    full = {names[i]: a for i, a in zip(cols, placed)}
    for n, a in zip(names, gathered):
        if n in _ROW_SHARDED:
            full[n] = a.reshape(N_DEV * a.shape[1], a.shape[2])
    return full


def _pack(arrs):
    flat = []
    for a in arrs:
        a = a.reshape(-1)
        flat.append(jnp.pad(a, (0, -a.size % _PACK)))
    return jnp.concatenate(flat).reshape(-1, BLOCK)


def _unpack(packed, shapes):
    flat = packed.reshape(-1)
    out, off = [], 0
    for s in shapes:
        n = math.prod(s)
        out.append(flat[off:off + n].reshape(s))
        off += n + (-n % _PACK)
    return out


def _layer_fwd(x, lw, own_rest=None, comm=None, loss_head=None):
    (h, qkv, rest, u0, yc, u), got = _inproj_fwd(x, lw["mix_norm_g"], lw["w_in"], lw["b_in"], lw["conv_w"],
                                                 lw["conv_b"], lw["conv_ln_g"], lw["conv_ln_b"], own_rest)
    if got is not None:
        lw.update(_full_weights(_REST, got))
    attn = _attn_fwd(qkv, lw["sinks"])
    merged, x1 = _merge_fwd(attn, u, rest, x, lw["w_attn_proj"], lw["w_conv_proj"], lw["b_conv_proj"], lw["w_out"])
    if loss_head is None:
        (h2, z, out), gathered = _mlp_fwd(x1, lw["mlp_norm_g"], lw["w_mlp1"], lw["w_mlp2"], comm)
    else:
        assert comm is None
        h2, z, *out = _mlp_fwd_loss(x1, lw["mlp_norm_g"], lw["w_mlp1"], lw["w_mlp2"], *loss_head)
        gathered = None
    saved = dict(x=x, h=h, qkv=qkv, rest=rest, attn=attn, u0=u0, yc=yc, u=u, merged=merged, x1=x1, h2=h2, z=z)
    return out, saved, gathered


_EARLY = ("w_mlp1", "w_mlp2")
_MIDDLE = ("w_out", "w_attn_proj", "w_conv_proj")
_LATE = ("w_in", "conv_w")


def _layer_bwd(dx2, lw, s, late_blocks, dx_is_result):
    g, recv = {}, {}
    late = None if late_blocks is None else _Exchange(late_blocks)
    (dx1, dz, dg2), late_recv = _mlp_bwd(dx2, s["x1"], s["z"], lw["mlp_norm_g"], lw["w_mlp1"], lw["w_mlp2"], late)
    g["mlp_norm_g"] = jnp.sum(dg2, axis=0)
    early = _Exchange([_tn_blocks(s["h2"], dz, "dw_mlp1", True),
                       _tn_blocks(s["z"], dx2, "dw_mlp2", False, relu_sq=True)])
    (dattn, dyc, dga, dgc, dbra, dbrc, dbc, dlg, dlb), early_recv = _merge_bwd(
        dx1, s["attn"], s["u"], s["rest"], s["yc"], lw["w_attn_proj"], lw["w_conv_proj"], lw["b_conv_proj"],
        lw["w_out"], lw["conv_ln_g"], lw["conv_ln_b"], early if dx_is_result else None)
    g["b_conv_proj"] = jnp.sum(dbc, axis=0)
    g["conv_ln_g"] = jnp.sum(dlg, axis=0)
    g["conv_ln_b"] = jnp.sum(dlb, axis=0)
    middle = [_tn_blocks(s["merged"], dx1, "dw_out", False), _tn_blocks(s["attn"], dbra, "dw_attn_proj", True),
              _tn_blocks(s["u"], dbrc, "dw_conv_proj", True)]
    (dq, hi, lo, dsk), middle_recv = _attn_bwd(s["qkv"], dattn, lw["sinks"], _Exchange(middle))
    recv.update(zip(_MIDDLE, middle_recv))
    g["sinks"] = -jnp.sum(dsk[0].reshape(N_Q, BLOCK), axis=1)
    (dproj, dx, dg1, dbin, dcw, dcb), behind_inproj = _inproj_bwd(
        dq, hi, lo, dyc, s["u0"], s["rest"], lw["conv_w"], dga, dgc, s["x"], lw["mix_norm_g"], lw["w_in"], dx1,
        None if dx_is_result else early)
    recv.update(zip(_EARLY, early_recv if dx_is_result else behind_inproj))
    dconv_w = jnp.sum(dcw.reshape(CONV_K, SUBLANES, CONV_C), axis=1)
    g["conv_b"] = jnp.sum(dcb, axis=0)
    g["mix_norm_g"] = jnp.sum(dg1, axis=0)
    g["b_in"] = jnp.sum(dbin, axis=0)
    own_late = [_tn_blocks(s["h"], dproj, "dw_in", True),
                dconv_w.reshape(CONV_K, N_DEV, CONV_C // N_DEV).transpose(1, 0, 2)]
    return dx, g, recv, late_recv, own_late


def kernel(x, mix_norm_g, w_in, b_in, sinks, conv_w, conv_b, conv_ln_g, conv_ln_b, w_attn_proj, w_conv_proj, b_conv_proj, w_out, mlp_norm_g, w_mlp1, w_mlp2, final_norm_g, loss_target, m_mix_norm_g, m_w_in, m_b_in, m_sinks, m_conv_w, m_conv_b, m_conv_ln_g, m_conv_ln_b, m_w_attn_proj, m_w_conv_proj, m_b_conv_proj, m_w_out, m_mlp_norm_g, m_w_mlp1, m_w_mlp2, m_final_norm_g, v_mix_norm_g, v_w_in, v_b_in, v_sinks, v_conv_w, v_conv_b, v_conv_ln_g, v_conv_ln_b, v_w_attn_proj, v_w_conv_proj, v_b_conv_proj, v_w_out, v_mlp_norm_g, v_w_mlp1, v_w_mlp2, v_final_norm_g):
    w = dict(mix_norm_g=mix_norm_g, w_in=w_in, b_in=b_in, sinks=sinks, conv_w=conv_w, conv_b=conv_b,
             conv_ln_g=conv_ln_g, conv_ln_b=conv_ln_b, w_attn_proj=w_attn_proj, w_conv_proj=w_conv_proj,
             b_conv_proj=b_conv_proj, w_out=w_out, mlp_norm_g=mlp_norm_g, w_mlp1=w_mlp1, w_mlp2=w_mlp2,
             final_norm_g=final_norm_g)
    m = dict(mix_norm_g=m_mix_norm_g, w_in=m_w_in, b_in=m_b_in, sinks=m_sinks, conv_w=m_conv_w, conv_b=m_conv_b,
             conv_ln_g=m_conv_ln_g, conv_ln_b=m_conv_ln_b, w_attn_proj=m_w_attn_proj, w_conv_proj=m_w_conv_proj,
             b_conv_proj=m_b_conv_proj, w_out=m_w_out, mlp_norm_g=m_mlp_norm_g, w_mlp1=m_w_mlp1, w_mlp2=m_w_mlp2,
             final_norm_g=m_final_norm_g)
    v = dict(mix_norm_g=v_mix_norm_g, w_in=v_w_in, b_in=v_b_in, sinks=v_sinks, conv_w=v_conv_w, conv_b=v_conv_b,
             conv_ln_g=v_conv_ln_g, conv_ln_b=v_conv_ln_b, w_attn_proj=v_w_attn_proj, w_conv_proj=v_w_conv_proj,
             b_conv_proj=v_b_conv_proj, w_out=v_w_out, mlp_norm_g=v_mlp_norm_g, w_mlp1=v_w_mlp1, w_mlp2=v_w_mlp2,
             final_norm_g=v_final_norm_g)
    T = x.shape[1]
    xs = x.reshape(T, D_MODEL)
    target = loss_target.reshape(T, D_MODEL)

    def gather_of(l, names):
        return _Gather([w[n][l] if n == "conv_w" else w[n][l].astype(BF16) for n in names])

    def layer_weights(l, names, gathered):
        lw = _full_weights(names, gathered)
        for n in _SMALL:
            if n != "final_norm_g":
                lw[n] = w[n][l] if n == "sinks" else w[n][l].reshape(1, -1)
        return lw

    acts = xs
    saved, weights = [], []
    for l in range(DEPTH):
        last = l + 1 == DEPTH
        following = None if last else gather_of(l + 1, _SHARDED)
        loss_head = (final_norm_g.reshape(1, -1), target) if last else None
        if l == 0:
            lw = layer_weights(0, _FIRST, _run_comm(gather_of(0, _FIRST), "gather_first"))
            acts, s, gathered = _layer_fwd(acts, lw, gather_of(0, _REST), following, loss_head)
        else:
            lw = layer_weights(l, _SHARDED, gathered)
            acts, s, gathered = _layer_fwd(acts, lw, None, following, loss_head)
        weights.append(lw)
        saved.append(s)
    lterms, dx, dgf = acts
    grads, received = [None] * DEPTH, [None] * DEPTH
    late = None
    for l in reversed(range(DEPTH)):
        dx, grads[l], received[l], late_recv, late = _layer_bwd(dx, weights[l], saved[l], late, l == 0)
        if late_recv is not None:
            received[l + 1].update(zip(_LATE, late_recv))
    received[0].update(zip(_LATE, _run_comm(_Exchange(late), "scatter_late")))
    grad = {n: jnp.stack([grads[l][n] for l in range(DEPTH)]) for n in _SMALL if n != "final_norm_g"}
    grad["final_norm_g"] = jnp.sum(dgf, axis=0)

    small_shapes = [w[n].shape for n in _SMALL] + [(1,)]
    small = _pack([grad[n] for n in _SMALL] + [jnp.sum(lterms).reshape(1)])
    small_parts = _run_comm(_Gather([small]), "gather_small")[0]

    out_g, out_d, out_m, out_v = {}, {}, {}, {}
    for n in _SHARDED:
        out_g[n], out_d[n], out_m[n], out_v[n] = _adamw_sharded(
            [received[l][n] for l in range(DEPTH)], w[n], m[n], v[n], "adamw_" + n)
    zero = jnp.zeros((1,), F32)
    res = _adamw_small(small_parts, _pack([w[n] for n in _SMALL] + [zero]), _pack([m[n] for n in _SMALL] + [zero]),
                       _pack([v[n] for n in _SMALL] + [zero]))
    unpacked = [_unpack(r, small_shapes) for r in res]
    for i, n in enumerate(_SMALL):
        out_g[n], out_d[n], out_m[n], out_v[n] = (u[i] for u in unpacked)
    loss = unpacked[0][-1].reshape(())
    return (loss, dx.reshape(x.shape), *[out_g[n] for n in _ORDER], *[out_d[n] for n in _ORDER],
            *[out_m[n] for n in _ORDER], *[out_v[n] for n in _ORDER])
```

```python
import functools
import math

import jax
import jax.numpy as jnp
from jax import lax
from jax.experimental import pallas as pl
from jax.experimental.pallas import tpu as pltpu

D_MODEL = 1024
SEQ = 16384
DEPTH = 2
N_Q = 8
N_KV = 2
GROUP = N_Q // N_KV
HEAD_DIM = 64
ATTN_W = N_Q * HEAD_DIM
KV_W = N_KV * HEAD_DIM
BLOCK = 128
CONV_C = D_MODEL // 2
CONV_K = 31
D_FF = 4 * D_MODEL
QKV_W = ATTN_W + 2 * KV_W
IN_W = QKV_W + 2 * CONV_C + 2 * D_MODEL
EPS = 1e-6
NEG = -1e30
N_DEV = 8

ADAM_LR = 0.001
ADAM_B1 = 0.9
ADAM_B2 = 0.999
ADAM_EPS = 1e-08
ADAM_WD = 0.01
ADAM_STEP = 10

F32 = jnp.float32
BF16 = jnp.bfloat16
MESH = pl.DeviceIdType.MESH

SUBLANES = 8
HALO = 32
FF_CHUNK = 1024
CONV_ROWS = 16
VMEM_LIMIT = 52 * 1024 * 1024

_NT = (((1,), (1,)), ((), ()))
_TN = (((0,), (0,)), ((), ()))


def _params(*sem):
    return pltpu.CompilerParams(dimension_semantics=sem, vmem_limit_bytes=VMEM_LIMIT)


def _tile(n, pref):
    t = min(n, pref)
    assert n % t == 0, (n, t)
    return t


def _sigmoid(v):
    return 1.0 / (1.0 + jnp.exp(-v))


def _rows8(v):
    r, n = v.shape
    return jnp.sum(v.reshape(r // SUBLANES, SUBLANES, n), axis=0)


def _dot(a, b):
    return jnp.dot(a, b, preferred_element_type=F32)


def _dot_nt(a, b):
    return lax.dot_general(a, b, _NT, preferred_element_type=F32)


def _dot_tn(a, b):
    return lax.dot_general(a, b, _TN, preferred_element_type=F32)


def _rms_bwd(xv, g, dh):
    r = lax.rsqrt(jnp.mean(xv * xv, axis=-1, keepdims=True) + EPS)
    xhat = xv * r
    dxhat = dh * g
    dx = r * (dxhat - xhat * jnp.mean(dxhat * xhat, axis=-1, keepdims=True))
    return dx, dh * xhat


def _row_spec(tm, n, col=0):
    return pl.BlockSpec((tm, n), lambda i: (i, col))


def _full_spec(shape):
    return pl.BlockSpec(shape, lambda *_: (0,) * len(shape))


def _weight_spec(shape):
    return pl.BlockSpec(shape, lambda *_: (0,) * len(shape), pipeline_mode=pl.Buffered(1))


def _mesh_pos():
    return lax.axis_index("x"), lax.axis_index("y"), lax.axis_index("c")


def _dev_index(dev):
    return 4 * dev[0] + 2 * dev[1] + dev[2]


class _Exchange:
    middle_at = None

    def __init__(self, arrs):
        self.arrays = list(arrs)

    def out_shape(self):
        return [jax.ShapeDtypeStruct(a.shape, a.dtype) for a in self.arrays]

    def scratch(self):
        n = len(self.arrays)
        return [pltpu.SemaphoreType.DMA((7 * n,)), pltpu.SemaphoreType.DMA((7 * n,)), pltpu.SemaphoreType.DMA((n,))]

    def _copies(self, ins, outs, sems):
        send_sems, recv_sems, local_sems = sems
        x, y, c = _mesh_pos()
        me = _dev_index((x, y, c))
        mine, sends, arrivals = [], [], []
        for p in range(len(self.arrays)):
            mine.append(pltpu.make_async_copy(ins[p].at[me], outs[p].at[me], local_sems.at[p]))
            for k in range(1, N_DEV):
                peer = (1 - x if k & 4 else x, 1 - y if k & 2 else y, 1 - c if k & 1 else c)
                pid = _dev_index(peer)
                pair = dict(send_sem=send_sems.at[7 * p + k - 1], recv_sem=recv_sems.at[7 * p + k - 1],
                            device_id=peer, device_id_type=MESH)
                sends.append(pltpu.make_async_remote_copy(src_ref=ins[p].at[pid], dst_ref=outs[p].at[me], **pair))
                arrivals.append(pltpu.make_async_remote_copy(src_ref=ins[p].at[pid], dst_ref=outs[p].at[pid], **pair))
        return mine, sends, arrivals

    def start(self, ins, outs, sems):
        mine, sends, _ = self._copies(ins, outs, sems)
        for cp in mine + sends:
            cp.start()

    def finish(self, ins, outs, sems):
        mine, sends, arrivals = self._copies(ins, outs, sems)
        for cp in arrivals:
            cp.wait_recv()
        for cp in sends:
            cp.wait_send()
        for cp in mine:
            cp.wait()


class _Gather:
    middle_at = 0.75

    def __init__(self, arrs):
        self.arrays = list(arrs)

    def out_shape(self):
        return [jax.ShapeDtypeStruct((N_DEV,) + a.shape, a.dtype) for a in self.arrays]

    def scratch(self):
        n = len(self.arrays)
        return [pltpu.SemaphoreType.DMA((7 * n,)), pltpu.SemaphoreType.DMA((7 * n,)), pltpu.SemaphoreType.DMA((n,))]

    def _copies(self, ins, outs, sems):
        send_sems, recv_sems, local_sems = sems
        x, y, c = _mesh_pos()
        me, sibling = (x, y, c), (x, y, 1 - c)
        chips = [(1 - x, y), (x, 1 - y), (1 - x, 1 - y)]
        n = len(self.arrays)

        def copy(p, k, dev, to, src=None):
            block = outs[p].at[_dev_index(dev)]
            return pltpu.make_async_remote_copy(
                src_ref=block if src is None else src, dst_ref=block,
                send_sem=send_sems.at[7 * p + k], recv_sem=recv_sems.at[7 * p + k],
                device_id=to, device_id_type=MESH)

        cp = dict(mine=[pltpu.make_async_copy(ins[p], outs[p].at[_dev_index(me)], local_sems.at[p])
                        for p in range(n)])
        cp["first"] = [copy(p, 0, me, sibling, src=ins[p]) for p in range(n)]
        cp["first"] += [copy(p, 1 + j, me, (*chip, c), src=ins[p]) for p in range(n) for j, chip in enumerate(chips)]
        cp["over_ici"] = [copy(p, 1 + j, (*chip, c), me) for j, chip in enumerate(chips) for p in range(n)]
        cp["passed"] = [copy(p, 4 + j, (*chip, c), sibling) for j, chip in enumerate(chips) for p in range(n)]
        cp["from_sibling"] = [copy(p, 0, sibling, me) for p in range(n)]
        cp["from_sibling"] += [copy(p, 4 + j, (*chip, 1 - c), me) for j, chip in enumerate(chips) for p in range(n)]
        return cp

    def start(self, ins, outs, sems):
        cp = self._copies(ins, outs, sems)
        for d in cp["mine"] + cp["first"]:
            d.start()

    def middle(self, ins, outs, sems):
        cp = self._copies(ins, outs, sems)
        for arrived, onward in zip(cp["over_ici"], cp["passed"]):
            arrived.wait_recv()
            onward.start()

    def finish(self, ins, outs, sems):
        cp = self._copies(ins, outs, sems)
        for d in cp["from_sibling"]:
            d.wait_recv()
        for d in cp["first"] + cp["passed"]:
            d.wait_send()
        for d in cp["mine"]:
            d.wait()


def _run_comm(comm, name):
    n = len(comm.arrays)

    def body(*refs):
        ins, outs, sems = refs[:n], refs[n:2 * n], refs[2 * n:]
        comm.start(ins, outs, sems)
        if comm.middle_at is not None:
            comm.middle(ins, outs, sems)
        comm.finish(ins, outs, sems)

    any_spec = pl.BlockSpec(memory_space=pl.ANY)
    return pl.pallas_call(
        body, name=name, in_specs=[any_spec] * n, out_specs=[any_spec] * n, out_shape=comm.out_shape(),
        scratch_shapes=comm.scratch(),
    )(*comm.arrays)


def _pallas(body, *, name, grid, in_specs, out_specs, out_shape, args, sem, scratch_shapes=(), comm=None):
    if comm is None:
        outs = pl.pallas_call(
            body, name=name, grid=grid, in_specs=in_specs, out_specs=out_specs, out_shape=out_shape,
            scratch_shapes=list(scratch_shapes), compiler_params=_params(*sem),
        )(*args)
        return outs, None
    n_in, n_out, n_scr, n_c = len(in_specs), len(out_specs), len(scratch_shapes), len(comm.arrays)
    steps = grid[0]
    middle = None if comm.middle_at is None else min(steps - 1, int(steps * comm.middle_at))

    def carried(*refs):
        ins, refs = refs[:n_in], refs[n_in:]
        cins, refs = refs[:n_c], refs[n_c:]
        outs, refs = refs[:n_out], refs[n_out:]
        couts, refs = refs[:n_c], refs[n_c:]
        scr, csems = refs[:n_scr], refs[n_scr:]
        step = pl.program_id(0)

        @pl.when(step == 0)
        def _():
            comm.start(cins, couts, csems)

        body(*ins, *outs, *scr)

        if middle is not None:
            @pl.when(step == middle)
            def _():
                comm.middle(cins, couts, csems)

        @pl.when(step == steps - 1)
        def _():
            comm.finish(cins, couts, csems)

    any_spec = pl.BlockSpec(memory_space=pl.ANY)
    res = pl.pallas_call(
        carried, name=name, grid=grid,
        in_specs=list(in_specs) + [any_spec] * n_c, out_specs=list(out_specs) + [any_spec] * n_c,
        out_shape=list(out_shape) + comm.out_shape(),
        scratch_shapes=list(scratch_shapes) + comm.scratch(),
        compiler_params=_params(*(("arbitrary",) + tuple(sem[1:]))),
    )(*args, *comm.arrays)
    return res[:n_out], res[n_out:]


def _inproj_fwd(x, g, w, b, cw, cb, lg, lb, comm=None):
    T, D = x.shape
    C = CONV_C
    rest_w = IN_W - QKV_W
    tm = _tile(T, 512)
    tp = _tile(tm, 256)

    def body(x_ref, g_ref, w_ref, b_ref, cw_ref, cb_ref, lg_ref, lb_ref, h_ref, qkv_ref, rest_ref, u0_ref, yc_ref,
             u_ref, ubuf, carry):
        @pl.when(pl.program_id(0) == 0)
        def _():
            carry[...] = jnp.zeros_like(carry)

        for part in range(tm // tp):
            rows = slice(part * tp, (part + 1) * tp)
            xv = x_ref[rows, :]
            r = lax.rsqrt(jnp.mean(xv * xv, axis=-1, keepdims=True) + EPS)
            h = (xv * r * g_ref[...]).astype(BF16)
            h_ref[rows, :] = h
            qkv_ref[rows, :] = (_dot(h, w_ref[:, :QKV_W]) + b_ref[:, :QKV_W]).astype(BF16)
            rest = (_dot(h, w_ref[:, QKV_W:]) + b_ref[:, QKV_W:]).astype(BF16)
            rest_ref[rows, :] = rest
            u0 = rest[:, :C].astype(F32) * _sigmoid(rest[:, C:2 * C].astype(F32))
            _conv_rows_fwd(u0, carry, ubuf, cw_ref, cb_ref, lg_ref, lb_ref, u0_ref.at[rows, :], yc_ref.at[rows, :],
                           u_ref.at[rows, :])

    return _pallas(
        body, name="inproj_fwd", grid=(T // tm,),
        in_specs=[_row_spec(tm, D), _full_spec((1, D)), _weight_spec((D, IN_W)), _full_spec((1, IN_W)),
                  _full_spec((CONV_K, C)), _full_spec((1, C)), _full_spec((1, C)), _full_spec((1, C))],
        out_specs=[_row_spec(tm, D), _row_spec(tm, QKV_W), _row_spec(tm, rest_w), _row_spec(tm, C),
                   _row_spec(tm, C), _row_spec(tm, C)],
        out_shape=[jax.ShapeDtypeStruct((T, D), BF16), jax.ShapeDtypeStruct((T, QKV_W), BF16),
                   jax.ShapeDtypeStruct((T, rest_w), BF16), jax.ShapeDtypeStruct((T, C), F32),
                   jax.ShapeDtypeStruct((T, C), F32), jax.ShapeDtypeStruct((T, C), BF16)],
        scratch_shapes=[pltpu.VMEM((SUBLANES, tp + HALO, C), F32), pltpu.VMEM((HALO, C), F32)],
        args=(x, g, w, b, cw, cb, lg, lb), sem=("arbitrary",), comm=comm)


def _fold_masks(first):
    shape = (BLOCK, GROUP * BLOCK)
    key = lax.broadcasted_iota(jnp.int32, shape, 0)
    qry = lax.broadcasted_iota(jnp.int32, shape, 1) & (BLOCK - 1)
    upper = key > qry
    dist = jnp.where(upper, qry + BLOCK - key, qry - key)
    keep = key <= qry + jnp.where(first, 0, BLOCK)
    return upper, dist.astype(F32), keep


def _fold(band, upper):
    return jnp.where(upper, band[:BLOCK, :], band[BLOCK:, :])


def _unfold(folded, upper):
    return jnp.concatenate([jnp.where(upper, folded, 0.0), jnp.where(upper, 0.0, folded)], axis=0)


def _head_row(values):
    return jnp.concatenate([jnp.full((1, BLOCK), v, F32) for v in values], axis=1)


def _head_consts(sink_ref, distf):
    bias, sink = [], []
    for kh in range(N_KV):
        heads = range(kh * GROUP, (kh + 1) * GROUP)
        bias.append(_head_row([2.0 ** (-8.0 * (h + 1) / N_Q) for h in heads]) * distf)
        sink.append(_head_row([sink_ref[h] for h in heads]))
    return bias, sink


def _heads_out(t):
    stacked = jnp.concatenate([t[:, g * BLOCK:(g + 1) * BLOCK] for g in range(GROUP)], axis=0)
    return stacked.T


def _fill_kv(kv_buf, cur_ref, prev_ref):
    scale = 1.0 / math.sqrt(HEAD_DIM)
    assert math.frexp(scale)[0] == 0.5
    for r0, ref in ((0, prev_ref), (BLOCK, cur_ref)):
        rows = ref.shape[0]
        kv_buf[r0:r0 + rows, :KV_W] = ref[:, ATTN_W:ATTN_W + KV_W] * scale
        kv_buf[r0:r0 + rows, KV_W:] = ref[:, ATTN_W + KV_W:]


def _group_rows(x, kh):
    return jnp.concatenate([x[:, h * HEAD_DIM:(h + 1) * HEAD_DIM] for h in range(kh * GROUP, (kh + 1) * GROUP)],
                           axis=0)


def _attn_probs(scores, bias, sink, upper, keep):
    s = _fold(scores, upper) - bias
    if keep is not None:
        s = jnp.where(keep, s, NEG)
    m = jnp.maximum(jnp.max(s, axis=0, keepdims=True), sink)
    p = jnp.exp(s - m)
    e = jnp.exp(sink - m)
    inv = 1.0 / (jnp.sum(p, axis=0, keepdims=True) + e)
    return p * inv, e * inv


def _attn_fwd(qkv, sinks):
    T = qkv.shape[0]
    tq = _tile(T, 1024)
    nblk = tq // BLOCK

    def body(sink_ref, cur_ref, prev_ref, o_ref, kv_buf):
        _fill_kv(kv_buf, cur_ref, prev_ref)
        upper, distf, keep = _fold_masks(pl.program_id(0) == 0)
        bias, sink = _head_consts(sink_ref, distf)
        for j in range(nblk):
            band = kv_buf[j * BLOCK:(j + 2) * BLOCK, :]
            q = cur_ref[j * BLOCK:(j + 1) * BLOCK, :ATTN_W]
            outs = []
            for kh in range(N_KV):
                k = band[:, kh * HEAD_DIM:(kh + 1) * HEAD_DIM]
                v = band[:, KV_W + kh * HEAD_DIM:KV_W + (kh + 1) * HEAD_DIM]
                p, _ = _attn_probs(_dot_nt(k, _group_rows(q, kh)), bias[kh], sink[kh], upper,
                                   keep if j == 0 else None)
                outs.append(_heads_out(_dot_tn(v, _unfold(p, upper).astype(BF16))))
            o_ref[j * BLOCK:(j + 1) * BLOCK, :] = jnp.concatenate(outs, axis=1).astype(BF16)

    return pl.pallas_call(
        body, name="attn_fwd", grid=(T // tq,),
        in_specs=[pl.BlockSpec(memory_space=pltpu.SMEM),
                  _row_spec(tq, QKV_W),
                  pl.BlockSpec((BLOCK, QKV_W), lambda i: (jnp.maximum(i * nblk - 1, 0), 0))],
        out_specs=_row_spec(tq, ATTN_W),
        out_shape=jax.ShapeDtypeStruct((T, ATTN_W), BF16),
        scratch_shapes=[pltpu.VMEM((tq + BLOCK, 2 * KV_W), BF16)],
        compiler_params=_params("parallel"),
    )(sinks, qkv, qkv)


def _shifted_copies(buf):
    n = buf.shape[1] - SUBLANES
    for s in range(1, SUBLANES):
        buf[s, 0:n, :] = buf[0, s:s + n, :]


def _shifted_rows(buf, start, rows):
    s = start % SUBLANES
    return buf[s, start - s:start - s + rows, :]


def _conv_rows_fwd(u0, carry, ubuf, w_ref, cb_ref, g_ref, b_ref, u0_ref, yc_ref, u_ref):
    n, C = u0.shape
    R = _tile(n, CONV_ROWS)
    ubuf[0, 0:HALO, :] = carry[...]
    ubuf[0, HALO:, :] = u0
    carry[...] = u0[n - HALO:, :]
    u0_ref[...] = u0
    _shifted_copies(ubuf)
    off = HALO - (CONV_K - 1)
    for c in range(n // R):
        acc = jnp.broadcast_to(cb_ref[...], (R, C))
        for j in range(CONV_K):
            acc = acc + w_ref[j:j + 1, :] * _shifted_rows(ubuf, c * R + off + j, R)
        yc_ref[c * R:(c + 1) * R, :] = acc
        xc = acc - jnp.mean(acc, axis=-1, keepdims=True)
        ln = xc * lax.rsqrt(jnp.mean(xc * xc, axis=-1, keepdims=True) + EPS) * g_ref[...] + b_ref[...]
        u_ref[c * R:(c + 1) * R, :] = (ln * _sigmoid(ln)).astype(BF16)


def _merge_fwd(attn, u, rest, x, wa, wc, bc, wo):
    T, D = x.shape
    tm = _tile(T, 512)
    gcol = 2 * CONV_C // D

    def body(attn_ref, u_ref, ga_ref, gc_ref, x_ref, wa_ref, wc_ref, bc_ref, wo_ref, m_ref, x1_ref):
        bra = _dot(attn_ref[...], wa_ref[...])
        brc = _dot(u_ref[...], wc_ref[...]) + bc_ref[...]
        mb = (_sigmoid(ga_ref[...].astype(F32)) * bra + _sigmoid(gc_ref[...].astype(F32)) * brc).astype(BF16)
        m_ref[...] = mb
        x1_ref[...] = x_ref[...] + _dot(mb, wo_ref[...])

    return pl.pallas_call(
        body, name="merge_fwd", grid=(T // tm,),
        in_specs=[_row_spec(tm, ATTN_W), _row_spec(tm, CONV_C), _row_spec(tm, D, gcol), _row_spec(tm, D, gcol + 1),
                  _row_spec(tm, D), _weight_spec((ATTN_W, D)), _weight_spec((CONV_C, D)), _full_spec((1, D)),
                  _weight_spec((D, D))],
        out_specs=[_row_spec(tm, D), _row_spec(tm, D)],
        out_shape=[jax.ShapeDtypeStruct((T, D), BF16), jax.ShapeDtypeStruct((T, D), F32)],
        compiler_params=_params("parallel"),
    )(attn, u, rest, rest, x, wa, wc, bc, wo)


def _mlp_tile(x_ref, g_ref, w1_ref, w2_ref, h_ref, z_ref):
    fc = _tile(D_FF, FF_CHUNK)
    xv = x_ref[...]
    r = lax.rsqrt(jnp.mean(xv * xv, axis=-1, keepdims=True) + EPS)
    h = (xv * r * g_ref[...]).astype(BF16)
    h_ref[...] = h
    acc = xv
    for c in range(D_FF // fc):
        cols = slice(c * fc, (c + 1) * fc)
        z = _dot(h, w1_ref[:, cols])
        z_ref[:, cols] = z.astype(BF16)
        acc = acc + _dot(jnp.square(jnp.maximum(z, 0.0)).astype(BF16), w2_ref[cols, :])
    return acc


def _mlp_fwd(x1, g, w1, w2, comm=None):
    T, D = x1.shape
    tm = _tile(T, 512)

    def body(x_ref, g_ref, w1_ref, w2_ref, h_ref, z_ref, o_ref):
        o_ref[...] = _mlp_tile(x_ref, g_ref, w1_ref, w2_ref, h_ref, z_ref)

    return _pallas(
        body, name="mlp_fwd", grid=(T // tm,),
        in_specs=[_row_spec(tm, D), _full_spec((1, D)), _weight_spec((D, D_FF)), _weight_spec((D_FF, D))],
        out_specs=[_row_spec(tm, D), _row_spec(tm, D_FF), _row_spec(tm, D)],
        out_shape=[jax.ShapeDtypeStruct((T, D), BF16), jax.ShapeDtypeStruct((T, D_FF), BF16),
                   jax.ShapeDtypeStruct((T, D), F32)],
        args=(x1, g, w1, w2), sem=("parallel",), comm=comm)


def _mlp_fwd_loss(x1, g, w1, w2, gf, target):
    T, D = x1.shape
    tm = _tile(T, 512)

    def body(x_ref, g_ref, w1_ref, w2_ref, gf_ref, t_ref, h_ref, z_ref, l_ref, dx_ref, dg_ref):
        @pl.when(pl.program_id(0) == 0)
        def _():
            l_ref[...] = jnp.zeros_like(l_ref)
            dg_ref[...] = jnp.zeros_like(dg_ref)

        x2 = _mlp_tile(x_ref, g_ref, w1_ref, w2_ref, h_ref, z_ref)
        r = lax.rsqrt(jnp.mean(x2 * x2, axis=-1, keepdims=True) + EPS)
        e = x2 * r * gf_ref[...] - t_ref[...]
        l_ref[...] += _rows8(e * e) * (0.5 / D)
        dx, dg = _rms_bwd(x2, gf_ref[...], e * (1.0 / D))
        dx_ref[...] = dx
        dg_ref[...] += _rows8(dg)

    return pl.pallas_call(
        body, name="mlp_fwd_loss", grid=(T // tm,),
        in_specs=[_row_spec(tm, D), _full_spec((1, D)), _weight_spec((D, D_FF)), _weight_spec((D_FF, D)),
                  _full_spec((1, D)), _row_spec(tm, D)],
        out_specs=[_row_spec(tm, D), _row_spec(tm, D_FF), _full_spec((SUBLANES, D)), _row_spec(tm, D),
                   _full_spec((SUBLANES, D))],
        out_shape=[jax.ShapeDtypeStruct((T, D), BF16), jax.ShapeDtypeStruct((T, D_FF), BF16),
                   jax.ShapeDtypeStruct((SUBLANES, D), F32), jax.ShapeDtypeStruct((T, D), F32),
                   jax.ShapeDtypeStruct((SUBLANES, D), F32)],
        compiler_params=_params("arbitrary"),
    )(x1, g, w1, w2, gf, target)


def _mlp_bwd(dx2, x1, z, g, w1, w2, comm=None):
    T, D = x1.shape
    tm = _tile(T, 512)
    fc = _tile(D_FF, FF_CHUNK)

    def body(dx2_ref, x_ref, z_ref, g_ref, w1_ref, w2_ref, dx1_ref, dz_ref, dg_ref):
        @pl.when(pl.program_id(0) == 0)
        def _():
            dg_ref[...] = jnp.zeros_like(dg_ref)

        dxo = dx2_ref[...]
        dxb = dxo.astype(BF16)
        dh = jnp.zeros((tm, D), F32)
        for c in range(D_FF // fc):
            cols = slice(c * fc, (c + 1) * fc)
            da = _dot_nt(dxb, w2_ref[cols, :])
            dz = (da * (2.0 * jnp.maximum(z_ref[:, cols].astype(F32), 0.0))).astype(BF16)
            dz_ref[:, cols] = dz
            dh = dh + _dot_nt(dz, w1_ref[:, cols])
        dx, dg = _rms_bwd(x_ref[...], g_ref[...], dh)
        dx1_ref[...] = dxo + dx
        dg_ref[...] += _rows8(dg)

    return _pallas(
        body, name="mlp_bwd", grid=(T // tm,),
        in_specs=[_row_spec(tm, D), _row_spec(tm, D), _row_spec(tm, D_FF), _full_spec((1, D)),
                  _weight_spec((D, D_FF)), _weight_spec((D_FF, D))],
        out_specs=[_row_spec(tm, D), _row_spec(tm, D_FF), _full_spec((SUBLANES, D))],
        out_shape=[jax.ShapeDtypeStruct((T, D), F32), jax.ShapeDtypeStruct((T, D_FF), BF16),
                   jax.ShapeDtypeStruct((SUBLANES, D), F32)],
        args=(dx2, x1, z, g, w1, w2), sem=("arbitrary",), comm=comm)


def _tn_blocks(a, b, name, col_sharded, relu_sq=False):
    T, M = a.shape
    N = b.shape[1]
    tk = _tile(T, 1024)
    tm = _tile(M, 512 if col_sharded else 1024)
    nb = N // N_DEV
    last = T // tk - 1

    def body(a_ref, b_ref, o_ref, acc_ref):
        k = pl.program_id(1)

        @pl.when(k == 0)
        def _():
            acc_ref[...] = jnp.zeros_like(acc_ref)

        av = a_ref[...]
        if relu_sq:
            av = jnp.square(jnp.maximum(av, 0.0))
        acc_ref[...] += _dot_tn(av.astype(BF16), b_ref[...].astype(BF16))

        @pl.when(k == last)
        def _():
            if col_sharded:
                for d in range(N_DEV):
                    o_ref[d] = acc_ref[:, d * nb:(d + 1) * nb].astype(BF16)
            else:
                o_ref[...] = acc_ref[...].astype(BF16)

    if col_sharded:
        out_spec = pl.BlockSpec((N_DEV, tm, nb), lambda i, k: (0, i, 0))
        out_shape = jax.ShapeDtypeStruct((N_DEV, M, nb), BF16)
    else:
        out_spec = pl.BlockSpec((tm, N), lambda i, k: (i, 0))
        out_shape = jax.ShapeDtypeStruct((M, N), BF16)
    out = pl.pallas_call(
        body, name=name, grid=(M // tm, T // tk),
        in_specs=[pl.BlockSpec((tk, tm), lambda i, k: (k, i)), pl.BlockSpec((tk, N), lambda i, k: (k, 0))],
        out_specs=out_spec, out_shape=out_shape,
        scratch_shapes=[pltpu.VMEM((tm, N), F32)],
        compiler_params=_params("parallel", "arbitrary"),
    )(a, b)
    return out if col_sharded else out.reshape(N_DEV, M // N_DEV, N)


def _merge_bwd(dx1, attn, u, rest, yc, wa, wc, bc, wo, lg, lb, comm=None):
    T, D = dx1.shape
    C = CONV_C
    tm = _tile(T, 512)
    gcol = 2 * CONV_C // D

    def body(dx_ref, attn_ref, u_ref, ga_ref, gc_ref, yc_ref, wa_ref, wc_ref, bc_ref, wo_ref, lg_ref, lb_ref,
             dattn_ref, dyc_ref, dga_ref, dgc_ref, dbra_ref, dbrc_ref, dbc_ref, dlg_ref, dlb_ref):
        @pl.when(pl.program_id(0) == 0)
        def _():
            dbc_ref[...] = jnp.zeros_like(dbc_ref)
            dlg_ref[...] = jnp.zeros_like(dlg_ref)
            dlb_ref[...] = jnp.zeros_like(dlb_ref)

        dm = _dot_nt(dx_ref[...].astype(BF16), wo_ref[...])
        bra = _dot(attn_ref[...], wa_ref[...])
        brc = _dot(u_ref[...], wc_ref[...]) + bc_ref[...]
        sa = _sigmoid(ga_ref[...].astype(F32))
        sc = _sigmoid(gc_ref[...].astype(F32))
        dbra = dm * sa
        dbrc = dm * sc
        dga_ref[...] = (dm * bra * sa * (1.0 - sa)).astype(BF16)
        dgc_ref[...] = (dm * brc * sc * (1.0 - sc)).astype(BF16)
        dbra_b = dbra.astype(BF16)
        dbrc_b = dbrc.astype(BF16)
        dbra_ref[...] = dbra_b
        dbrc_ref[...] = dbrc_b
        dbc_ref[...] += _rows8(dbrc)
        dattn_ref[...] = _dot_nt(dbra_b, wa_ref[...]).astype(BF16)
        dyc, dlg, dlb = _swish_norm_bwd(_dot_nt(dbrc_b, wc_ref[...]), yc_ref[...], lg_ref[...], lb_ref[...])
        dyc_ref[...] = dyc
        dlg_ref[...] += _rows8(dlg)
        dlb_ref[...] += _rows8(dlb)

    return _pallas(
        body, name="merge_bwd", grid=(T // tm,),
        in_specs=[_row_spec(tm, D), _row_spec(tm, ATTN_W), _row_spec(tm, C), _row_spec(tm, D, gcol),
                  _row_spec(tm, D, gcol + 1), _row_spec(tm, C), _weight_spec((ATTN_W, D)), _weight_spec((C, D)),
                  _full_spec((1, D)), _weight_spec((D, D)), _full_spec((1, C)), _full_spec((1, C))],
        out_specs=[_row_spec(tm, ATTN_W), _row_spec(tm, C), _row_spec(tm, D), _row_spec(tm, D),
                   _row_spec(tm, D), _row_spec(tm, D), _full_spec((SUBLANES, D)), _full_spec((SUBLANES, C)),
                   _full_spec((SUBLANES, C))],
        out_shape=[jax.ShapeDtypeStruct((T, ATTN_W), BF16), jax.ShapeDtypeStruct((T, C), F32),
                   jax.ShapeDtypeStruct((T, D), BF16), jax.ShapeDtypeStruct((T, D), BF16),
                   jax.ShapeDtypeStruct((T, D), BF16), jax.ShapeDtypeStruct((T, D), BF16),
                   jax.ShapeDtypeStruct((SUBLANES, D), F32), jax.ShapeDtypeStruct((SUBLANES, C), F32),
                   jax.ShapeDtypeStruct((SUBLANES, C), F32)],
        args=(dx1, attn, u, rest, rest, yc, wa, wc, bc, wo, lg, lb), sem=("arbitrary",), comm=comm)


def _swish_norm_bwd(du, yv, g, b):
    xc = yv - jnp.mean(yv, axis=-1, keepdims=True)
    rstd = lax.rsqrt(jnp.mean(xc * xc, axis=-1, keepdims=True) + EPS)
    xn = xc * rstd
    ln = xn * g + b
    sg = _sigmoid(ln)
    dln = du * sg * (1.0 + ln * (1.0 - sg))
    dxn = dln * g
    dyc = rstd * (dxn - jnp.mean(dxn, axis=-1, keepdims=True) - xn * jnp.mean(dxn * xn, axis=-1, keepdims=True))
    return dyc, dln * xn, dln


def _conv_taps_bwd(first, last, dy_ref, dyn_ref, u0_ref, u0p_ref, glu_ref, w_ref, dglu_ref, dw_ref, db_ref,
                   dbuf, ubuf):
    tm, C = dy_ref.shape
    R = _tile(tm, CONV_ROWS)
    dbuf[0, 0:tm, :] = dy_ref[...]
    dbuf[0, tm:, :] = jnp.where(last, 0.0, dyn_ref[...])
    ubuf[0, 0:HALO, :] = jnp.where(first, 0.0, u0p_ref[...])
    ubuf[0, HALO:, :] = u0_ref[...]
    _shifted_copies(dbuf)
    _shifted_copies(ubuf)
    off = HALO - (CONV_K - 1)
    for c in range(tm // R):
        rows = slice(c * R, (c + 1) * R)
        dy = dbuf[0, rows, :]
        acc = jnp.zeros((R, C), F32)
        for j in range(CONV_K):
            acc = acc + w_ref[j:j + 1, :] * _shifted_rows(dbuf, c * R + CONV_K - 1 - j, R)
            dw_ref[j * SUBLANES:(j + 1) * SUBLANES, :] += _rows8(dy * _shifted_rows(ubuf, c * R + off + j, R))
        db_ref[...] += _rows8(dy)
        a = glu_ref[rows, :C].astype(F32)
        sb = _sigmoid(glu_ref[rows, C:].astype(F32))
        dglu_ref[rows, :C] = (acc * sb).astype(BF16)
        dglu_ref[rows, C:] = (acc * a * sb * (1.0 - sb)).astype(BF16)


def _attn_bwd(qkv, dattn, sinks, comm=None):
    T = qkv.shape[0]
    tq = _tile(T, 1024)
    nblk = tq // BLOCK
    scale = 1.0 / math.sqrt(HEAD_DIM)

    def body(sink_ref, cur_ref, prev_ref, do_ref, dq_ref, hi_ref, lo_ref, ds_ref, kv_buf):
        i = pl.program_id(0)

        @pl.when(i == 0)
        def _():
            ds_ref[...] = jnp.zeros_like(ds_ref)

        _fill_kv(kv_buf, cur_ref, prev_ref)
        upper, distf, keep = _fold_masks(i == 0)
        bias, sink = _head_consts(sink_ref, distf)
        for j in range(nblk):
            rows = slice(j * BLOCK, (j + 1) * BLOCK)
            band = kv_buf[j * BLOCK:(j + 2) * BLOCK, :]
            q = cur_ref[rows, :ATTN_W]
            do = do_ref[rows, :]
            dqs, dks, dvs = [], [], []
            for kh in range(N_KV):
                k = band[:, kh * HEAD_DIM:(kh + 1) * HEAD_DIM]
                v = band[:, KV_W + kh * HEAD_DIM:KV_W + (kh + 1) * HEAD_DIM]
                qg = _group_rows(q, kh)
                dog = _group_rows(do, kh)
                p, psink = _attn_probs(_dot_nt(k, qg), bias[kh], sink[kh], upper, keep if j == 0 else None)
                pdp = p * _fold(_dot_nt(v, dog), upper)
                delta = jnp.sum(pdp, axis=0, keepdims=True)
                lanes = slice(kh * GROUP * BLOCK, (kh + 1) * GROUP * BLOCK)
                ds_ref[0:1, lanes] += psink * delta
                dsb = _unfold(pdp - p * delta, upper).astype(BF16)
                dqs.append(_heads_out(_dot_tn(k, dsb)))
                dks.append(_dot(dsb, qg) * scale)
                dvs.append(_dot(_unfold(p, upper).astype(BF16), dog))
            dq_ref[rows, :] = jnp.concatenate(dqs, axis=1).astype(BF16)
            dkv = jnp.concatenate(dks + dvs, axis=1)
            lo_ref[rows, :] = dkv[:BLOCK, :]
            hi_ref[rows, :] = dkv[BLOCK:, :]

    return _pallas(
        body, name="attn_bwd", grid=(T // tq,),
        in_specs=[pl.BlockSpec(memory_space=pltpu.SMEM),
                  _row_spec(tq, QKV_W),
                  pl.BlockSpec((BLOCK, QKV_W), lambda i: (jnp.maximum(i * nblk - 1, 0), 0)),
                  _row_spec(tq, ATTN_W)],
        out_specs=[_row_spec(tq, ATTN_W), _row_spec(tq, 2 * KV_W), _row_spec(tq, 2 * KV_W),
                   _full_spec((SUBLANES, N_Q * BLOCK))],
        out_shape=[jax.ShapeDtypeStruct((T, ATTN_W), BF16), jax.ShapeDtypeStruct((T, 2 * KV_W), F32),
                   jax.ShapeDtypeStruct((T, 2 * KV_W), F32), jax.ShapeDtypeStruct((SUBLANES, N_Q * BLOCK), F32)],
        scratch_shapes=[pltpu.VMEM((tq + BLOCK, 2 * KV_W), BF16)],
        args=(sinks, qkv, qkv, dattn), sem=("arbitrary",), comm=comm)


def _inproj_bwd(dq, hi, lo, dyc, u0, rest, cw, dga, dgc, x, g, w, dx1, comm=None):
    T, D = x.shape
    C = CONV_C
    tm = _tile(T, 256)
    per = tm // BLOCK
    per_halo = tm // HALO
    nt = T // tm
    kv2 = 2 * KV_W
    glu0, gate0 = QKV_W, QKV_W + 2 * C

    def body(dq_ref, hi_ref, lo_ref, lon_ref, dy_ref, dyn_ref, u0_ref, u0p_ref, glu_ref, cw_ref, dga_ref, dgc_ref,
             x_ref, g_ref, w_ref, dx1_ref, dp_ref, dx_ref, dg_ref, dbias_ref, dcw_ref, dcb_ref, dbuf, ubuf):
        i = pl.program_id(0)

        @pl.when(i == 0)
        def _():
            for ref in (dg_ref, dbias_ref, dcw_ref, dcb_ref):
                ref[...] = jnp.zeros_like(ref)

        def part(cols):
            dp = dp_ref[:, cols]
            dbias_ref[:, cols] += _rows8(dp.astype(F32))
            return _dot_nt(dp, w_ref[:, cols])

        dp_ref[:, :ATTN_W] = dq_ref[...]
        lo_next = jnp.where(i < nt - 1, lon_ref[...], 0.0)
        lo_shift = jnp.concatenate([lo_ref[BLOCK:, :], lo_next], axis=0) if tm > BLOCK else lo_next
        dp_ref[:, ATTN_W:QKV_W] = (hi_ref[...] + lo_shift).astype(BF16)
        dp_ref[:, gate0:gate0 + D] = dga_ref[...]
        dp_ref[:, gate0 + D:] = dgc_ref[...]
        _conv_taps_bwd(i == 0, i == nt - 1, dy_ref, dyn_ref, u0_ref, u0p_ref, glu_ref, cw_ref,
                       dp_ref.at[:, glu0:gate0], dcw_ref, dcb_ref, dbuf, ubuf)
        dh = part(slice(0, QKV_W)) + part(slice(gate0, IN_W)) + part(slice(glu0, gate0))
        dx, dg = _rms_bwd(x_ref[...], g_ref[...], dh)
        dx_ref[...] = dx1_ref[...] + dx
        dg_ref[...] += _rows8(dg)

    return _pallas(
        body, name="inproj_bwd", grid=(nt,),
        in_specs=[_row_spec(tm, ATTN_W), _row_spec(tm, kv2), _row_spec(tm, kv2),
                  pl.BlockSpec((BLOCK, kv2), lambda i: (jnp.minimum((i + 1) * per, T // BLOCK - 1), 0)),
                  _row_spec(tm, C),
                  pl.BlockSpec((HALO, C), lambda i: (jnp.minimum((i + 1) * per_halo, T // HALO - 1), 0)),
                  _row_spec(tm, C),
                  pl.BlockSpec((HALO, C), lambda i: (jnp.maximum(i * per_halo - 1, 0), 0)),
                  _row_spec(tm, 2 * C), _full_spec((CONV_K, C)),
                  _row_spec(tm, D), _row_spec(tm, D), _row_spec(tm, D), _full_spec((1, D)),
                  _weight_spec((D, IN_W)), _row_spec(tm, D)],
        out_specs=[_row_spec(tm, IN_W), _row_spec(tm, D), _full_spec((SUBLANES, D)), _full_spec((SUBLANES, IN_W)),
                   _full_spec((CONV_K * SUBLANES, C)), _full_spec((SUBLANES, C))],
        out_shape=[jax.ShapeDtypeStruct((T, IN_W), BF16), jax.ShapeDtypeStruct((T, D), F32),
                   jax.ShapeDtypeStruct((SUBLANES, D), F32), jax.ShapeDtypeStruct((SUBLANES, IN_W), F32),
                   jax.ShapeDtypeStruct((CONV_K * SUBLANES, C), F32), jax.ShapeDtypeStruct((SUBLANES, C), F32)],
        scratch_shapes=[pltpu.VMEM((SUBLANES, tm + HALO, C), F32), pltpu.VMEM((SUBLANES, tm + HALO, C), F32)],
        args=(dq, hi, lo, lo, dyc, dyc, u0, u0, rest, cw, dga, dgc, x, g, w, dx1), sem=("arbitrary",), comm=comm)


def _adamw_math(g, w, m, v):
    c1 = 1.0 / (1.0 - ADAM_B1 ** ADAM_STEP)
    c2 = 1.0 / (1.0 - ADAM_B2 ** ADAM_STEP)
    mn = ADAM_B1 * m + (1.0 - ADAM_B1) * g
    vn = ADAM_B2 * v + (1.0 - ADAM_B2) * (g * g)
    return -ADAM_LR * ((mn * c1) / (jnp.sqrt(vn * c2) + ADAM_EPS) + ADAM_WD * w), mn, vn


def _adamw_sharded(parts, w, m, v, name):
    depth, a, b = w.shape
    tr = _tile(a, 256) if a % SUBLANES == 0 else a
    nr = a // tr

    def body(*refs):
        p_refs, (w_ref, m_ref, v_ref, g_ref, d_ref, mo_ref, vo_ref) = refs[:depth], refs[depth:]
        layer = pl.program_id(0)
        for l in range(depth):
            @pl.when(layer == l)
            def _(l=l):
                g = p_refs[l][0].astype(F32)
                for s in range(1, N_DEV):
                    g = g + p_refs[l][s].astype(F32)
                g_ref[...] = g
                d_ref[...], mo_ref[...], vo_ref[...] = _adamw_math(g, w_ref[...], m_ref[...], v_ref[...])

    def part_spec(l):
        return pl.BlockSpec((N_DEV, tr, b),
                            lambda k, i: (0, jnp.where(k == l, i, jnp.where(k < l, 0, nr - 1)), 0))

    spec = pl.BlockSpec((None, tr, b), lambda k, i: (k, i, 0))
    out = jax.ShapeDtypeStruct((depth, a, b), F32)
    return pl.pallas_call(
        body, name=name, grid=(depth, nr),
        in_specs=[part_spec(l) for l in range(depth)] + [spec] * 3,
        out_specs=[spec] * 4, out_shape=[out] * 4,
        compiler_params=_params("arbitrary", "arbitrary"),
    )(*parts, w, m, v)


def _adamw_small(parts, w, m, v):
    R, N = w.shape

    def body(p_ref, w_ref, m_ref, v_ref, g_ref, d_ref, mo_ref, vo_ref):
        g = p_ref[0]
        for s in range(1, N_DEV):
            g = g + p_ref[s]
        g_ref[...] = g
        d_ref[...], mo_ref[...], vo_ref[...] = _adamw_math(g, w_ref[...], m_ref[...], v_ref[...])

    out = jax.ShapeDtypeStruct((R, N), F32)
    return pl.pallas_call(
        body, name="adamw_small", grid=(1,),
        in_specs=[_full_spec((N_DEV, R, N))] + [_full_spec((R, N))] * 3,
        out_specs=[_full_spec((R, N))] * 4, out_shape=[out] * 4,
        compiler_params=_params("arbitrary"),
    )(parts, w, m, v)


_SHARDED = ("w_in", "conv_w", "w_attn_proj", "w_conv_proj", "w_out", "w_mlp1", "w_mlp2")
_ROW_SHARDED = ("w_out", "w_mlp2")
_FIRST = ("w_in", "conv_w")
_REST = tuple(n for n in _SHARDED if n not in _FIRST)
_SMALL = ("mix_norm_g", "b_in", "sinks", "conv_b", "conv_ln_g", "conv_ln_b", "b_conv_proj", "mlp_norm_g",
          "final_norm_g")
_ORDER = ("mix_norm_g", "w_in", "b_in", "sinks", "conv_w", "conv_b", "conv_ln_g", "conv_ln_b", "w_attn_proj",
          "w_conv_proj", "b_conv_proj", "w_out", "mlp_norm_g", "w_mlp1", "w_mlp2", "final_norm_g")
_PACK = 1024


def _full_weights(names, gathered):
    cols = [i for i, n in enumerate(names) if n not in _ROW_SHARDED]

    def body(*refs):
        for src, dst in zip(refs[:len(cols)], refs[len(cols):]):
            b = src.shape[2]
            for d in range(N_DEV):
                dst[:, d * b:(d + 1) * b] = src[d]

    vmem = pl.BlockSpec(memory_space=pltpu.VMEM)
    placed = pl.pallas_call(
        body, name="place_" + names[cols[0]], in_specs=[vmem] * len(cols), out_specs=[vmem] * len(cols),
        out_shape=[jax.ShapeDtypeStruct((gathered[i].shape[1], N_DEV * gathered[i].shape[2]), gathered[i].dtype)
                   for i in cols],
        compiler_params=pltpu.CompilerParams(vmem_limit_bytes=VMEM_LIMIT),
    )(*[gathered[i] for i in cols])
    full = {names[i]: a for i, a in zip(cols, placed)}
    for n, a in zip(names, gathered):
        if n in _ROW_SHARDED:
            full[n] = a.reshape(N_DEV * a.shape[1], a.shape[2])
    return full


def _pack(arrs):
    flat = []
    for a in arrs:
        a = a.reshape(-1)
        flat.append(jnp.pad(a, (0, -a.size % _PACK)))
    return jnp.concatenate(flat).reshape(-1, BLOCK)


def _unpack(packed, shapes):
    flat = packed.reshape(-1)
    out, off = [], 0
    for s in shapes:
        n = math.prod(s)
        out.append(flat[off:off + n].reshape(s))
        off += n + (-n % _PACK)
    return out


def _layer_fwd(x, lw, own_rest=None, comm=None, loss_head=None):
    (h, qkv, rest, u0, yc, u), got = _inproj_fwd(x, lw["mix_norm_g"], lw["w_in"], lw["b_in"], lw["conv_w"],
                                                 lw["conv_b"], lw["conv_ln_g"], lw["conv_ln_b"], own_rest)
    if got is not None:
        lw.update(_full_weights(_REST, got))
    attn = _attn_fwd(qkv, lw["sinks"])
    merged, x1 = _merge_fwd(attn, u, rest, x, lw["w_attn_proj"], lw["w_conv_proj"], lw["b_conv_proj"], lw["w_out"])
    if loss_head is None:
        (h2, z, out), gathered = _mlp_fwd(x1, lw["mlp_norm_g"], lw["w_mlp1"], lw["w_mlp2"], comm)
    else:
        assert comm is None
        h2, z, *out = _mlp_fwd_loss(x1, lw["mlp_norm_g"], lw["w_mlp1"], lw["w_mlp2"], *loss_head)
        gathered = None
    saved = dict(x=x, h=h, qkv=qkv, rest=rest, attn=attn, u0=u0, yc=yc, u=u, merged=merged, x1=x1, h2=h2, z=z)
    return out, saved, gathered


_EARLY = ("w_mlp1", "w_mlp2")
_MIDDLE = ("w_out", "w_attn_proj", "w_conv_proj")
_LATE = ("w_in", "conv_w")


def _layer_bwd(dx2, lw, s, late_blocks, dx_is_result):
    g, recv = {}, {}
    late = None if late_blocks is None else _Exchange(late_blocks)
    (dx1, dz, dg2), late_recv = _mlp_bwd(dx2, s["x1"], s["z"], lw["mlp_norm_g"], lw["w_mlp1"], lw["w_mlp2"], late)
    g["mlp_norm_g"] = jnp.sum(dg2, axis=0)
    early = [_tn_blocks(s["h2"], dz, "dw_mlp1", True), _tn_blocks(s["z"], dx2, "dw_mlp2", False, relu_sq=True)]
    (dattn, dyc, dga, dgc, dbra, dbrc, dbc, dlg, dlb), behind_merge = _merge_bwd(
        dx1, s["attn"], s["u"], s["rest"], s["yc"], lw["w_attn_proj"], lw["w_conv_proj"], lw["b_conv_proj"],
        lw["w_out"], lw["conv_ln_g"], lw["conv_ln_b"], _Exchange(early[:1]) if dx_is_result else None)
    g["b_conv_proj"] = jnp.sum(dbc, axis=0)
    g["conv_ln_g"] = jnp.sum(dlg, axis=0)
    g["conv_ln_b"] = jnp.sum(dlb, axis=0)
    middle = [_tn_blocks(s["merged"], dx1, "dw_out", False), _tn_blocks(s["attn"], dbra, "dw_attn_proj", True),
              _tn_blocks(s["u"], dbrc, "dw_conv_proj", True)]
    (dq, hi, lo, dsk), behind_attn = _attn_bwd(s["qkv"], dattn, lw["sinks"],
                                               _Exchange(middle + (early[1:] if dx_is_result else [])))
    recv.update(zip(_MIDDLE, behind_attn))
    g["sinks"] = -jnp.sum(dsk[0].reshape(N_Q, BLOCK), axis=1)
    (dproj, dx, dg1, dbin, dcw, dcb), behind_inproj = _inproj_bwd(
        dq, hi, lo, dyc, s["u0"], s["rest"], lw["conv_w"], dga, dgc, s["x"], lw["mix_norm_g"], lw["w_in"], dx1,
        None if dx_is_result else _Exchange(early))
    recv.update(zip(_EARLY, behind_merge + behind_attn[len(middle):] if dx_is_result else behind_inproj))
    dconv_w = jnp.sum(dcw.reshape(CONV_K, SUBLANES, CONV_C), axis=1)
    g["conv_b"] = jnp.sum(dcb, axis=0)
    g["mix_norm_g"] = jnp.sum(dg1, axis=0)
    g["b_in"] = jnp.sum(dbin, axis=0)
    own_late = [_tn_blocks(s["h"], dproj, "dw_in", True),
                dconv_w.reshape(CONV_K, N_DEV, CONV_C // N_DEV).transpose(1, 0, 2)]
    return dx, g, recv, late_recv, own_late


def kernel(x, mix_norm_g, w_in, b_in, sinks, conv_w, conv_b, conv_ln_g, conv_ln_b, w_attn_proj, w_conv_proj, b_conv_proj, w_out, mlp_norm_g, w_mlp1, w_mlp2, final_norm_g, loss_target, m_mix_norm_g, m_w_in, m_b_in, m_sinks, m_conv_w, m_conv_b, m_conv_ln_g, m_conv_ln_b, m_w_attn_proj, m_w_conv_proj, m_b_conv_proj, m_w_out, m_mlp_norm_g, m_w_mlp1, m_w_mlp2, m_final_norm_g, v_mix_norm_g, v_w_in, v_b_in, v_sinks, v_conv_w, v_conv_b, v_conv_ln_g, v_conv_ln_b, v_w_attn_proj, v_w_conv_proj, v_b_conv_proj, v_w_out, v_mlp_norm_g, v_w_mlp1, v_w_mlp2, v_final_norm_g):
    w = dict(mix_norm_g=mix_norm_g, w_in=w_in, b_in=b_in, sinks=sinks, conv_w=conv_w, conv_b=conv_b,
             conv_ln_g=conv_ln_g, conv_ln_b=conv_ln_b, w_attn_proj=w_attn_proj, w_conv_proj=w_conv_proj,
             b_conv_proj=b_conv_proj, w_out=w_out, mlp_norm_g=mlp_norm_g, w_mlp1=w_mlp1, w_mlp2=w_mlp2,
             final_norm_g=final_norm_g)
    m = dict(mix_norm_g=m_mix_norm_g, w_in=m_w_in, b_in=m_b_in, sinks=m_sinks, conv_w=m_conv_w, conv_b=m_conv_b,
             conv_ln_g=m_conv_ln_g, conv_ln_b=m_conv_ln_b, w_attn_proj=m_w_attn_proj, w_conv_proj=m_w_conv_proj,
             b_conv_proj=m_b_conv_proj, w_out=m_w_out, mlp_norm_g=m_mlp_norm_g, w_mlp1=m_w_mlp1, w_mlp2=m_w_mlp2,
             final_norm_g=m_final_norm_g)
    v = dict(mix_norm_g=v_mix_norm_g, w_in=v_w_in, b_in=v_b_in, sinks=v_sinks, conv_w=v_conv_w, conv_b=v_conv_b,
             conv_ln_g=v_conv_ln_g, conv_ln_b=v_conv_ln_b, w_attn_proj=v_w_attn_proj, w_conv_proj=v_w_conv_proj,
             b_conv_proj=v_b_conv_proj, w_out=v_w_out, mlp_norm_g=v_mlp_norm_g, w_mlp1=v_w_mlp1, w_mlp2=v_w_mlp2,
             final_norm_g=v_final_norm_g)
    T = x.shape[1]
    xs = x.reshape(T, D_MODEL)
    target = loss_target.reshape(T, D_MODEL)

    def gather_of(l, names):
        return _Gather([w[n][l] if n == "conv_w" else w[n][l].astype(BF16) for n in names])

    def layer_weights(l, names, gathered):
        lw = _full_weights(names, gathered)
        for n in _SMALL:
            if n != "final_norm_g":
                lw[n] = w[n][l] if n == "sinks" else w[n][l].reshape(1, -1)
        return lw

    acts = xs
    saved, weights = [], []
    for l in range(DEPTH):
        last = l + 1 == DEPTH
        following = None if last else gather_of(l + 1, _SHARDED)
        loss_head = (final_norm_g.reshape(1, -1), target) if last else None
        if l == 0:
            lw = layer_weights(0, _FIRST, _run_comm(gather_of(0, _FIRST), "gather_first"))
            acts, s, gathered = _layer_fwd(acts, lw, gather_of(0, _REST), following, loss_head)
        else:
            lw = layer_weights(l, _SHARDED, gathered)
            acts, s, gathered = _layer_fwd(acts, lw, None, following, loss_head)
        weights.append(lw)
        saved.append(s)
    lterms, dx, dgf = acts
    grads, received = [None] * DEPTH, [None] * DEPTH
    late = None
    for l in reversed(range(DEPTH)):
        dx, grads[l], received[l], late_recv, late = _layer_bwd(dx, weights[l], saved[l], late, l == 0)
        if late_recv is not None:
            received[l + 1].update(zip(_LATE, late_recv))
    received[0].update(zip(_LATE, _run_comm(_Exchange(late), "scatter_late")))
    grad = {n: jnp.stack([grads[l][n] for l in range(DEPTH)]) for n in _SMALL if n != "final_norm_g"}
    grad["final_norm_g"] = jnp.sum(dgf, axis=0)

    small_shapes = [w[n].shape for n in _SMALL] + [(1,)]
    small = _pack([grad[n] for n in _SMALL] + [jnp.sum(lterms).reshape(1)])
    small_parts = _run_comm(_Gather([small]), "gather_small")[0]

    out_g, out_d, out_m, out_v = {}, {}, {}, {}
    for n in _SHARDED:
        out_g[n], out_d[n], out_m[n], out_v[n] = _adamw_sharded(
            [received[l][n] for l in range(DEPTH)], w[n], m[n], v[n], "adamw_" + n)
    zero = jnp.zeros((1,), F32)
    res = _adamw_small(small_parts, _pack([w[n] for n in _SMALL] + [zero]), _pack([m[n] for n in _SMALL] + [zero]),
                       _pack([v[n] for n in _SMALL] + [zero]))
    unpacked = [_unpack(r, small_shapes) for r in res]
    for i, n in enumerate(_SMALL):
        out_g[n], out_d[n], out_m[n], out_v[n] = (u[i] for u in unpacked)
    loss = unpacked[0][-1].reshape(())
    return (loss, dx.reshape(x.shape), *[out_g[n] for n in _ORDER], *[out_d[n] for n in _ORDER],
            *[out_m[n] for n in _ORDER], *[out_v[n] for n in _ORDER])
```

```python
import functools
import math

import jax
import jax.numpy as jnp
from jax import lax
from jax.experimental import pallas as pl
from jax.experimental.pallas import tpu as pltpu

D_MODEL = 1024
SEQ = 16384
DEPTH = 2
N_Q = 8
N_KV = 2
GROUP = N_Q // N_KV
HEAD_DIM = 64
ATTN_W = N_Q * HEAD_DIM
KV_W = N_KV * HEAD_DIM
BLOCK = 128
CONV_C = D_MODEL // 2
CONV_K = 31
D_FF = 4 * D_MODEL
QKV_W = ATTN_W + 2 * KV_W
IN_W = QKV_W + 2 * CONV_C + 2 * D_MODEL
EPS = 1e-6
NEG = -1e30
N_DEV = 8

ADAM_LR = 0.001
ADAM_B1 = 0.9
ADAM_B2 = 0.999
ADAM_EPS = 1e-08
ADAM_WD = 0.01
ADAM_STEP = 10

F32 = jnp.float32
BF16 = jnp.bfloat16
MESH = pl.DeviceIdType.MESH

SUBLANES = 8
HALO = 32
FF_CHUNK = 1024
CONV_ROWS = 16
VMEM_LIMIT = 52 * 1024 * 1024

_NT = (((1,), (1,)), ((), ()))
_TN = (((0,), (0,)), ((), ()))


def _params(*sem):
    return pltpu.CompilerParams(dimension_semantics=sem, vmem_limit_bytes=VMEM_LIMIT)


def _tile(n, pref):
    t = min(n, pref)
    assert n % t == 0, (n, t)
    return t


def _sigmoid(v):
    return 1.0 / (1.0 + jnp.exp(-v))


def _rows8(v):
    r, n = v.shape
    return jnp.sum(v.reshape(r // SUBLANES, SUBLANES, n), axis=0)


def _dot(a, b):
    return jnp.dot(a, b, preferred_element_type=F32)


def _dot_nt(a, b):
    return lax.dot_general(a, b, _NT, preferred_element_type=F32)


def _dot_tn(a, b):
    return lax.dot_general(a, b, _TN, preferred_element_type=F32)


def _rms_bwd(xv, g, dh):
    r = lax.rsqrt(jnp.mean(xv * xv, axis=-1, keepdims=True) + EPS)
    xhat = xv * r
    dxhat = dh * g
    dx = r * (dxhat - xhat * jnp.mean(dxhat * xhat, axis=-1, keepdims=True))
    return dx, dh * xhat


def _row_spec(tm, n, col=0):
    return pl.BlockSpec((tm, n), lambda i: (i, col))


def _full_spec(shape):
    return pl.BlockSpec(shape, lambda *_: (0,) * len(shape))


def _weight_spec(shape):
    return pl.BlockSpec(shape, lambda *_: (0,) * len(shape), pipeline_mode=pl.Buffered(1))


def _mesh_pos():
    return lax.axis_index("x"), lax.axis_index("y"), lax.axis_index("c")


def _dev_index(dev):
    return 4 * dev[0] + 2 * dev[1] + dev[2]


class _Exchange:
    middle_at = None

    def __init__(self, arrs):
        self.arrays = list(arrs)

    def out_shape(self):
        return [jax.ShapeDtypeStruct(a.shape, a.dtype) for a in self.arrays]

    def scratch(self):
        n = len(self.arrays)
        return [pltpu.SemaphoreType.DMA((7 * n,)), pltpu.SemaphoreType.DMA((7 * n,)), pltpu.SemaphoreType.DMA((n,))]

    def _copies(self, ins, outs, sems):
        send_sems, recv_sems, local_sems = sems
        x, y, c = _mesh_pos()
        me = _dev_index((x, y, c))
        mine, sends, arrivals = [], [], []
        for p in range(len(self.arrays)):
            mine.append(pltpu.make_async_copy(ins[p].at[me], outs[p].at[me], local_sems.at[p]))
            for k in range(1, N_DEV):
                peer = (1 - x if k & 4 else x, 1 - y if k & 2 else y, 1 - c if k & 1 else c)
                pid = _dev_index(peer)
                pair = dict(send_sem=send_sems.at[7 * p + k - 1], recv_sem=recv_sems.at[7 * p + k - 1],
                            device_id=peer, device_id_type=MESH)
                sends.append(pltpu.make_async_remote_copy(src_ref=ins[p].at[pid], dst_ref=outs[p].at[me], **pair))
                arrivals.append(pltpu.make_async_remote_copy(src_ref=ins[p].at[pid], dst_ref=outs[p].at[pid], **pair))
        return mine, sends, arrivals

    def start(self, ins, outs, sems):
        mine, sends, _ = self._copies(ins, outs, sems)
        for cp in mine + sends:
            cp.start()

    def finish(self, ins, outs, sems):
        mine, sends, arrivals = self._copies(ins, outs, sems)
        for cp in arrivals:
            cp.wait_recv()
        for cp in sends:
            cp.wait_send()
        for cp in mine:
            cp.wait()


class _Gather:
    middle_at = 0.75

    def __init__(self, arrs):
        self.arrays = list(arrs)

    def out_shape(self):
        return [jax.ShapeDtypeStruct((N_DEV,) + a.shape, a.dtype) for a in self.arrays]

    def scratch(self):
        n = len(self.arrays)
        return [pltpu.SemaphoreType.DMA((7 * n,)), pltpu.SemaphoreType.DMA((7 * n,)), pltpu.SemaphoreType.DMA((n,))]

    def _copies(self, ins, outs, sems):
        send_sems, recv_sems, local_sems = sems
        x, y, c = _mesh_pos()
        me, sibling = (x, y, c), (x, y, 1 - c)
        chips = [(1 - x, y), (x, 1 - y), (1 - x, 1 - y)]
        n = len(self.arrays)

        def copy(p, k, dev, to, src=None):
            block = outs[p].at[_dev_index(dev)]
            return pltpu.make_async_remote_copy(
                src_ref=block if src is None else src, dst_ref=block,
                send_sem=send_sems.at[7 * p + k], recv_sem=recv_sems.at[7 * p + k],
                device_id=to, device_id_type=MESH)

        cp = dict(mine=[pltpu.make_async_copy(ins[p], outs[p].at[_dev_index(me)], local_sems.at[p])
                        for p in range(n)])
        cp["first"] = [copy(p, 0, me, sibling, src=ins[p]) for p in range(n)]
        cp["first"] += [copy(p, 1 + j, me, (*chip, c), src=ins[p]) for p in range(n) for j, chip in enumerate(chips)]
        cp["over_ici"] = [copy(p, 1 + j, (*chip, c), me) for j, chip in enumerate(chips) for p in range(n)]
        cp["passed"] = [copy(p, 4 + j, (*chip, c), sibling) for j, chip in enumerate(chips) for p in range(n)]
        cp["from_sibling"] = [copy(p, 0, sibling, me) for p in range(n)]
        cp["from_sibling"] += [copy(p, 4 + j, (*chip, 1 - c), me) for j, chip in enumerate(chips) for p in range(n)]
        return cp

    def start(self, ins, outs, sems):
        cp = self._copies(ins, outs, sems)
        for d in cp["mine"] + cp["first"]:
            d.start()

    def middle(self, ins, outs, sems):
        cp = self._copies(ins, outs, sems)
        for arrived, onward in zip(cp["over_ici"], cp["passed"]):
            arrived.wait_recv()
            onward.start()

    def finish(self, ins, outs, sems):
        cp = self._copies(ins, outs, sems)
        for d in cp["from_sibling"]:
            d.wait_recv()
        for d in cp["first"] + cp["passed"]:
            d.wait_send()
        for d in cp["mine"]:
            d.wait()


def _run_comm(comm, name):
    n = len(comm.arrays)

    def body(*refs):
        ins, outs, sems = refs[:n], refs[n:2 * n], refs[2 * n:]
        comm.start(ins, outs, sems)
        if comm.middle_at is not None:
            comm.middle(ins, outs, sems)
        comm.finish(ins, outs, sems)

    any_spec = pl.BlockSpec(memory_space=pl.ANY)
    return pl.pallas_call(
        body, name=name, in_specs=[any_spec] * n, out_specs=[any_spec] * n, out_shape=comm.out_shape(),
        scratch_shapes=comm.scratch(),
    )(*comm.arrays)


def _pallas(body, *, name, grid, in_specs, out_specs, out_shape, args, sem, scratch_shapes=(), comm=None):
    if comm is None:
        outs = pl.pallas_call(
            body, name=name, grid=grid, in_specs=in_specs, out_specs=out_specs, out_shape=out_shape,
            scratch_shapes=list(scratch_shapes), compiler_params=_params(*sem),
        )(*args)
        return outs, None
    n_in, n_out, n_scr, n_c = len(in_specs), len(out_specs), len(scratch_shapes), len(comm.arrays)
    steps = grid[0]
    middle = None if comm.middle_at is None else min(steps - 1, int(steps * comm.middle_at))

    def carried(*refs):
        ins, refs = refs[:n_in], refs[n_in:]
        cins, refs = refs[:n_c], refs[n_c:]
        outs, refs = refs[:n_out], refs[n_out:]
        couts, refs = refs[:n_c], refs[n_c:]
        scr, csems = refs[:n_scr], refs[n_scr:]
        step = pl.program_id(0)

        @pl.when(step == 0)
        def _():
            comm.start(cins, couts, csems)

        body(*ins, *outs, *scr)

        if middle is not None:
            @pl.when(step == middle)
            def _():
                comm.middle(cins, couts, csems)

        @pl.when(step == steps - 1)
        def _():
            comm.finish(cins, couts, csems)

    any_spec = pl.BlockSpec(memory_space=pl.ANY)
    res = pl.pallas_call(
        carried, name=name, grid=grid,
        in_specs=list(in_specs) + [any_spec] * n_c, out_specs=list(out_specs) + [any_spec] * n_c,
        out_shape=list(out_shape) + comm.out_shape(),
        scratch_shapes=list(scratch_shapes) + comm.scratch(),
        compiler_params=_params(*(("arbitrary",) + tuple(sem[1:]))),
    )(*args, *comm.arrays)
    return res[:n_out], res[n_out:]


def _inproj_fwd(x, g, w, b, cw, cb, lg, lb, comm=None):
    T, D = x.shape
    C = CONV_C
    rest_w = IN_W - QKV_W
    tm = _tile(T, 512)
    tp = _tile(tm, 256)

    def body(x_ref, g_ref, w_ref, b_ref, cw_ref, cb_ref, lg_ref, lb_ref, h_ref, qkv_ref, rest_ref, u0_ref, yc_ref,
             u_ref, ubuf, carry):
        @pl.when(pl.program_id(0) == 0)
        def _():
            carry[...] = jnp.zeros_like(carry)

        for part in range(tm // tp):
            rows = slice(part * tp, (part + 1) * tp)
            xv = x_ref[rows, :]
            r = lax.rsqrt(jnp.mean(xv * xv, axis=-1, keepdims=True) + EPS)
            h = (xv * r * g_ref[...]).astype(BF16)
            h_ref[rows, :] = h
            qkv_ref[rows, :] = (_dot(h, w_ref[:, :QKV_W]) + b_ref[:, :QKV_W]).astype(BF16)
            rest = (_dot(h, w_ref[:, QKV_W:]) + b_ref[:, QKV_W:]).astype(BF16)
            rest_ref[rows, :] = rest
            u0 = rest[:, :C].astype(F32) * _sigmoid(rest[:, C:2 * C].astype(F32))
            _conv_rows_fwd(u0, carry, ubuf, cw_ref, cb_ref, lg_ref, lb_ref, u0_ref.at[rows, :], yc_ref.at[rows, :],
                           u_ref.at[rows, :])

    return _pallas(
        body, name="inproj_fwd", grid=(T // tm,),
        in_specs=[_row_spec(tm, D), _full_spec((1, D)), _weight_spec((D, IN_W)), _full_spec((1, IN_W)),
                  _full_spec((CONV_K, C)), _full_spec((1, C)), _full_spec((1, C)), _full_spec((1, C))],
        out_specs=[_row_spec(tm, D), _row_spec(tm, QKV_W), _row_spec(tm, rest_w), _row_spec(tm, C),
                   _row_spec(tm, C), _row_spec(tm, C)],
        out_shape=[jax.ShapeDtypeStruct((T, D), BF16), jax.ShapeDtypeStruct((T, QKV_W), BF16),
                   jax.ShapeDtypeStruct((T, rest_w), BF16), jax.ShapeDtypeStruct((T, C), F32),
                   jax.ShapeDtypeStruct((T, C), F32), jax.ShapeDtypeStruct((T, C), BF16)],
        scratch_shapes=[pltpu.VMEM((SUBLANES, tp + HALO, C), F32), pltpu.VMEM((HALO, C), F32)],
        args=(x, g, w, b, cw, cb, lg, lb), sem=("arbitrary",), comm=comm)


def _fold_masks(first):
    shape = (BLOCK, GROUP * BLOCK)
    key = lax.broadcasted_iota(jnp.int32, shape, 0)
    qry = lax.broadcasted_iota(jnp.int32, shape, 1) & (BLOCK - 1)
    upper = key > qry
    dist = jnp.where(upper, qry + BLOCK - key, qry - key)
    keep = key <= qry + jnp.where(first, 0, BLOCK)
    return upper, dist.astype(F32), keep


def _fold(band, upper):
    return jnp.where(upper, band[:BLOCK, :], band[BLOCK:, :])


def _unfold(folded, upper):
    return jnp.concatenate([jnp.where(upper, folded, 0.0), jnp.where(upper, 0.0, folded)], axis=0)


def _head_row(values):
    return jnp.concatenate([jnp.full((1, BLOCK), v, F32) for v in values], axis=1)


def _head_consts(sink_ref, distf):
    bias, sink = [], []
    for kh in range(N_KV):
        heads = range(kh * GROUP, (kh + 1) * GROUP)
        bias.append(_head_row([2.0 ** (-8.0 * (h + 1) / N_Q) for h in heads]) * distf)
        sink.append(_head_row([sink_ref[h] for h in heads]))
    return bias, sink


def _heads_out(t):
    stacked = jnp.concatenate([t[:, g * BLOCK:(g + 1) * BLOCK] for g in range(GROUP)], axis=0)
    return stacked.T


def _fill_kv(kv_buf, cur_ref, prev_ref):
    scale = 1.0 / math.sqrt(HEAD_DIM)
    assert math.frexp(scale)[0] == 0.5
    for r0, ref in ((0, prev_ref), (BLOCK, cur_ref)):
        rows = ref.shape[0]
        kv_buf[r0:r0 + rows, :KV_W] = ref[:, ATTN_W:ATTN_W + KV_W] * scale
        kv_buf[r0:r0 + rows, KV_W:] = ref[:, ATTN_W + KV_W:]


def _group_rows(x, kh):
    return jnp.concatenate([x[:, h * HEAD_DIM:(h + 1) * HEAD_DIM] for h in range(kh * GROUP, (kh + 1) * GROUP)],
                           axis=0)


def _attn_probs(scores, bias, sink, upper, keep):
    s = _fold(scores, upper) - bias
    if keep is not None:
        s = jnp.where(keep, s, NEG)
    m = jnp.maximum(jnp.max(s, axis=0, keepdims=True), sink)
    p = jnp.exp(s - m)
    e = jnp.exp(sink - m)
    inv = 1.0 / (jnp.sum(p, axis=0, keepdims=True) + e)
    return p * inv, e * inv


def _attn_fwd(qkv, sinks):
    T = qkv.shape[0]
    tq = _tile(T, 2048)
    nblk = tq // BLOCK

    def body(sink_ref, cur_ref, prev_ref, o_ref, kv_buf):
        _fill_kv(kv_buf, cur_ref, prev_ref)
        upper, distf, keep = _fold_masks(pl.program_id(0) == 0)
        bias, sink = _head_consts(sink_ref, distf)
        for j in range(nblk):
            band = kv_buf[j * BLOCK:(j + 2) * BLOCK, :]
            q = cur_ref[j * BLOCK:(j + 1) * BLOCK, :ATTN_W]
            outs = []
            for kh in range(N_KV):
                k = band[:, kh * HEAD_DIM:(kh + 1) * HEAD_DIM]
                v = band[:, KV_W + kh * HEAD_DIM:KV_W + (kh + 1) * HEAD_DIM]
                p, _ = _attn_probs(_dot_nt(k, _group_rows(q, kh)), bias[kh], sink[kh], upper,
                                   keep if j == 0 else None)
                outs.append(_heads_out(_dot_tn(v, _unfold(p, upper).astype(BF16))))
            o_ref[j * BLOCK:(j + 1) * BLOCK, :] = jnp.concatenate(outs, axis=1).astype(BF16)

    return pl.pallas_call(
        body, name="attn_fwd", grid=(T // tq,),
        in_specs=[pl.BlockSpec(memory_space=pltpu.SMEM),
                  _row_spec(tq, QKV_W),
                  pl.BlockSpec((BLOCK, QKV_W), lambda i: (jnp.maximum(i * nblk - 1, 0), 0))],
        out_specs=_row_spec(tq, ATTN_W),
        out_shape=jax.ShapeDtypeStruct((T, ATTN_W), BF16),
        scratch_shapes=[pltpu.VMEM((tq + BLOCK, 2 * KV_W), BF16)],
        compiler_params=_params("parallel"),
    )(sinks, qkv, qkv)


def _shifted_copies(buf):
    n = buf.shape[1] - SUBLANES
    for s in range(1, SUBLANES):
        buf[s, 0:n, :] = buf[0, s:s + n, :]


def _shifted_rows(buf, start, rows):
    s = start % SUBLANES
    return buf[s, start - s:start - s + rows, :]


def _conv_rows_fwd(u0, carry, ubuf, w_ref, cb_ref, g_ref, b_ref, u0_ref, yc_ref, u_ref):
    n, C = u0.shape
    R = _tile(n, CONV_ROWS)
    ubuf[0, 0:HALO, :] = carry[...]
    ubuf[0, HALO:, :] = u0
    carry[...] = u0[n - HALO:, :]
    u0_ref[...] = u0
    _shifted_copies(ubuf)
    off = HALO - (CONV_K - 1)
    for c in range(n // R):
        acc = jnp.broadcast_to(cb_ref[...], (R, C))
        for j in range(CONV_K):
            acc = acc + w_ref[j:j + 1, :] * _shifted_rows(ubuf, c * R + off + j, R)
        yc_ref[c * R:(c + 1) * R, :] = acc
        xc = acc - jnp.mean(acc, axis=-1, keepdims=True)
        ln = xc * lax.rsqrt(jnp.mean(xc * xc, axis=-1, keepdims=True) + EPS) * g_ref[...] + b_ref[...]
        u_ref[c * R:(c + 1) * R, :] = (ln * _sigmoid(ln)).astype(BF16)


def _merge_fwd(attn, u, rest, x, wa, wc, bc, wo):
    T, D = x.shape
    tm = _tile(T, 512)
    gcol = 2 * CONV_C // D

    def body(attn_ref, u_ref, ga_ref, gc_ref, x_ref, wa_ref, wc_ref, bc_ref, wo_ref, m_ref, x1_ref):
        bra = _dot(attn_ref[...], wa_ref[...])
        brc = _dot(u_ref[...], wc_ref[...]) + bc_ref[...]
        mb = (_sigmoid(ga_ref[...].astype(F32)) * bra + _sigmoid(gc_ref[...].astype(F32)) * brc).astype(BF16)
        m_ref[...] = mb
        x1_ref[...] = x_ref[...] + _dot(mb, wo_ref[...])

    return pl.pallas_call(
        body, name="merge_fwd", grid=(T // tm,),
        in_specs=[_row_spec(tm, ATTN_W), _row_spec(tm, CONV_C), _row_spec(tm, D, gcol), _row_spec(tm, D, gcol + 1),
                  _row_spec(tm, D), _weight_spec((ATTN_W, D)), _weight_spec((CONV_C, D)), _full_spec((1, D)),
                  _weight_spec((D, D))],
        out_specs=[_row_spec(tm, D), _row_spec(tm, D)],
        out_shape=[jax.ShapeDtypeStruct((T, D), BF16), jax.ShapeDtypeStruct((T, D), F32)],
        compiler_params=_params("parallel"),
    )(attn, u, rest, rest, x, wa, wc, bc, wo)


def _mlp_tile(x_ref, g_ref, w1_ref, w2_ref, h_ref, z_ref):
    fc = _tile(D_FF, FF_CHUNK)
    xv = x_ref[...]
    r = lax.rsqrt(jnp.mean(xv * xv, axis=-1, keepdims=True) + EPS)
    h = (xv * r * g_ref[...]).astype(BF16)
    h_ref[...] = h
    acc = xv
    for c in range(D_FF // fc):
        cols = slice(c * fc, (c + 1) * fc)
        z = _dot(h, w1_ref[:, cols])
        z_ref[:, cols] = z.astype(BF16)
        acc = acc + _dot(jnp.square(jnp.maximum(z, 0.0)).astype(BF16), w2_ref[cols, :])
    return acc


def _mlp_fwd(x1, g, w1, w2, comm=None):
    T, D = x1.shape
    tm = _tile(T, 512)

    def body(x_ref, g_ref, w1_ref, w2_ref, h_ref, z_ref, o_ref):
        o_ref[...] = _mlp_tile(x_ref, g_ref, w1_ref, w2_ref, h_ref, z_ref)

    return _pallas(
        body, name="mlp_fwd", grid=(T // tm,),
        in_specs=[_row_spec(tm, D), _full_spec((1, D)), _weight_spec((D, D_FF)), _weight_spec((D_FF, D))],
        out_specs=[_row_spec(tm, D), _row_spec(tm, D_FF), _row_spec(tm, D)],
        out_shape=[jax.ShapeDtypeStruct((T, D), BF16), jax.ShapeDtypeStruct((T, D_FF), BF16),
                   jax.ShapeDtypeStruct((T, D), F32)],
        args=(x1, g, w1, w2), sem=("parallel",), comm=comm)


def _mlp_fwd_loss(x1, g, w1, w2, gf, target):
    T, D = x1.shape
    tm = _tile(T, 512)

    def body(x_ref, g_ref, w1_ref, w2_ref, gf_ref, t_ref, h_ref, z_ref, l_ref, dx_ref, dg_ref):
        @pl.when(pl.program_id(0) == 0)
        def _():
            l_ref[...] = jnp.zeros_like(l_ref)
            dg_ref[...] = jnp.zeros_like(dg_ref)

        x2 = _mlp_tile(x_ref, g_ref, w1_ref, w2_ref, h_ref, z_ref)
        r = lax.rsqrt(jnp.mean(x2 * x2, axis=-1, keepdims=True) + EPS)
        e = x2 * r * gf_ref[...] - t_ref[...]
        l_ref[...] += _rows8(e * e) * (0.5 / D)
        dx, dg = _rms_bwd(x2, gf_ref[...], e * (1.0 / D))
        dx_ref[...] = dx
        dg_ref[...] += _rows8(dg)

    return pl.pallas_call(
        body, name="mlp_fwd_loss", grid=(T // tm,),
        in_specs=[_row_spec(tm, D), _full_spec((1, D)), _weight_spec((D, D_FF)), _weight_spec((D_FF, D)),
                  _full_spec((1, D)), _row_spec(tm, D)],
        out_specs=[_row_spec(tm, D), _row_spec(tm, D_FF), _full_spec((SUBLANES, D)), _row_spec(tm, D),
                   _full_spec((SUBLANES, D))],
        out_shape=[jax.ShapeDtypeStruct((T, D), BF16), jax.ShapeDtypeStruct((T, D_FF), BF16),
                   jax.ShapeDtypeStruct((SUBLANES, D), F32), jax.ShapeDtypeStruct((T, D), F32),
                   jax.ShapeDtypeStruct((SUBLANES, D), F32)],
        compiler_params=_params("arbitrary"),
    )(x1, g, w1, w2, gf, target)


def _mlp_bwd(dx2, x1, z, g, w1, w2, comm=None):
    T, D = x1.shape
    tm = _tile(T, 512)
    fc = _tile(D_FF, FF_CHUNK)

    def body(dx2_ref, x_ref, z_ref, g_ref, w1_ref, w2_ref, dx1_ref, dz_ref, dg_ref):
        @pl.when(pl.program_id(0) == 0)
        def _():
            dg_ref[...] = jnp.zeros_like(dg_ref)

        dxo = dx2_ref[...]
        dxb = dxo.astype(BF16)
        dh = jnp.zeros((tm, D), F32)
        for c in range(D_FF // fc):
            cols = slice(c * fc, (c + 1) * fc)
            da = _dot_nt(dxb, w2_ref[cols, :])
            dz = (da * (2.0 * jnp.maximum(z_ref[:, cols].astype(F32), 0.0))).astype(BF16)
            dz_ref[:, cols] = dz
            dh = dh + _dot_nt(dz, w1_ref[:, cols])
        dx, dg = _rms_bwd(x_ref[...], g_ref[...], dh)
        dx1_ref[...] = dxo + dx
        dg_ref[...] += _rows8(dg)

    return _pallas(
        body, name="mlp_bwd", grid=(T // tm,),
        in_specs=[_row_spec(tm, D), _row_spec(tm, D), _row_spec(tm, D_FF), _full_spec((1, D)),
                  _weight_spec((D, D_FF)), _weight_spec((D_FF, D))],
        out_specs=[_row_spec(tm, D), _row_spec(tm, D_FF), _full_spec((SUBLANES, D))],
        out_shape=[jax.ShapeDtypeStruct((T, D), F32), jax.ShapeDtypeStruct((T, D_FF), BF16),
                   jax.ShapeDtypeStruct((SUBLANES, D), F32)],
        args=(dx2, x1, z, g, w1, w2), sem=("arbitrary",), comm=comm)


def _tn_blocks(a, b, name, col_sharded, relu_sq=False):
    T, M = a.shape
    N = b.shape[1]
    tk = _tile(T, 1024)
    tm = _tile(M, 512 if col_sharded else 1024)
    nb = N // N_DEV
    last = T // tk - 1

    def body(a_ref, b_ref, o_ref, acc_ref):
        k = pl.program_id(1)

        @pl.when(k == 0)
        def _():
            acc_ref[...] = jnp.zeros_like(acc_ref)

        av = a_ref[...]
        if relu_sq:
            av = jnp.square(jnp.maximum(av, 0.0))
        acc_ref[...] += _dot_tn(av.astype(BF16), b_ref[...].astype(BF16))

        @pl.when(k == last)
        def _():
            if col_sharded:
                for d in range(N_DEV):
                    o_ref[d] = acc_ref[:, d * nb:(d + 1) * nb].astype(BF16)
            else:
                o_ref[...] = acc_ref[...].astype(BF16)

    if col_sharded:
        out_spec = pl.BlockSpec((N_DEV, tm, nb), lambda i, k: (0, i, 0))
        out_shape = jax.ShapeDtypeStruct((N_DEV, M, nb), BF16)
    else:
        out_spec = pl.BlockSpec((tm, N), lambda i, k: (i, 0))
        out_shape = jax.ShapeDtypeStruct((M, N), BF16)
    out = pl.pallas_call(
        body, name=name, grid=(M // tm, T // tk),
        in_specs=[pl.BlockSpec((tk, tm), lambda i, k: (k, i)), pl.BlockSpec((tk, N), lambda i, k: (k, 0))],
        out_specs=out_spec, out_shape=out_shape,
        scratch_shapes=[pltpu.VMEM((tm, N), F32)],
        compiler_params=_params("parallel", "arbitrary"),
    )(a, b)
    return out if col_sharded else out.reshape(N_DEV, M // N_DEV, N)


def _merge_bwd(dx1, attn, u, rest, yc, wa, wc, bc, wo, lg, lb, comm=None):
    T, D = dx1.shape
    C = CONV_C
    tm = _tile(T, 512)
    gcol = 2 * CONV_C // D

    def body(dx_ref, attn_ref, u_ref, ga_ref, gc_ref, yc_ref, wa_ref, wc_ref, bc_ref, wo_ref, lg_ref, lb_ref,
             dattn_ref, dyc_ref, dga_ref, dgc_ref, dbra_ref, dbrc_ref, dbc_ref, dlg_ref, dlb_ref):
        @pl.when(pl.program_id(0) == 0)
        def _():
            dbc_ref[...] = jnp.zeros_like(dbc_ref)
            dlg_ref[...] = jnp.zeros_like(dlg_ref)
            dlb_ref[...] = jnp.zeros_like(dlb_ref)

        dm = _dot_nt(dx_ref[...].astype(BF16), wo_ref[...])
        bra = _dot(attn_ref[...], wa_ref[...])
        brc = _dot(u_ref[...], wc_ref[...]) + bc_ref[...]
        sa = _sigmoid(ga_ref[...].astype(F32))
        sc = _sigmoid(gc_ref[...].astype(F32))
        dbra = dm * sa
        dbrc = dm * sc
        dga_ref[...] = (dm * bra * sa * (1.0 - sa)).astype(BF16)
        dgc_ref[...] = (dm * brc * sc * (1.0 - sc)).astype(BF16)
        dbra_b = dbra.astype(BF16)
        dbrc_b = dbrc.astype(BF16)
        dbra_ref[...] = dbra_b
        dbrc_ref[...] = dbrc_b
        dbc_ref[...] += _rows8(dbrc)
        dattn_ref[...] = _dot_nt(dbra_b, wa_ref[...]).astype(BF16)
        dyc, dlg, dlb = _swish_norm_bwd(_dot_nt(dbrc_b, wc_ref[...]), yc_ref[...], lg_ref[...], lb_ref[...])
        dyc_ref[...] = dyc
        dlg_ref[...] += _rows8(dlg)
        dlb_ref[...] += _rows8(dlb)

    return _pallas(
        body, name="merge_bwd", grid=(T // tm,),
        in_specs=[_row_spec(tm, D), _row_spec(tm, ATTN_W), _row_spec(tm, C), _row_spec(tm, D, gcol),
                  _row_spec(tm, D, gcol + 1), _row_spec(tm, C), _weight_spec((ATTN_W, D)), _weight_spec((C, D)),
                  _full_spec((1, D)), _weight_spec((D, D)), _full_spec((1, C)), _full_spec((1, C))],
        out_specs=[_row_spec(tm, ATTN_W), _row_spec(tm, C), _row_spec(tm, D), _row_spec(tm, D),
                   _row_spec(tm, D), _row_spec(tm, D), _full_spec((SUBLANES, D)), _full_spec((SUBLANES, C)),
                   _full_spec((SUBLANES, C))],
        out_shape=[jax.ShapeDtypeStruct((T, ATTN_W), BF16), jax.ShapeDtypeStruct((T, C), F32),
                   jax.ShapeDtypeStruct((T, D), BF16), jax.ShapeDtypeStruct((T, D), BF16),
                   jax.ShapeDtypeStruct((T, D), BF16), jax.ShapeDtypeStruct((T, D), BF16),
                   jax.ShapeDtypeStruct((SUBLANES, D), F32), jax.ShapeDtypeStruct((SUBLANES, C), F32),
                   jax.ShapeDtypeStruct((SUBLANES, C), F32)],
        args=(dx1, attn, u, rest, rest, yc, wa, wc, bc, wo, lg, lb), sem=("arbitrary",), comm=comm)


def _swish_norm_bwd(du, yv, g, b):
    xc = yv - jnp.mean(yv, axis=-1, keepdims=True)
    rstd = lax.rsqrt(jnp.mean(xc * xc, axis=-1, keepdims=True) + EPS)
    xn = xc * rstd
    ln = xn * g + b
    sg = _sigmoid(ln)
    dln = du * sg * (1.0 + ln * (1.0 - sg))
    dxn = dln * g
    dyc = rstd * (dxn - jnp.mean(dxn, axis=-1, keepdims=True) - xn * jnp.mean(dxn * xn, axis=-1, keepdims=True))
    return dyc, dln * xn, dln


def _conv_taps_bwd(first, last, dy_ref, dyn_ref, u0_ref, u0p_ref, glu_ref, w_ref, dglu_ref, dw_ref, db_ref,
                   dbuf, ubuf):
    tm, C = dy_ref.shape
    R = _tile(tm, CONV_ROWS)
    dbuf[0, 0:tm, :] = dy_ref[...]
    dbuf[0, tm:, :] = jnp.where(last, 0.0, dyn_ref[...])
    ubuf[0, 0:HALO, :] = jnp.where(first, 0.0, u0p_ref[...])
    ubuf[0, HALO:, :] = u0_ref[...]
    _shifted_copies(dbuf)
    _shifted_copies(ubuf)
    off = HALO - (CONV_K - 1)
    for c in range(tm // R):
        rows = slice(c * R, (c + 1) * R)
        dy = dbuf[0, rows, :]
        acc = jnp.zeros((R, C), F32)
        for j in range(CONV_K):
            acc = acc + w_ref[j:j + 1, :] * _shifted_rows(dbuf, c * R + CONV_K - 1 - j, R)
            dw_ref[j * SUBLANES:(j + 1) * SUBLANES, :] += _rows8(dy * _shifted_rows(ubuf, c * R + off + j, R))
        db_ref[...] += _rows8(dy)
        a = glu_ref[rows, :C].astype(F32)
        sb = _sigmoid(glu_ref[rows, C:].astype(F32))
        dglu_ref[rows, :C] = (acc * sb).astype(BF16)
        dglu_ref[rows, C:] = (acc * a * sb * (1.0 - sb)).astype(BF16)


def _attn_bwd(qkv, dattn, sinks, comm=None):
    T = qkv.shape[0]
    tq = _tile(T, 2048)
    nblk = tq // BLOCK
    scale = 1.0 / math.sqrt(HEAD_DIM)

    def body(sink_ref, cur_ref, prev_ref, do_ref, dq_ref, hi_ref, lo_ref, ds_ref, kv_buf):
        i = pl.program_id(0)

        @pl.when(i == 0)
        def _():
            ds_ref[...] = jnp.zeros_like(ds_ref)

        _fill_kv(kv_buf, cur_ref, prev_ref)
        upper, distf, keep = _fold_masks(i == 0)
        bias, sink = _head_consts(sink_ref, distf)
        for j in range(nblk):
            rows = slice(j * BLOCK, (j + 1) * BLOCK)
            band = kv_buf[j * BLOCK:(j + 2) * BLOCK, :]
            q = cur_ref[rows, :ATTN_W]
            do = do_ref[rows, :]
            dqs, dks, dvs = [], [], []
            for kh in range(N_KV):
                k = band[:, kh * HEAD_DIM:(kh + 1) * HEAD_DIM]
                v = band[:, KV_W + kh * HEAD_DIM:KV_W + (kh + 1) * HEAD_DIM]
                qg = _group_rows(q, kh)
                dog = _group_rows(do, kh)
                p, psink = _attn_probs(_dot_nt(k, qg), bias[kh], sink[kh], upper, keep if j == 0 else None)
                pdp = p * _fold(_dot_nt(v, dog), upper)
                delta = jnp.sum(pdp, axis=0, keepdims=True)
                lanes = slice(kh * GROUP * BLOCK, (kh + 1) * GROUP * BLOCK)
                ds_ref[0:1, lanes] += psink * delta
                dsb = _unfold(pdp - p * delta, upper).astype(BF16)
                dqs.append(_heads_out(_dot_tn(k, dsb)))
                dks.append(_dot(dsb, qg) * scale)
                dvs.append(_dot(_unfold(p, upper).astype(BF16), dog))
            dq_ref[rows, :] = jnp.concatenate(dqs, axis=1).astype(BF16)
            dkv = jnp.concatenate(dks + dvs, axis=1)
            lo_ref[rows, :] = dkv[:BLOCK, :]
            hi_ref[rows, :] = dkv[BLOCK:, :]

    return _pallas(
        body, name="attn_bwd", grid=(T // tq,),
        in_specs=[pl.BlockSpec(memory_space=pltpu.SMEM),
                  _row_spec(tq, QKV_W),
                  pl.BlockSpec((BLOCK, QKV_W), lambda i: (jnp.maximum(i * nblk - 1, 0), 0)),
                  _row_spec(tq, ATTN_W)],
        out_specs=[_row_spec(tq, ATTN_W), _row_spec(tq, 2 * KV_W), _row_spec(tq, 2 * KV_W),
                   _full_spec((SUBLANES, N_Q * BLOCK))],
        out_shape=[jax.ShapeDtypeStruct((T, ATTN_W), BF16), jax.ShapeDtypeStruct((T, 2 * KV_W), F32),
                   jax.ShapeDtypeStruct((T, 2 * KV_W), F32), jax.ShapeDtypeStruct((SUBLANES, N_Q * BLOCK), F32)],
        scratch_shapes=[pltpu.VMEM((tq + BLOCK, 2 * KV_W), BF16)],
        args=(sinks, qkv, qkv, dattn), sem=("arbitrary",), comm=comm)


def _inproj_bwd(dq, hi, lo, dyc, u0, rest, cw, dga, dgc, x, g, w, dx1, comm=None):
    T, D = x.shape
    C = CONV_C
    tm = _tile(T, 256)
    per = tm // BLOCK
    per_halo = tm // HALO
    nt = T // tm
    kv2 = 2 * KV_W
    glu0, gate0 = QKV_W, QKV_W + 2 * C

    def body(dq_ref, hi_ref, lo_ref, lon_ref, dy_ref, dyn_ref, u0_ref, u0p_ref, glu_ref, cw_ref, dga_ref, dgc_ref,
             x_ref, g_ref, w_ref, dx1_ref, dp_ref, dx_ref, dg_ref, dbias_ref, dcw_ref, dcb_ref, dbuf, ubuf):
        i = pl.program_id(0)

        @pl.when(i == 0)
        def _():
            for ref in (dg_ref, dbias_ref, dcw_ref, dcb_ref):
                ref[...] = jnp.zeros_like(ref)

        def part(cols):
            dp = dp_ref[:, cols]
            dbias_ref[:, cols] += _rows8(dp.astype(F32))
            return _dot_nt(dp, w_ref[:, cols])

        dp_ref[:, :ATTN_W] = dq_ref[...]
        lo_next = jnp.where(i < nt - 1, lon_ref[...], 0.0)
        lo_shift = jnp.concatenate([lo_ref[BLOCK:, :], lo_next], axis=0) if tm > BLOCK else lo_next
        dp_ref[:, ATTN_W:QKV_W] = (hi_ref[...] + lo_shift).astype(BF16)
        dp_ref[:, gate0:gate0 + D] = dga_ref[...]
        dp_ref[:, gate0 + D:] = dgc_ref[...]
        _conv_taps_bwd(i == 0, i == nt - 1, dy_ref, dyn_ref, u0_ref, u0p_ref, glu_ref, cw_ref,
                       dp_ref.at[:, glu0:gate0], dcw_ref, dcb_ref, dbuf, ubuf)
        dh = part(slice(0, QKV_W)) + part(slice(gate0, IN_W)) + part(slice(glu0, gate0))
        dx, dg = _rms_bwd(x_ref[...], g_ref[...], dh)
        dx_ref[...] = dx1_ref[...] + dx
        dg_ref[...] += _rows8(dg)

    return _pallas(
        body, name="inproj_bwd", grid=(nt,),
        in_specs=[_row_spec(tm, ATTN_W), _row_spec(tm, kv2), _row_spec(tm, kv2),
                  pl.BlockSpec((BLOCK, kv2), lambda i: (jnp.minimum((i + 1) * per, T // BLOCK - 1), 0)),
                  _row_spec(tm, C),
                  pl.BlockSpec((HALO, C), lambda i: (jnp.minimum((i + 1) * per_halo, T // HALO - 1), 0)),
                  _row_spec(tm, C),
                  pl.BlockSpec((HALO, C), lambda i: (jnp.maximum(i * per_halo - 1, 0), 0)),
                  _row_spec(tm, 2 * C), _full_spec((CONV_K, C)),
                  _row_spec(tm, D), _row_spec(tm, D), _row_spec(tm, D), _full_spec((1, D)),
                  _weight_spec((D, IN_W)), _row_spec(tm, D)],
        out_specs=[_row_spec(tm, IN_W), _row_spec(tm, D), _full_spec((SUBLANES, D)), _full_spec((SUBLANES, IN_W)),
                   _full_spec((CONV_K * SUBLANES, C)), _full_spec((SUBLANES, C))],
        out_shape=[jax.ShapeDtypeStruct((T, IN_W), BF16), jax.ShapeDtypeStruct((T, D), F32),
                   jax.ShapeDtypeStruct((SUBLANES, D), F32), jax.ShapeDtypeStruct((SUBLANES, IN_W), F32),
                   jax.ShapeDtypeStruct((CONV_K * SUBLANES, C), F32), jax.ShapeDtypeStruct((SUBLANES, C), F32)],
        scratch_shapes=[pltpu.VMEM((SUBLANES, tm + HALO, C), F32), pltpu.VMEM((SUBLANES, tm + HALO, C), F32)],
        args=(dq, hi, lo, lo, dyc, dyc, u0, u0, rest, cw, dga, dgc, x, g, w, dx1), sem=("arbitrary",), comm=comm)


def _adamw_math(g, w, m, v):
    c1 = 1.0 / (1.0 - ADAM_B1 ** ADAM_STEP)
    c2 = 1.0 / (1.0 - ADAM_B2 ** ADAM_STEP)
    mn = ADAM_B1 * m + (1.0 - ADAM_B1) * g
    vn = ADAM_B2 * v + (1.0 - ADAM_B2) * (g * g)
    return -ADAM_LR * ((mn * c1) / (jnp.sqrt(vn * c2) + ADAM_EPS) + ADAM_WD * w), mn, vn


def _adamw_sharded(parts, w, m, v, name):
    depth, a, b = w.shape
    tr = _tile(a, 256) if a % SUBLANES == 0 else a
    nr = a // tr

    def body(*refs):
        p_refs, (w_ref, m_ref, v_ref, g_ref, d_ref, mo_ref, vo_ref) = refs[:depth], refs[depth:]
        layer = pl.program_id(0)
        for l in range(depth):
            @pl.when(layer == l)
            def _(l=l):
                g = p_refs[l][0].astype(F32)
                for s in range(1, N_DEV):
                    g = g + p_refs[l][s].astype(F32)
                g_ref[...] = g
                d_ref[...], mo_ref[...], vo_ref[...] = _adamw_math(g, w_ref[...], m_ref[...], v_ref[...])

    def part_spec(l):
        return pl.BlockSpec((N_DEV, tr, b),
                            lambda k, i: (0, jnp.where(k == l, i, jnp.where(k < l, 0, nr - 1)), 0))

    spec = pl.BlockSpec((None, tr, b), lambda k, i: (k, i, 0))
    out = jax.ShapeDtypeStruct((depth, a, b), F32)
    return pl.pallas_call(
        body, name=name, grid=(depth, nr),
        in_specs=[part_spec(l) for l in range(depth)] + [spec] * 3,
        out_specs=[spec] * 4, out_shape=[out] * 4,
        compiler_params=_params("arbitrary", "arbitrary"),
    )(*parts, w, m, v)


def _adamw_small(parts, w, m, v):
    R, N = w.shape

    def body(p_ref, w_ref, m_ref, v_ref, g_ref, d_ref, mo_ref, vo_ref):
        g = p_ref[0]
        for s in range(1, N_DEV):
            g = g + p_ref[s]
        g_ref[...] = g
        d_ref[...], mo_ref[...], vo_ref[...] = _adamw_math(g, w_ref[...], m_ref[...], v_ref[...])

    out = jax.ShapeDtypeStruct((R, N), F32)
    return pl.pallas_call(
        body, name="adamw_small", grid=(1,),
        in_specs=[_full_spec((N_DEV, R, N))] + [_full_spec((R, N))] * 3,
        out_specs=[_full_spec((R, N))] * 4, out_shape=[out] * 4,
        compiler_params=_params("arbitrary"),
    )(parts, w, m, v)


_SHARDED = ("w_in", "conv_w", "w_attn_proj", "w_conv_proj", "w_out", "w_mlp1", "w_mlp2")
_ROW_SHARDED = ("w_out", "w_mlp2")
_FIRST = ("w_in", "conv_w")
_REST = tuple(n for n in _SHARDED if n not in _FIRST)
_SMALL = ("mix_norm_g", "b_in", "sinks", "conv_b", "conv_ln_g", "conv_ln_b", "b_conv_proj", "mlp_norm_g",
          "final_norm_g")
_ORDER = ("mix_norm_g", "w_in", "b_in", "sinks", "conv_w", "conv_b", "conv_ln_g", "conv_ln_b", "w_attn_proj",
          "w_conv_proj", "b_conv_proj", "w_out", "mlp_norm_g", "w_mlp1", "w_mlp2", "final_norm_g")
_PACK = 1024


def _full_weights(names, gathered):
    cols = [i for i, n in enumerate(names) if n not in _ROW_SHARDED]

    def body(*refs):
        for src, dst in zip(refs[:len(cols)], refs[len(cols):]):
            b = src.shape[2]
            for d in range(N_DEV):
                dst[:, d * b:(d + 1) * b] = src[d]

    vmem = pl.BlockSpec(memory_space=pltpu.VMEM)
    placed = pl.pallas_call(
        body, name="place_" + names[cols[0]], in_specs=[vmem] * len(cols), out_specs=[vmem] * len(cols),
        out_shape=[jax.ShapeDtypeStruct((gathered[i].shape[1], N_DEV * gathered[i].shape[2]), gathered[i].dtype)
                   for i in cols],
        compiler_params=pltpu.CompilerParams(vmem_limit_bytes=VMEM_LIMIT),
    )(*[gathered[i] for i in cols])
    full = {names[i]: a for i, a in zip(cols, placed)}
    for n, a in zip(names, gathered):
        if n in _ROW_SHARDED:
            full[n] = a.reshape(N_DEV * a.shape[1], a.shape[2])
    return full


def _pack(arrs):
    flat = []
    for a in arrs:
        a = a.reshape(-1)
        flat.append(jnp.pad(a, (0, -a.size % _PACK)))
    return jnp.concatenate(flat).reshape(-1, BLOCK)


def _unpack(packed, shapes):
    flat = packed.reshape(-1)
    out, off = [], 0
    for s in shapes:
        n = math.prod(s)
        out.append(flat[off:off + n].reshape(s))
        off += n + (-n % _PACK)
    return out


def _layer_fwd(x, lw, own_rest=None, comm=None, loss_head=None):
    (h, qkv, rest, u0, yc, u), got = _inproj_fwd(x, lw["mix_norm_g"], lw["w_in"], lw["b_in"], lw["conv_w"],
                                                 lw["conv_b"], lw["conv_ln_g"], lw["conv_ln_b"], own_rest)
    if got is not None:
        lw.update(_full_weights(_REST, got))
    attn = _attn_fwd(qkv, lw["sinks"])
    merged, x1 = _merge_fwd(attn, u, rest, x, lw["w_attn_proj"], lw["w_conv_proj"], lw["b_conv_proj"], lw["w_out"])
    if loss_head is None:
        (h2, z, out), gathered = _mlp_fwd(x1, lw["mlp_norm_g"], lw["w_mlp1"], lw["w_mlp2"], comm)
    else:
        assert comm is None
        h2, z, *out = _mlp_fwd_loss(x1, lw["mlp_norm_g"], lw["w_mlp1"], lw["w_mlp2"], *loss_head)
        gathered = None
    saved = dict(x=x, h=h, qkv=qkv, rest=rest, attn=attn, u0=u0, yc=yc, u=u, merged=merged, x1=x1, h2=h2, z=z)
    return out, saved, gathered


_EARLY = ("w_mlp1", "w_mlp2")
_MIDDLE = ("w_out", "w_attn_proj", "w_conv_proj")
_LATE = ("w_in", "conv_w")


def _layer_bwd(dx2, lw, s, late_blocks, dx_is_result):
    g, recv = {}, {}
    late = None if late_blocks is None else _Exchange(late_blocks)
    (dx1, dz, dg2), late_recv = _mlp_bwd(dx2, s["x1"], s["z"], lw["mlp_norm_g"], lw["w_mlp1"], lw["w_mlp2"], late)
    g["mlp_norm_g"] = jnp.sum(dg2, axis=0)
    early = [_tn_blocks(s["h2"], dz, "dw_mlp1", True), _tn_blocks(s["z"], dx2, "dw_mlp2", False, relu_sq=True)]
    (dattn, dyc, dga, dgc, dbra, dbrc, dbc, dlg, dlb), behind_merge = _merge_bwd(
        dx1, s["attn"], s["u"], s["rest"], s["yc"], lw["w_attn_proj"], lw["w_conv_proj"], lw["b_conv_proj"],
        lw["w_out"], lw["conv_ln_g"], lw["conv_ln_b"], _Exchange(early[:1]) if dx_is_result else None)
    g["b_conv_proj"] = jnp.sum(dbc, axis=0)
    g["conv_ln_g"] = jnp.sum(dlg, axis=0)
    g["conv_ln_b"] = jnp.sum(dlb, axis=0)
    middle = [_tn_blocks(s["merged"], dx1, "dw_out", False), _tn_blocks(s["attn"], dbra, "dw_attn_proj", True),
              _tn_blocks(s["u"], dbrc, "dw_conv_proj", True)]
    (dq, hi, lo, dsk), behind_attn = _attn_bwd(s["qkv"], dattn, lw["sinks"],
                                               _Exchange(middle + (early[1:] if dx_is_result else [])))
    recv.update(zip(_MIDDLE, behind_attn))
    g["sinks"] = -jnp.sum(dsk[0].reshape(N_Q, BLOCK), axis=1)
    (dproj, dx, dg1, dbin, dcw, dcb), behind_inproj = _inproj_bwd(
        dq, hi, lo, dyc, s["u0"], s["rest"], lw["conv_w"], dga, dgc, s["x"], lw["mix_norm_g"], lw["w_in"], dx1,
        None if dx_is_result else _Exchange(early))
    recv.update(zip(_EARLY, behind_merge + behind_attn[len(middle):] if dx_is_result else behind_inproj))
    dconv_w = jnp.sum(dcw.reshape(CONV_K, SUBLANES, CONV_C), axis=1)
    g["conv_b"] = jnp.sum(dcb, axis=0)
    g["mix_norm_g"] = jnp.sum(dg1, axis=0)
    g["b_in"] = jnp.sum(dbin, axis=0)
    own_late = [_tn_blocks(s["h"], dproj, "dw_in", True),
                dconv_w.reshape(CONV_K, N_DEV, CONV_C // N_DEV).transpose(1, 0, 2)]
    return dx, g, recv, late_recv, own_late


def kernel(x, mix_norm_g, w_in, b_in, sinks, conv_w, conv_b, conv_ln_g, conv_ln_b, w_attn_proj, w_conv_proj, b_conv_proj, w_out, mlp_norm_g, w_mlp1, w_mlp2, final_norm_g, loss_target, m_mix_norm_g, m_w_in, m_b_in, m_sinks, m_conv_w, m_conv_b, m_conv_ln_g, m_conv_ln_b, m_w_attn_proj, m_w_conv_proj, m_b_conv_proj, m_w_out, m_mlp_norm_g, m_w_mlp1, m_w_mlp2, m_final_norm_g, v_mix_norm_g, v_w_in, v_b_in, v_sinks, v_conv_w, v_conv_b, v_conv_ln_g, v_conv_ln_b, v_w_attn_proj, v_w_conv_proj, v_b_conv_proj, v_w_out, v_mlp_norm_g, v_w_mlp1, v_w_mlp2, v_final_norm_g):
    w = dict(mix_norm_g=mix_norm_g, w_in=w_in, b_in=b_in, sinks=sinks, conv_w=conv_w, conv_b=conv_b,
             conv_ln_g=conv_ln_g, conv_ln_b=conv_ln_b, w_attn_proj=w_attn_proj, w_conv_proj=w_conv_proj,
             b_conv_proj=b_conv_proj, w_out=w_out, mlp_norm_g=mlp_norm_g, w_mlp1=w_mlp1, w_mlp2=w_mlp2,
             final_norm_g=final_norm_g)
    m = dict(mix_norm_g=m_mix_norm_g, w_in=m_w_in, b_in=m_b_in, sinks=m_sinks, conv_w=m_conv_w, conv_b=m_conv_b,
             conv_ln_g=m_conv_ln_g, conv_ln_b=m_conv_ln_b, w_attn_proj=m_w_attn_proj, w_conv_proj=m_w_conv_proj,
             b_conv_proj=m_b_conv_proj, w_out=m_w_out, mlp_norm_g=m_mlp_norm_g, w_mlp1=m_w_mlp1, w_mlp2=m_w_mlp2,
             final_norm_g=m_final_norm_g)
    v = dict(mix_norm_g=v_mix_norm_g, w_in=v_w_in, b_in=v_b_in, sinks=v_sinks, conv_w=v_conv_w, conv_b=v_conv_b,
             conv_ln_g=v_conv_ln_g, conv_ln_b=v_conv_ln_b, w_attn_proj=v_w_attn_proj, w_conv_proj=v_w_conv_proj,
             b_conv_proj=v_b_conv_proj, w_out=v_w_out, mlp_norm_g=v_mlp_norm_g, w_mlp1=v_w_mlp1, w_mlp2=v_w_mlp2,
             final_norm_g=v_final_norm_g)
    T = x.shape[1]
    xs = x.reshape(T, D_MODEL)
    target = loss_target.reshape(T, D_MODEL)

    def gather_of(l, names):
        return _Gather([w[n][l] if n == "conv_w" else w[n][l].astype(BF16) for n in names])

    def layer_weights(l, names, gathered):
        lw = _full_weights(names, gathered)
        for n in _SMALL:
            if n != "final_norm_g":
                lw[n] = w[n][l] if n == "sinks" else w[n][l].reshape(1, -1)
        return lw

    acts = xs
    saved, weights = [], []
    for l in range(DEPTH):
        last = l + 1 == DEPTH
        following = None if last else gather_of(l + 1, _SHARDED)
        loss_head = (final_norm_g.reshape(1, -1), target) if last else None
        if l == 0:
            lw = layer_weights(0, _FIRST, _run_comm(gather_of(0, _FIRST), "gather_first"))
            acts, s, gathered = _layer_fwd(acts, lw, gather_of(0, _REST), following, loss_head)
        else:
            lw = layer_weights(l, _SHARDED, gathered)
            acts, s, gathered = _layer_fwd(acts, lw, None, following, loss_head)
        weights.append(lw)
        saved.append(s)
    lterms, dx, dgf = acts
    grads, received = [None] * DEPTH, [None] * DEPTH
    late = None
    for l in reversed(range(DEPTH)):
        dx, grads[l], received[l], late_recv, late = _layer_bwd(dx, weights[l], saved[l], late, l == 0)
        if late_recv is not None:
            received[l + 1].update(zip(_LATE, late_recv))
    received[0].update(zip(_LATE, _run_comm(_Exchange(late), "scatter_late")))
    grad = {n: jnp.stack([grads[l][n] for l in range(DEPTH)]) for n in _SMALL if n != "final_norm_g"}
    grad["final_norm_g"] = jnp.sum(dgf, axis=0)

    small_shapes = [w[n].shape for n in _SMALL] + [(1,)]
    small = _pack([grad[n] for n in _SMALL] + [jnp.sum(lterms).reshape(1)])
    small_parts = _run_comm(_Gather([small]), "gather_small")[0]

    out_g, out_d, out_m, out_v = {}, {}, {}, {}
    for n in _SHARDED:
        out_g[n], out_d[n], out_m[n], out_v[n] = _adamw_sharded(
            [received[l][n] for l in range(DEPTH)], w[n], m[n], v[n], "adamw_" + n)
    zero = jnp.zeros((1,), F32)
    res = _adamw_small(small_parts, _pack([w[n] for n in _SMALL] + [zero]), _pack([m[n] for n in _SMALL] + [zero]),
                       _pack([v[n] for n in _SMALL] + [zero]))
    unpacked = [_unpack(r, small_shapes) for r in res]
    for i, n in enumerate(_SMALL):
        out_g[n], out_d[n], out_m[n], out_v[n] = (u[i] for u in unpacked)
    loss = unpacked[0][-1].reshape(())
    return (loss, dx.reshape(x.shape), *[out_g[n] for n in _ORDER], *[out_d[n] for n in _ORDER],
            *[out_m[n] for n in _ORDER], *[out_v[n] for n in _ORDER])
```
